```python
import jax, jax.numpy as jnp
from jax import lax
import numpy as np

D_MODEL = 1024
BATCH = 4
SEQ = 8192
DEPTH = 1

GRID_W = 64
CTX_LEN = 256
A_WIDTH = 1024
A_HEAD_DIM = 128
A_HEADS = A_WIDTH // A_HEAD_DIM
A_CHUNK = 64
B_WIDTH = 1024
B_BLOCKS = 8
B_BLOCK_DIM = B_WIDTH // B_BLOCKS
B_CONV = 4
RG_C = 8.0
N_BRANCH = 2
IN_COLS = 5 * A_WIDTH + 2 * B_WIDTH + N_BRANCH * D_MODEL
DEEPNORM_ALPHA = (2 * DEPTH) ** 0.25
DEEPNORM_BETA = (8 * DEPTH) ** -0.25
LN_EPS = 1e-5
RMS_EPS = 1e-6

kernel_name = "hgrn2_rglru_gated_hybrid_dit"


def _in_split_points():
    sizes = [A_WIDTH] * 5 + [B_WIDTH] * 2 + [D_MODEL] * N_BRANCH
    return [int(v) for v in np.cumsum(sizes)[:-1]]


def layer_norm(t, g, b):
    tf = t.astype(jnp.float32)
    mu = jnp.mean(tf, axis=-1, keepdims=True)
    var = jnp.mean(jnp.square(tf - mu), axis=-1, keepdims=True)
    return ((tf - mu) * lax.rsqrt(var + LN_EPS) * g.astype(jnp.float32) + b.astype(jnp.float32)).astype(t.dtype)


def rms_norm(t, g):
    tf = t.astype(jnp.float32)
    return tf * lax.rsqrt(jnp.mean(jnp.square(tf), axis=-1, keepdims=True) + RMS_EPS) * g.astype(jnp.float32)


def to_heads(t):
    b, l, _ = t.shape
    return t.reshape(b, l, A_HEADS, A_HEAD_DIM).transpose(0, 2, 1, 3)


def from_heads(t):
    b, h, l, d = t.shape
    return t.transpose(0, 2, 1, 3).reshape(b, l, h * d)


def grid_to_colmajor(t, rows):
    b, _, ch = t.shape
    return t.reshape(b, rows, GRID_W, ch).transpose(0, 2, 1, 3)


def colmajor_to_grid(t):
    b, w, r, ch = t.shape
    return t.transpose(0, 2, 1, 3).reshape(b, r * w, ch)


def gla_chunkwise(q, k, v, logf, s0):
    b, h, t, dk = q.shape
    dv = v.shape[-1]
    n = t // A_CHUNK
    q = q.reshape(b, h, n, A_CHUNK, dk)
    k = k.reshape(b, h, n, A_CHUNK, dk)
    v = v.reshape(b, h, n, A_CHUNK, dv)
    g = jnp.cumsum(logf.reshape(b, h, n, A_CHUNK, dk), axis=-2)
    g_last = g[..., -1:, :]
    q_dec = q * jnp.exp(g)
    k_inv = k * jnp.exp(-g)
    k_end = k * jnp.exp(g_last - g)
    mask = jnp.tril(jnp.ones((A_CHUNK, A_CHUNK), dtype=bool))
    scores = jnp.where(mask, jnp.einsum('bhnck,bhnsk->bhncs', q_dec, k_inv), 0.0)
    o_intra = jnp.einsum('bhncs,bhnsv->bhncv', scores, v)
    u = jnp.einsum('bhnsk,bhnsv->bhnkv', k_end, v)
    decay = jnp.exp(g_last[..., 0, :])

    def step(s, inp):
        d_n, u_n = inp
        return d_n[..., None] * s + u_n, s

    s_fin, s_start = lax.scan(step, s0, (jnp.moveaxis(decay, 2, 0), jnp.moveaxis(u, 2, 0)))
    s_start = jnp.moveaxis(s_start, 0, 2)
    o_inter = jnp.einsum('bhnck,bhnkv->bhncv', q_dec, s_start)
    return (o_intra + o_inter).reshape(b, h, t, dv), s_fin


def gla_prefixed(ctx_in, lat_in, reverse):
    if reverse:
        ctx_in = tuple(jnp.flip(a, axis=2) for a in ctx_in)
        lat_in = tuple(jnp.flip(a, axis=2) for a in lat_in)
    b, h, _, dk = ctx_in[0].shape
    dv = ctx_in[2].shape[-1]
    s0 = jnp.zeros((b, h, dk, dv), jnp.float32)
    o_c, s_c = gla_chunkwise(*ctx_in, s0)
    o_x, _ = gla_chunkwise(*lat_in, s_c)
    if reverse:
        o_c, o_x = jnp.flip(o_c, axis=2), jnp.flip(o_x, axis=2)
    return o_c, o_x


def hgrn2_features(z, lb):
    f32 = jnp.float32
    q = to_heads(jax.nn.silu(z[0].astype(f32)) * (A_HEAD_DIM ** -0.5))
    v = to_heads(z[3].astype(f32))
    f_fwd = lb[0] + (1.0 - lb[0]) * jax.nn.sigmoid(z[1].astype(f32))
    f_bwd = lb[1] + (1.0 - lb[1]) * jax.nn.sigmoid(z[2].astype(f32))
    fwd = (q, to_heads(1.0 - f_fwd), v, to_heads(jnp.log(f_fwd)))
    bwd = (q, to_heads(1.0 - f_bwd), v, to_heads(jnp.log(f_bwd)))
    return fwd, bwd


def centred_dwconv(t, w, bias):
    lo = (B_CONV - 1) // 2
    hi = B_CONV - 1 - lo
    n = t.shape[-2]
    tp = jnp.pad(t, [(0, 0)] * (t.ndim - 2) + [(lo, hi), (0, 0)])
    out = bias + tp[..., 0:n, :] * w[0]
    for kk in range(1, B_CONV):
        out = out + tp[..., kk:kk + n, :] * w[kk]
    return out


def rglru_gates(xc, w_r, b_r, w_i, b_i, lam):
    b, l, ch = xc.shape
    xb = xc.reshape(b, l, B_BLOCKS, B_BLOCK_DIM)
    r = jax.nn.sigmoid(jnp.einsum('blgi,gij->blgj', xb, w_r).reshape(b, l, ch) + b_r)
    i = jax.nn.sigmoid(jnp.einsum('blgi,gij->blgj', xb, w_i).reshape(b, l, ch) + b_i)
    log_a = -RG_C * r * jax.nn.softplus(-lam)
    a = jnp.exp(log_a)
    mult = jnp.sqrt(-jnp.expm1(2.0 * log_a))
    return a, mult * (i * xc)


def linear_scan(a, bterm, h0):
    bterm = bterm.at[:, 0].add(a[:, 0] * h0)

    def comb(left, right):
        al, bl = left
        ar, br = right
        return al * ar, ar * bl + br

    _, h = lax.associative_scan(comb, (a, bterm), axis=1)
    return h


def rglru_prefixed(xc_c, xc_x, params, reverse):
    if reverse:
        xc_c, xc_x = jnp.flip(xc_c, axis=1), jnp.flip(xc_x, axis=1)
    a_c, b_c = rglru_gates(xc_c, *params)
    h_c = linear_scan(a_c, b_c, jnp.zeros((xc_c.shape[0], B_WIDTH), jnp.float32))
    a_x, b_x = rglru_gates(xc_x, *params)
    h_x = linear_scan(a_x, b_x, h_c[:, -1])
    if reverse:
        h_c, h_x = jnp.flip(h_c, axis=1), jnp.flip(h_x, axis=1)
    return h_c, h_x


def hybrid_layer(x, ctx, c, c_ctx, w_mod, b_mod, w_in, b_in, lb, norm_a_g, conv_w, conv_b,
                 w_r, b_r, w_i, b_i, lam, p_a, p_b, w_out, ln_g, ln_b, last):
    f32 = jnp.float32
    bsz, t, _ = x.shape
    rows = t // GRID_W
    mod_x = jax.nn.silu(c) @ w_mod + b_mod
    mod_c = jax.nn.silu(c_ctx) @ w_mod + b_mod
    sh_x, sc_x, gt_x = jnp.split(mod_x[:, None, :], 3, axis=-1)
    sh_c, sc_c, gt_c = jnp.split(mod_c, 3, axis=-1)
    u_x = x * (1.0 + sc_x) + sh_x
    u_c = ctx * (1.0 + sc_c) + sh_c
    splits = _in_split_points()
    z_x = jnp.split(u_x @ w_in + b_in, splits, axis=-1)
    z_c = jnp.split(u_c @ w_in + b_in, splits, axis=-1)

    fx, bx = hgrn2_features(z_x, lb)
    fc, bc = hgrn2_features(z_c, lb)
    oc_f, ox_f = gla_prefixed(fc, fx, False)
    oc_b, ox_b = gla_prefixed(bc, bx, True)

    xc_x = centred_dwconv(grid_to_colmajor(z_x[5], rows), conv_w, conv_b).astype(f32).reshape(bsz, t, B_WIDTH)
    xc_c = centred_dwconv(z_c[5], conv_w, conv_b).astype(f32)
    hc_f, hx_f = rglru_prefixed(xc_c, xc_x, (w_r[0], b_r[0], w_i[0], b_i[0], lam[0]), False)
    hc_b, hx_b = rglru_prefixed(xc_c, xc_x, (w_r[1], b_r[1], w_i[1], b_i[1], lam[1]), True)
    hx = colmajor_to_grid((hx_f + hx_b).reshape(bsz, GRID_W, rows, B_WIDTH))

    def merge(z, o_a, h_b):
        o_a = from_heads(rms_norm(o_a, norm_a_g)) * jax.nn.silu(z[4].astype(f32))
        o_b = h_b * jax.nn.silu(z[6].astype(f32))
        y = jax.nn.sigmoid(z[7]) * (o_a @ p_a) + jax.nn.sigmoid(z[8]) * (o_b @ p_b)
        return y @ w_out

    x_new = layer_norm(DEEPNORM_ALPHA * x + gt_x * merge(z_x, ox_f + ox_b, hx), ln_g, ln_b)
    if last:
        return x_new, ctx
    ctx_new = layer_norm(DEEPNORM_ALPHA * ctx + gt_c * merge(z_c, oc_f + oc_b, hc_f + hc_b), ln_g, ln_b)
    return x_new, ctx_new


def setup_inputs(seed: int = 0) -> dict:
    key = jax.random.key(seed)
    ks = jax.random.split(key, 24)
    f32 = jnp.float32
    n = lambda k, s, sc: jax.random.normal(k, s, f32) * sc
    u_a = jax.random.uniform(ks[17], (DEPTH, 2, B_WIDTH), f32, 0.9, 0.999)
    s_a = u_a ** (1.0 / RG_C)
    return {
        "x": n(ks[0], (BATCH, SEQ, D_MODEL), 1.0),
        "c": n(ks[1], (BATCH, D_MODEL), 1.0),
        "ctx": n(ks[2], (BATCH, CTX_LEN, D_MODEL), 1.0),
        "c_ctx": n(ks[3], (D_MODEL,), 1.0),
        "w_mod": n(ks[4], (DEPTH, D_MODEL, 3 * D_MODEL), 0.5 * D_MODEL ** -0.5),
        "b_mod": n(ks[5], (DEPTH, 3 * D_MODEL), 0.01),
        "w_in": n(ks[6], (DEPTH, D_MODEL, IN_COLS), D_MODEL ** -0.5),
        "b_in": n(ks[7], (DEPTH, IN_COLS), 0.01),
        "lb_logits": n(ks[8], (DEPTH + 1, 2, A_WIDTH), 0.1),
        "norm_a_g": 1.0 + n(ks[9], (DEPTH, A_HEAD_DIM), 0.01),
        "conv_w": n(ks[10], (DEPTH, B_CONV, B_WIDTH), B_CONV ** -0.5),
        "conv_b": n(ks[11], (DEPTH, B_WIDTH), 0.01),
        "w_r": n(ks[12], (DEPTH, 2, B_BLOCKS, B_BLOCK_DIM, B_BLOCK_DIM), B_BLOCK_DIM ** -0.5),
        "b_r": n(ks[13], (DEPTH, 2, B_WIDTH), 0.01),
        "w_i": n(ks[14], (DEPTH, 2, B_BLOCKS, B_BLOCK_DIM, B_BLOCK_DIM), B_BLOCK_DIM ** -0.5),
        "b_i": n(ks[15], (DEPTH, 2, B_WIDTH), 0.01),
        "lam": jnp.log(s_a) - jnp.log1p(-s_a),
        "p_a": n(ks[18], (DEPTH, A_WIDTH, D_MODEL), DEEPNORM_BETA * A_WIDTH ** -0.5),
        "p_b": n(ks[19], (DEPTH, B_WIDTH, D_MODEL), DEEPNORM_BETA * B_WIDTH ** -0.5),
        "w_out": n(ks[20], (DEPTH, D_MODEL, D_MODEL), DEEPNORM_BETA * D_MODEL ** -0.5),
        "ln_g": 1.0 + n(ks[21], (DEPTH, D_MODEL), 0.01),
        "ln_b": n(ks[22], (DEPTH, D_MODEL), 0.01),
    }


def reference(x, c, ctx, c_ctx, w_mod, b_mod, w_in, b_in, lb_logits, norm_a_g, conv_w, conv_b,
              w_r, b_r, w_i, b_i, lam, p_a, p_b, w_out, ln_g, ln_b):
    lb_all = jnp.cumsum(jax.nn.softmax(lb_logits.astype(jnp.float32), axis=0), axis=0)
    for layer in range(DEPTH):
        x, ctx = hybrid_layer(
            x, ctx, c, c_ctx, w_mod[layer], b_mod[layer], w_in[layer], b_in[layer], lb_all[layer],
            norm_a_g[layer], conv_w[layer], conv_b[layer], w_r[layer], b_r[layer], w_i[layer], b_i[layer],
            lam[layer], p_a[layer], p_b[layer], w_out[layer], ln_g[layer], ln_b[layer],
            last=(layer == DEPTH - 1))
    return x
```

```python
import functools

import jax
import jax.numpy as jnp
from jax import lax
from jax.experimental import pallas as pl
from jax.experimental.pallas import tpu as pltpu

F32 = jnp.float32
BF16 = jnp.bfloat16

D_MODEL = 1024
GRID_W = 64
HEAD_DIM = 128
N_HEADS = D_MODEL // HEAD_DIM
N_BLOCKS = 8
BLOCK_DIM = D_MODEL // N_BLOCKS
N_CONV = 4
RG_C = 8.0
LN_EPS = 1e-5
RMS_EPS = 1e-6
DEEPNORM_ALPHA = 2.0 ** 0.25
Q_SCALE = HEAD_DIM ** -0.5

VMEM_LIMIT = 56 * 1024 * 1024

OPS_LATENT = ("q", "logf0", "logf1", "id", "silu", "id", "silu", "sig", "sig")
COLS_CTX = (1, 2, 3, 5)
OPS_CTX = ("logf0", "logf1", "id", "id")


def _sigmoid(x):
    return jax.nn.sigmoid(x)


def _dot(a, b):
    return jnp.dot(a, b, preferred_element_type=F32)


def _dot_nt(a, b):
    return lax.dot_general(a, b, (((1,), (1,)), ((), ())), preferred_element_type=F32)


def _dot_tn(a, b):
    return lax.dot_general(a, b, (((0,), (0,)), ((), ())), preferred_element_type=F32)


def _cumsum_rows(mask_bf16, x):
    hi = x.astype(BF16)
    r1 = x - hi.astype(F32)
    mid = r1.astype(BF16)
    lo = (r1 - mid.astype(F32)).astype(BF16)
    return _dot(mask_bf16, hi) + _dot(mask_bf16, mid) + _dot(mask_bf16, lo)


def _mod_kernel(c_ref, w_ref, b_ref, lbl_ref, mod_ref, lb_ref):
    c = c_ref[...]
    s = c * _sigmoid(c)
    mod_ref[...] = _dot(s, w_ref[...]) + b_ref[...]
    l = lbl_ref[...]
    e = jnp.exp(l - jnp.max(l, axis=0, keepdims=True))
    lb_ref[...] = e[0] / jnp.sum(e, axis=0)


def _mod_call(c8, w_mod, b_mod, lb_logits):
    return pl.pallas_call(
        _mod_kernel,
        out_shape=(jax.ShapeDtypeStruct((8, 3 * D_MODEL), F32),
                   jax.ShapeDtypeStruct((2, D_MODEL), F32)),
        compiler_params=pltpu.CompilerParams(vmem_limit_bytes=VMEM_LIMIT),
        name="mod",
    )(c8, w_mod, b_mod, lb_logits)


def _inproj_kernel(x_ref, mod_ref, w_ref, b_ref, lb_ref, o_ref, u_ref, *, ops):
    j = pl.program_id(1)

    @pl.when(j == 0)
    def _():
        sh = mod_ref[0:1, :]
        sc = mod_ref[1:2, :]
        u_ref[...] = (x_ref[...] * (1.0 + sc) + sh).astype(BF16)

    z = _dot(u_ref[...], w_ref[j]) + b_ref[j]

    def cols(kind):
        pred = None
        for idx, op in enumerate(ops):
            if op == kind:
                p = j == idx
                pred = p if pred is None else (pred | p)
        return pred

    if "q" in ops:
        @pl.when(cols("q"))
        def _():
            o_ref[...] = z * _sigmoid(z) * Q_SCALE

    for d in (0, 1):
        if f"logf{d}" in ops:
            @pl.when(cols(f"logf{d}"))
            def _(d=d):
                lb = lb_ref[d:d + 1, :]
                o_ref[...] = jnp.log(lb + (1.0 - lb) * _sigmoid(z))

    if "id" in ops:
        @pl.when(cols("id"))
        def _():
            o_ref[...] = z

    if "silu" in ops:
        @pl.when(cols("silu"))
        def _():
            o_ref[...] = z * _sigmoid(z)

    if "sig" in ops:
        @pl.when(cols("sig"))
        def _():
            o_ref[...] = _sigmoid(z)


def _inproj_call(x2, mod3, w3, b3, lb, *, ops, tm, tiles_per_mod):
    n_tok = x2.shape[0]
    n_col = len(ops)
    return pl.pallas_call(
        functools.partial(_inproj_kernel, ops=ops),
        grid=(n_tok // tm, n_col),
        in_specs=[
            pl.BlockSpec((tm, D_MODEL), lambda i, j: (i, 0)),
            pl.BlockSpec((None, 3, D_MODEL), lambda i, j: (i // tiles_per_mod, 0, 0)),
            pl.BlockSpec((n_col, D_MODEL, D_MODEL), lambda i, j: (0, 0, 0),
                         pipeline_mode=pl.Buffered(1)),
            pl.BlockSpec((n_col, 1, D_MODEL), lambda i, j: (0, 0, 0)),
            pl.BlockSpec((2, D_MODEL), lambda i, j: (0, 0)),
        ],
        out_specs=pl.BlockSpec((None, tm, D_MODEL), lambda i, j: (j, i, 0)),
        out_shape=jax.ShapeDtypeStruct((n_col, n_tok, D_MODEL), F32),
        scratch_shapes=[pltpu.VMEM((tm, D_MODEL), BF16)],
        compiler_params=pltpu.CompilerParams(
            dimension_semantics=("arbitrary", "arbitrary"), vmem_limit_bytes=VMEM_LIMIT),
        name="inproj",
    )(x2, mod3, w3, b3, lb)


def _lru_ab(xc, wg, br, bi, sp):
    g = _dot(xc.astype(BF16), wg)
    r = _sigmoid(g[:, :BLOCK_DIM] + br)
    i = _sigmoid(g[:, BLOCK_DIM:] + bi)
    log_a = (-RG_C) * sp * r
    a = jnp.exp(log_a)
    mult = jnp.sqrt(1.0 - jnp.exp(2.0 * log_a))
    return a, mult * (i * xc)


def _softplus(y):
    return jnp.maximum(y, 0.0) + jnp.log(1.0 + jnp.exp(-jnp.abs(y)))


def _ctx_kernel(lff_ref, lfb_ref, v_ref, z5_ref, cw_ref, cb_ref, wg_ref, br_ref, bi_ref, lam_ref,
                sf_ref, sb_ref, hf_ref, hb_ref, zp_ref, a_ref, b_ref):
    n = lff_ref.shape[0]
    ri = lax.broadcasted_iota(jnp.int32, (n, n), 0)
    ci = lax.broadcasted_iota(jnp.int32, (n, n), 1)
    tril = (ci <= ri).astype(F32).astype(BF16)
    v = v_ref[...].astype(BF16)

    lf = lff_ref[...]
    g = _cumsum_rows(tril, lf)
    ke = ((1.0 - jnp.exp(lf)) * jnp.exp(g[n - 1:n, :] - g)).astype(BF16)
    for h in range(N_HEADS):
        sl = slice(h * HEAD_DIM, (h + 1) * HEAD_DIM)
        sf_ref[h] = _dot_tn(v[:, sl], ke[:, sl])
    lf = lfb_ref[...]
    g = _cumsum_rows(tril, lf)
    ke = ((1.0 - jnp.exp(lf)) * jnp.exp(g - lf)).astype(BF16)
    for h in range(N_HEADS):
        sl = slice(h * HEAD_DIM, (h + 1) * HEAD_DIM)
        sb_ref[h] = _dot_tn(v[:, sl], ke[:, sl])

    zp_ref[0:8, :] = jnp.zeros((8, D_MODEL), F32)
    zp_ref[8 + n:16 + n, :] = jnp.zeros((8, D_MODEL), F32)
    zp_ref[8:8 + n, :] = z5_ref[...]
    xc = cb_ref[...] + zp_ref[7:7 + n, :] * cw_ref[0:1, :]
    for kk in range(1, N_CONV):
        xc = xc + zp_ref[7 + kk:7 + kk + n, :] * cw_ref[kk:kk + 1, :]

    for d, h_ref in ((0, hf_ref), (1, hb_ref)):
        sp = _softplus(-lam_ref[d:d + 1, :])
        for blk in range(N_BLOCKS):
            sl = slice(blk * BLOCK_DIM, (blk + 1) * BLOCK_DIM)
            a, b = _lru_ab(xc[:, sl], wg_ref[d, blk], br_ref[d:d + 1, sl], bi_ref[d:d + 1, sl], sp[:, sl])
            a_ref[:, sl] = a
            b_ref[:, sl] = b

        def step(t, h, d=d):
            tt = (n - 1 - t) if d == 1 else t
            return a_ref[pl.ds(tt, 1), :] * h + b_ref[pl.ds(tt, 1), :]

        h_ref[...] = lax.fori_loop(0, n, step, jnp.zeros((1, D_MODEL), F32))


def _ctx_call(feat_c, conv_w, conv_b, wg, br, bi, lam, *, batch, n_ctx):
    feat = lambda j: pl.BlockSpec((None, n_ctx, D_MODEL), lambda b, j=j: (j, b, 0))
    full = lambda shape: pl.BlockSpec(shape, lambda b: (0,) * len(shape))
    state = pl.BlockSpec((None, N_HEADS, HEAD_DIM, HEAD_DIM), lambda b: (b, 0, 0, 0))
    hvec = pl.BlockSpec((None, 1, D_MODEL), lambda b: (b, 0, 0))
    return pl.pallas_call(
        _ctx_kernel,
        grid=(batch,),
        in_specs=[feat(0), feat(1), feat(2), feat(3),
                  full((N_CONV, D_MODEL)), full((1, D_MODEL)),
                  full((2, N_BLOCKS, BLOCK_DIM, 2 * BLOCK_DIM)),
                  full((2, D_MODEL)), full((2, D_MODEL)), full((2, D_MODEL))],
        out_specs=(state, state, hvec, hvec),
        out_shape=(jax.ShapeDtypeStruct((batch, N_HEADS, HEAD_DIM, HEAD_DIM), F32),
                   jax.ShapeDtypeStruct((batch, N_HEADS, HEAD_DIM, HEAD_DIM), F32),
                   jax.ShapeDtypeStruct((batch, 1, D_MODEL), F32),
                   jax.ShapeDtypeStruct((batch, 1, D_MODEL), F32)),
        scratch_shapes=[pltpu.VMEM((n_ctx + 16, D_MODEL), F32),
                        pltpu.VMEM((n_ctx, D_MODEL), F32),
                        pltpu.VMEM((n_ctx, D_MODEL), F32)],
        compiler_params=pltpu.CompilerParams(
            dimension_semantics=("arbitrary",), vmem_limit_bytes=VMEM_LIMIT),
        name="ctx_states",
    )(feat_c, feat_c, feat_c, feat_c, conv_w, conv_b, wg, br, bi, lam)


def _gla_kernel(q_ref, lf_ref, v_ref, s0_ref, o_ref, st_ref, *, reverse, rows):
    @pl.when(pl.program_id(1) == 0)
    def _():
        st_ref[...] = s0_ref[...]

    ri = lax.broadcasted_iota(jnp.int32, (GRID_W, GRID_W), 0)
    ci = lax.broadcasted_iota(jnp.int32, (GRID_W, GRID_W), 1)
    allow = (ci >= ri) if reverse else (ci <= ri)
    mask = allow.astype(F32).astype(BF16)
    edge = 0 if reverse else GRID_W - 1

    def body(n, carry):
        r = (rows - 1 - n) if reverse else n
        off = pl.multiple_of(r * GRID_W, GRID_W)
        q = q_ref[pl.ds(off, GRID_W), :]
        lf = lf_ref[pl.ds(off, GRID_W), :]
        v = v_ref[pl.ds(off, GRID_W), :].astype(BF16)
        g = _cumsum_rows(mask, lf)
        g_all = g[edge:edge + 1, :]
        k = 1.0 - jnp.exp(lf)
        qd = (q * jnp.exp(g)).astype(BF16)
        ki = (k * jnp.exp(-g)).astype(BF16)
        ke = (k * jnp.exp(g_all - g)).astype(BF16)
        dec = jnp.exp(g_all)
        for h in range(N_HEADS):
            sl = slice(h * HEAD_DIM, (h + 1) * HEAD_DIM)
            st = st_ref[h]
            p = jnp.where(allow, _dot_nt(qd[:, sl], ki[:, sl]), 0.0).astype(BF16)
            o_ref[pl.ds(off, GRID_W), sl] = _dot(p, v[:, sl]) + _dot_nt(qd[:, sl], st.astype(BF16))
            st_ref[h] = st * dec[:, sl] + _dot_tn(v[:, sl], ke[:, sl])
        return carry

    lax.fori_loop(0, rows, body, 0)


def _gla_call(feat, lf_col, s0, *, reverse, batch, seq, rows):
    tb = rows * GRID_W
    nrb = seq // tb

    def tok(col):
        def index(b, i, col=col):
            ii = (nrb - 1 - i) if reverse else i
            return (col, b * nrb + ii, 0)
        return pl.BlockSpec((None, tb, D_MODEL), index)

    def out_index(b, i):
        ii = (nrb - 1 - i) if reverse else i
        return (b * nrb + ii, 0)

    return pl.pallas_call(
        functools.partial(_gla_kernel, reverse=reverse, rows=rows),
        grid=(batch, nrb),
        in_specs=[tok(0), tok(lf_col), tok(3),
                  pl.BlockSpec((None, N_HEADS, HEAD_DIM, HEAD_DIM), lambda b, i: (b, 0, 0, 0))],
        out_specs=pl.BlockSpec((tb, D_MODEL), out_index),
        out_shape=jax.ShapeDtypeStruct((batch * seq, D_MODEL), F32),
        scratch_shapes=[pltpu.VMEM((N_HEADS, HEAD_DIM, HEAD_DIM), F32)],
        compiler_params=pltpu.CompilerParams(
            dimension_semantics=("arbitrary", "arbitrary"), vmem_limit_bytes=VMEM_LIMIT),
        name="gla_bwd" if reverse else "gla_fwd",
    )(feat, feat, feat, s0)


def _lru_kernel(z5_ref, cw_ref, cb_ref, wg_ref, br_ref, bi_ref, lam_ref, h0f_ref, h0b_ref, o_ref,
                zp_ref, xc_ref, a_ref, b_ref, car_ref, *, n_rows, rows_per_step):
    w = GRID_W
    seq = n_rows * w
    tb = rows_per_step * w
    n_steps = n_rows // rows_per_step

    zp_ref[0:w, :] = jnp.zeros((w, BLOCK_DIM), F32)
    zp_ref[w + seq:, :] = jnp.zeros((2 * w, BLOCK_DIM), F32)

    def copy_in(s, c):
        off = pl.multiple_of(s * tb, tb)
        zp_ref[pl.ds(w + off, tb), :] = z5_ref[pl.ds(off, tb), :]
        return c

    lax.fori_loop(0, n_steps, copy_in, 0)

    def conv(s, c):
        off = pl.multiple_of(s * tb, tb)
        acc = cb_ref[...] + zp_ref[pl.ds(off, tb), :] * cw_ref[0:1, :]
        for kk in range(1, N_CONV):
            acc = acc + zp_ref[pl.ds(off + kk * w, tb), :] * cw_ref[kk:kk + 1, :]
        xc_ref[pl.ds(off, tb), :] = acc
        return c

    lax.fori_loop(0, n_steps, conv, 0)

    for d in (0, 1):
        reverse = d == 1
        sp = _softplus(-lam_ref[d:d + 1, :])
        wg = wg_ref[d]
        br = br_ref[d:d + 1, :]
        bi = bi_ref[d:d + 1, :]

        def gates(s, c):
            off = pl.multiple_of(s * tb, tb)
            a, b = _lru_ab(xc_ref[pl.ds(off, tb), :], wg, br, bi, sp)
            a_ref[pl.ds(off, tb), :] = a
            b_ref[pl.ds(off, tb), :] = b
            return c

        lax.fori_loop(0, n_steps, gates, 0)

        def scan(n, hp):
            h, p = hp
            r = (n_rows - 1 - n) if reverse else n
            off = pl.multiple_of(r * w, w)
            a = a_ref[pl.ds(off, w), :]
            h = a * h + b_ref[pl.ds(off, w), :]
            p = p * a
            b_ref[pl.ds(off, w), :] = h
            a_ref[pl.ds(off, w), :] = p
            return h, p

        lax.fori_loop(0, n_rows, scan,
                      (jnp.zeros((w, BLOCK_DIM), F32), jnp.ones((w, BLOCK_DIM), F32)))

        end = 0 if reverse else (n_rows - 1) * w

        def carry(n, c):
            col = (w - 1 - n) if reverse else n
            car_ref[pl.ds(col, 1), :] = c
            return b_ref[pl.ds(end + col, 1), :] + a_ref[pl.ds(end + col, 1), :] * c

        lax.fori_loop(0, w, carry, (h0b_ref if reverse else h0f_ref)[...])

        def fix(s, c):
            off = pl.multiple_of(s * tb, tb)
            car = jnp.concatenate([car_ref[...]] * rows_per_step, axis=0)
            hloc = b_ref[pl.ds(off, tb), :] + a_ref[pl.ds(off, tb), :] * car
            if reverse:
                o_ref[pl.ds(off, tb), :] = o_ref[pl.ds(off, tb), :] + hloc
            else:
                o_ref[pl.ds(off, tb), :] = hloc
            return c

        lax.fori_loop(0, n_steps, fix, 0)


def _lru_call(feat, conv_w, conv_b, wg, br, bi, lam, h0f, h0b, *, batch, seq):
    n_rows = seq // GRID_W
    vec = lambda n: pl.BlockSpec((n, BLOCK_DIM), lambda b, c: (0, c))
    return pl.pallas_call(
        functools.partial(_lru_kernel, n_rows=n_rows, rows_per_step=8),
        grid=(batch, N_BLOCKS),
        in_specs=[pl.BlockSpec((None, seq, BLOCK_DIM), lambda b, c: (5, b, c)),
                  vec(N_CONV), vec(1),
                  pl.BlockSpec((2, None, BLOCK_DIM, 2 * BLOCK_DIM), lambda b, c: (0, c, 0, 0)),
                  vec(2), vec(2), vec(2),
                  pl.BlockSpec((None, 1, BLOCK_DIM), lambda b, c: (b, 0, c)),
                  pl.BlockSpec((None, 1, BLOCK_DIM), lambda b, c: (b, 0, c))],
        out_specs=pl.BlockSpec((seq, BLOCK_DIM), lambda b, c: (b, c)),
        out_shape=jax.ShapeDtypeStruct((batch * seq, D_MODEL), F32),
        scratch_shapes=[pltpu.VMEM((seq + 3 * GRID_W, BLOCK_DIM), F32),
                        pltpu.VMEM((seq, BLOCK_DIM), F32),
                        pltpu.VMEM((seq, BLOCK_DIM), F32),
                        pltpu.VMEM((seq, BLOCK_DIM), F32),
                        pltpu.VMEM((GRID_W, BLOCK_DIM), F32)],
        compiler_params=pltpu.CompilerParams(
            dimension_semantics=("arbitrary", "arbitrary"), vmem_limit_bytes=VMEM_LIMIT),
        name="lru",
    )(feat, conv_w, conv_b, wg, br, bi, lam, h0f, h0b)


def _merge_kernel(of_ref, ob_ref, hx_ref, g4_ref, g6_ref, m7_ref, m8_ref, x_ref, mod_ref,
                  ng_ref, pa_ref, pb_ref, wo_ref, lg_ref, lbias_ref, o_ref):
    oa = of_ref[...] + ob_ref[...]
    g4 = g4_ref[...]
    parts = []
    for h in range(N_HEADS):
        sl = slice(h * HEAD_DIM, (h + 1) * HEAD_DIM)
        t = oa[:, sl]
        ms = jnp.mean(t * t, axis=-1, keepdims=True)
        parts.append(t * lax.rsqrt(ms + RMS_EPS) * ng_ref[...] * g4[:, sl])
    o_a = jnp.concatenate(parts, axis=-1).astype(BF16)
    o_b = (hx_ref[...] * g6_ref[...]).astype(BF16)
    y = m7_ref[...] * _dot(o_a, pa_ref[...]) + m8_ref[...] * _dot(o_b, pb_ref[...])
    y = _dot(y.astype(BF16), wo_ref[...])
    t = DEEPNORM_ALPHA * x_ref[...] + mod_ref[2:3, :] * y
    mu = jnp.mean(t, axis=-1, keepdims=True)
    tc = t - mu
    var = jnp.mean(tc * tc, axis=-1, keepdims=True)
    o_ref[...] = tc * lax.rsqrt(var + LN_EPS) * lg_ref[...] + lbias_ref[...]


def _merge_call(o_f, o_b, hx, feat, x2, mod3, ng, pa, pb, wo, lg, lbias, *, tm, tiles_per_batch):
    n_tok = x2.shape[0]
    tok = pl.BlockSpec((tm, D_MODEL), lambda i: (i, 0))
    col = lambda j: pl.BlockSpec((None, tm, D_MODEL), lambda i, j=j: (j, i, 0))
    full = lambda shape: pl.BlockSpec(shape, lambda i: (0,) * len(shape))
    return pl.pallas_call(
        _merge_kernel,
        grid=(n_tok // tm,),
        in_specs=[tok, tok, tok, col(4), col(6), col(7), col(8), tok,
                  pl.BlockSpec((None, 3, D_MODEL), lambda i: (i // tiles_per_batch, 0, 0)),
                  full((1, HEAD_DIM)), full((D_MODEL, D_MODEL)), full((D_MODEL, D_MODEL)),
                  full((D_MODEL, D_MODEL)), full((1, D_MODEL)), full((1, D_MODEL))],
        out_specs=tok,
        out_shape=jax.ShapeDtypeStruct((n_tok, D_MODEL), F32),
        compiler_params=pltpu.CompilerParams(
            dimension_semantics=("arbitrary",), vmem_limit_bytes=VMEM_LIMIT),
        name="merge",
    )(o_f, o_b, hx, feat, feat, feat, feat, x2, mod3, ng, pa, pb, wo, lg, lbias)


def kernel(x, c, ctx, c_ctx, w_mod, b_mod, w_in, b_in, lb_logits, norm_a_g, conv_w, conv_b,
           w_r, b_r, w_i, b_i, lam, p_a, p_b, w_out, ln_g, ln_b):
    batch, seq, d = x.shape
    n_ctx = ctx.shape[1]
    assert d == D_MODEL and seq % GRID_W == 0 and w_in.shape[0] == 1

    n_cols = w_in.shape[-1] // D_MODEL
    w3 = w_in[0].reshape(D_MODEL, n_cols, D_MODEL).transpose(1, 0, 2).astype(BF16)
    b3 = b_in[0].reshape(n_cols, 1, D_MODEL)
    cols_c = jnp.array(COLS_CTX)
    wg = jnp.concatenate([w_r[0], w_i[0]], axis=-1).astype(BF16)
    c8 = jnp.zeros((8, D_MODEL), F32).at[:batch].set(c).at[batch].set(c_ctx)

    mod, lb = _mod_call(c8, w_mod[0], b_mod[0][None, :], lb_logits)
    mod3 = mod.reshape(8, 3, D_MODEL)

    x2 = x.reshape(batch * seq, D_MODEL)
    feat = _inproj_call(x2, mod3, w3, b3, lb, ops=OPS_LATENT, tm=1024, tiles_per_mod=seq // 1024)
    feat_c = _inproj_call(ctx.reshape(batch * n_ctx, D_MODEL), mod3[batch:batch + 1],
                          w3[cols_c], b3[cols_c], lb, ops=OPS_CTX, tm=n_ctx,
                          tiles_per_mod=batch)

    s0f, s0b, h0f, h0b = _ctx_call(feat_c, conv_w[0], conv_b[0][None, :], wg, b_r[0], b_i[0], lam[0],
                                   batch=batch, n_ctx=n_ctx)

    o_f = _gla_call(feat, 1, s0f, reverse=False, batch=batch, seq=seq, rows=8)
    o_b = _gla_call(feat, 2, s0b, reverse=True, batch=batch, seq=seq, rows=8)
    hx = _lru_call(feat, conv_w[0], conv_b[0][None, :], wg, b_r[0], b_i[0], lam[0], h0f, h0b,
                   batch=batch, seq=seq)

    out = _merge_call(o_f, o_b, hx, feat, x2, mod3, norm_a_g[0][None, :],
                      p_a[0].astype(BF16), p_b[0].astype(BF16), w_out[0].astype(BF16),
                      ln_g[0][None, :], ln_b[0][None, :], tm=256, tiles_per_batch=seq // 256)
    return out.reshape(batch, seq, D_MODEL)
```

```python
import functools

import jax
import jax.numpy as jnp
from jax import lax
from jax.experimental import pallas as pl
from jax.experimental.pallas import tpu as pltpu

F32 = jnp.float32
BF16 = jnp.bfloat16

D_MODEL = 1024
GRID_W = 64
HEAD_DIM = 128
N_HEADS = D_MODEL // HEAD_DIM
N_BLOCKS = 8
BLOCK_DIM = D_MODEL // N_BLOCKS
N_CONV = 4
RG_C = 8.0
LN_EPS = 1e-5
RMS_EPS = 1e-6
DEEPNORM_ALPHA = 2.0 ** 0.25
Q_SCALE = HEAD_DIM ** -0.5

VMEM_LIMIT = 56 * 1024 * 1024

LATENT_F32 = (0, 1, 2, 5)
LATENT_BF16 = (3, 4, 6, 7, 8)
CTX_F32 = (1, 2, 3, 5)
F_Q, F_F0, F_F1, F_Z5 = 0, 1, 2, 3
H_V, H_G4, H_G6, H_M7, H_M8 = 0, 1, 2, 3, 4


def _sigmoid(x):
    return 0.5 * jnp.tanh(0.5 * x) + 0.5


def _silu(x):
    h = 0.5 * x
    return h * jnp.tanh(h) + h


def _log_forget(z, lb):
    return jnp.log((0.5 + 0.5 * lb) + (0.5 - 0.5 * lb) * jnp.tanh(0.5 * z))


def _dot(a, b):
    return jnp.dot(a, b, preferred_element_type=F32)


def _dot_nt(a, b):
    return lax.dot_general(a, b, (((1,), (1,)), ((), ())), preferred_element_type=F32)


def _dot_tn(a, b):
    return lax.dot_general(a, b, (((0,), (0,)), ((), ())), preferred_element_type=F32)


def _cumsum_rows(mask_bf16, x):
    hi = x.astype(BF16)
    r1 = x - hi.astype(F32)
    mid = r1.astype(BF16)
    lo = (r1 - mid.astype(F32)).astype(BF16)
    return _dot(mask_bf16, hi) + _dot(mask_bf16, mid) + _dot(mask_bf16, lo)


def _mod_kernel(c_ref, w_ref, b_ref, lbl_ref, mod_ref, lb_ref):
    mod_ref[...] = _dot(_silu(c_ref[...]), w_ref[...]) + b_ref[...]
    l = lbl_ref[...]
    e = jnp.exp(l - jnp.max(l, axis=0, keepdims=True))
    lb_ref[...] = e[0] / jnp.sum(e, axis=0)


def _mod_call(c8, w_mod, b_mod, lb_logits):
    return pl.pallas_call(
        _mod_kernel,
        out_shape=(jax.ShapeDtypeStruct((8, 3 * D_MODEL), F32),
                   jax.ShapeDtypeStruct((2, D_MODEL), F32)),
        compiler_params=pltpu.CompilerParams(vmem_limit_bytes=VMEM_LIMIT),
        name="mod",
    )(c8, w_mod, b_mod, lb_logits)


def _inproj_kernel(x_ref, mod_ref, w_ref, b_ref, o_ref, u_ref):
    j = pl.program_id(1)

    @pl.when(j == 0)
    def _():
        u_ref[...] = (x_ref[...] * (1.0 + mod_ref[1:2, :]) + mod_ref[0:1, :]).astype(BF16)

    o_ref[...] = (_dot(u_ref[...], w_ref[j]) + b_ref[j]).astype(o_ref.dtype)


def _inproj_call(x2, mod3, w3, b3, *, out_dtype, tm, tiles_per_mod):
    n_tok = x2.shape[0]
    n_col = w3.shape[0]
    return pl.pallas_call(
        _inproj_kernel,
        grid=(n_tok // tm, n_col),
        in_specs=[
            pl.BlockSpec((tm, D_MODEL), lambda i, j: (i, 0)),
            pl.BlockSpec((None, 3, D_MODEL), lambda i, j: (i // tiles_per_mod, 0, 0)),
            pl.BlockSpec((n_col, D_MODEL, D_MODEL), lambda i, j: (0, 0, 0), pipeline_mode=pl.Buffered(1)),
            pl.BlockSpec((n_col, 1, D_MODEL), lambda i, j: (0, 0, 0)),
        ],
        out_specs=pl.BlockSpec((None, None, tm, D_MODEL), lambda i, j: (i, j, 0, 0)),
        out_shape=jax.ShapeDtypeStruct((n_tok // tm, n_col, tm, D_MODEL), out_dtype),
        scratch_shapes=[pltpu.VMEM((tm, D_MODEL), BF16)],
        compiler_params=pltpu.CompilerParams(
            dimension_semantics=("arbitrary", "arbitrary"), vmem_limit_bytes=VMEM_LIMIT),
        name="inproj",
    )(x2, mod3, w3, b3)


def _lru_ab(xc, g_r, g_i, br, bi, sp):
    r = _sigmoid(g_r + br)
    i = _sigmoid(g_i + bi)
    a = jnp.exp((-RG_C) * sp * r)
    mult = jnp.sqrt(1.0 - a * a)
    return a, mult * (i * xc)


def _softplus(y):
    return jnp.maximum(y, 0.0) + jnp.log(1.0 + jnp.exp(-jnp.abs(y)))


def _ctx_kernel(zf_ref, zb_ref, v_ref, z5_ref, lb_ref, cw_ref, cb_ref, wg_ref, br_ref, bi_ref, lam_ref,
                sf_ref, sb_ref, hf_ref, hb_ref, zp_ref, a_ref, b_ref):
    n = zf_ref.shape[0]
    ri = lax.broadcasted_iota(jnp.int32, (n, n), 0)
    ci = lax.broadcasted_iota(jnp.int32, (n, n), 1)
    tril = (ci <= ri).astype(F32).astype(BF16)
    v = v_ref[...].astype(BF16)

    lf = _log_forget(zf_ref[...], lb_ref[0:1, :])
    g = _cumsum_rows(tril, lf)
    ke = ((1.0 - jnp.exp(lf)) * jnp.exp(g[n - 1:n, :] - g)).astype(BF16)
    for h in range(N_HEADS):
        sl = slice(h * HEAD_DIM, (h + 1) * HEAD_DIM)
        sf_ref[h] = _dot_tn(v[:, sl], ke[:, sl])
    lf = _log_forget(zb_ref[...], lb_ref[1:2, :])
    g = _cumsum_rows(tril, lf)
    ke = ((1.0 - jnp.exp(lf)) * jnp.exp(g - lf)).astype(BF16)
    for h in range(N_HEADS):
        sl = slice(h * HEAD_DIM, (h + 1) * HEAD_DIM)
        sb_ref[h] = _dot_tn(v[:, sl], ke[:, sl])

    zp_ref[0:8, :] = jnp.zeros((8, D_MODEL), F32)
    zp_ref[8 + n:16 + n, :] = jnp.zeros((8, D_MODEL), F32)
    zp_ref[8:8 + n, :] = z5_ref[...]
    xc = cb_ref[...] + zp_ref[7:7 + n, :] * cw_ref[0:1, :]
    for kk in range(1, N_CONV):
        xc = xc + zp_ref[7 + kk:7 + kk + n, :] * cw_ref[kk:kk + 1, :]

    for d, h_ref in ((0, hf_ref), (1, hb_ref)):
        sp = _softplus(-lam_ref[d:d + 1, :])
        for blk in range(N_BLOCKS):
            sl = slice(blk * BLOCK_DIM, (blk + 1) * BLOCK_DIM)
            g = _dot(xc[:, sl].astype(BF16), wg_ref[blk, :, 2 * d * BLOCK_DIM:2 * (d + 1) * BLOCK_DIM])
            a, b = _lru_ab(xc[:, sl], g[:, :BLOCK_DIM], g[:, BLOCK_DIM:],
                           br_ref[d:d + 1, sl], bi_ref[d:d + 1, sl], sp[:, sl])
            a_ref[:, sl] = a
            b_ref[:, sl] = b

        def step(t, h, d=d):
            tt = (n - 1 - t) if d == 1 else t
            return a_ref[pl.ds(tt, 1), :] * h + b_ref[pl.ds(tt, 1), :]

        h_ref[...] = lax.fori_loop(0, n, step, jnp.zeros((1, D_MODEL), F32))


def _ctx_call(feat_c, lb, conv_w, conv_b, wg, br, bi, lam, *, batch, n_ctx):
    feat = lambda j: pl.BlockSpec((None, None, n_ctx, D_MODEL), lambda b, j=j: (b, j, 0, 0))
    full = lambda shape: pl.BlockSpec(shape, lambda b: (0,) * len(shape))
    state = pl.BlockSpec((None, N_HEADS, HEAD_DIM, HEAD_DIM), lambda b: (b, 0, 0, 0))
    hvec = pl.BlockSpec((None, 1, D_MODEL), lambda b: (b, 0, 0))
    return pl.pallas_call(
        _ctx_kernel,
        grid=(batch,),
        in_specs=[feat(0), feat(1), feat(2), feat(3), full((2, D_MODEL)),
                  full((N_CONV, D_MODEL)), full((1, D_MODEL)),
                  full((N_BLOCKS, BLOCK_DIM, 4 * BLOCK_DIM)),
                  full((2, D_MODEL)), full((2, D_MODEL)), full((2, D_MODEL))],
        out_specs=(state, state, hvec, hvec),
        out_shape=(jax.ShapeDtypeStruct((batch, N_HEADS, HEAD_DIM, HEAD_DIM), F32),
                   jax.ShapeDtypeStruct((batch, N_HEADS, HEAD_DIM, HEAD_DIM), F32),
                   jax.ShapeDtypeStruct((batch, 1, D_MODEL), F32),
                   jax.ShapeDtypeStruct((batch, 1, D_MODEL), F32)),
        scratch_shapes=[pltpu.VMEM((n_ctx + 16, D_MODEL), F32),
                        pltpu.VMEM((n_ctx, D_MODEL), F32),
                        pltpu.VMEM((n_ctx, D_MODEL), F32)],
        compiler_params=pltpu.CompilerParams(
            dimension_semantics=("arbitrary",), vmem_limit_bytes=VMEM_LIMIT),
        name="ctx_states",
    )(feat_c, feat_c, feat_c, feat_c, lb, conv_w, conv_b, wg, br, bi, lam)


def _gla_kernel(*refs, reverse, rows, finish):
    if finish:
        zq_ref, zf_ref, v_ref, lb_ref, s0_ref, other_ref, ng_ref, o_ref, st_ref = refs
    else:
        zq_ref, zf_ref, v_ref, lb_ref, s0_ref, o_ref, st_ref = refs

    @pl.when(pl.program_id(1) == 0)
    def _():
        st_ref[...] = s0_ref[...]

    ri = lax.broadcasted_iota(jnp.int32, (GRID_W, GRID_W), 0)
    ci = lax.broadcasted_iota(jnp.int32, (GRID_W, GRID_W), 1)
    allow = (ci >= ri) if reverse else (ci <= ri)
    mask = allow.astype(F32).astype(BF16)
    edge = 0 if reverse else GRID_W - 1
    edge_d = 1 if reverse else 0

    def body(n, carry):
        r = (rows - 1 - n) if reverse else n
        off = pl.multiple_of(r * GRID_W, GRID_W)
        q = _silu(zq_ref[pl.ds(off, GRID_W), :]) * Q_SCALE
        lf = _log_forget(zf_ref[pl.ds(off, GRID_W), :], lb_ref[edge_d:edge_d + 1, :])
        v = v_ref[pl.ds(off, GRID_W), :]
        g = _cumsum_rows(mask, lf)
        g_all = g[edge:edge + 1, :]
        k = 1.0 - jnp.exp(lf)
        qd = (q * jnp.exp(g)).astype(BF16)
        ki = (k * jnp.exp(-g)).astype(BF16)
        ke = (k * jnp.exp(g_all - g)).astype(BF16)
        dec = jnp.exp(g_all)
        for h in range(N_HEADS):
            sl = slice(h * HEAD_DIM, (h + 1) * HEAD_DIM)
            st = st_ref[h]
            p = jnp.where(allow, _dot_nt(qd[:, sl], ki[:, sl]), 0.0).astype(BF16)
            o = _dot(p, v[:, sl]) + _dot_nt(qd[:, sl], st.astype(BF16))
            st_ref[h] = st * dec[:, sl] + _dot_tn(v[:, sl], ke[:, sl])
            if finish:
                o = o + other_ref[pl.ds(off, GRID_W), sl]
                ms = jnp.mean(o * o, axis=-1, keepdims=True)
                o = o * lax.rsqrt(ms + RMS_EPS) * ng_ref[...]
            o_ref[pl.ds(off, GRID_W), sl] = o.astype(o_ref.dtype)
        return carry

    lax.fori_loop(0, rows, body, 0)


def _gla_call(feat, feat_h, lb, f_col, s0, other=None, ng=None, *, reverse, batch, seq, rows):
    tb = rows * GRID_W
    nrb = seq // tb
    finish = other is not None
    per_tile = feat.shape[2] // tb

    def row_block(b, i):
        return b * nrb + ((nrb - 1 - i) if reverse else i)

    def col(c):
        return pl.BlockSpec((None, None, tb, D_MODEL), lambda b, i, c=c: (
            row_block(b, i) // per_tile, c, row_block(b, i) % per_tile, 0))

    tok = pl.BlockSpec((tb, D_MODEL), lambda b, i: (row_block(b, i), 0))
    in_specs = [col(F_Q), col(f_col), col(H_V),
                pl.BlockSpec((2, D_MODEL), lambda b, i: (0, 0)),
                pl.BlockSpec((None, N_HEADS, HEAD_DIM, HEAD_DIM), lambda b, i: (b, 0, 0, 0))]
    args = [feat, feat, feat_h, lb, s0]
    if finish:
        in_specs += [tok, pl.BlockSpec((1, HEAD_DIM), lambda b, i: (0, 0))]
        args += [other, ng]
    return pl.pallas_call(
        functools.partial(_gla_kernel, reverse=reverse, rows=rows, finish=finish),
        grid=(batch, nrb),
        in_specs=in_specs,
        out_specs=tok,
        out_shape=jax.ShapeDtypeStruct((batch * seq, D_MODEL), BF16 if finish else F32),
        scratch_shapes=[pltpu.VMEM((N_HEADS, HEAD_DIM, HEAD_DIM), F32)],
        compiler_params=pltpu.CompilerParams(
            dimension_semantics=("arbitrary", "arbitrary"), vmem_limit_bytes=VMEM_LIMIT),
        name="gla_bwd" if reverse else "gla_fwd",
    )(*args)


def _lru_kernel(z5_ref, cw_ref, cb_ref, wg_ref, br_ref, bi_ref, lam_ref, h0f_ref, h0b_ref, o_ref,
                zp_ref, af_ref, bf_ref, ab_ref, bb_ref, cf_ref, cr_ref, *, n_rows, rows_per_step):
    w = GRID_W
    seq = n_rows * w
    tb = rows_per_step * w
    n_steps = n_rows // rows_per_step

    zp_ref[0:w, :] = jnp.zeros((w, BLOCK_DIM), F32)
    zp_ref[w + seq:, :] = jnp.zeros((2 * w, BLOCK_DIM), F32)
    tile = z5_ref.shape[1]
    for k in range(z5_ref.shape[0]):
        zp_ref[w + k * tile:w + (k + 1) * tile, :] = z5_ref[k]

    sp = [_softplus(-lam_ref[d:d + 1, :]) for d in (0, 1)]

    def gates(s, c):
        off = pl.multiple_of(s * tb, tb)
        xc = cb_ref[...] + zp_ref[pl.ds(off, tb), :] * cw_ref[0:1, :]
        for kk in range(1, N_CONV):
            xc = xc + zp_ref[pl.ds(off + kk * w, tb), :] * cw_ref[kk:kk + 1, :]
        g = _dot(xc.astype(BF16), wg_ref[...])
        for d, a_ref, b_ref in ((0, af_ref, bf_ref), (1, ab_ref, bb_ref)):
            c0 = 2 * d * BLOCK_DIM
            a, b = _lru_ab(xc, g[:, c0:c0 + BLOCK_DIM], g[:, c0 + BLOCK_DIM:c0 + 2 * BLOCK_DIM],
                           br_ref[d:d + 1, :], bi_ref[d:d + 1, :], sp[d])
            a_ref[pl.ds(off, tb), :] = a
            b_ref[pl.ds(off, tb), :] = b
        return c

    lax.fori_loop(0, n_steps, gates, 0)

    def scan(n, carry):
        hf, pf, hb, pb = carry
        off_f = pl.multiple_of(n * w, w)
        off_b = pl.multiple_of((n_rows - 1 - n) * w, w)
        a = af_ref[pl.ds(off_f, w), :]
        hf = a * hf + bf_ref[pl.ds(off_f, w), :]
        pf = pf * a
        bf_ref[pl.ds(off_f, w), :] = hf
        af_ref[pl.ds(off_f, w), :] = pf
        a = ab_ref[pl.ds(off_b, w), :]
        hb = a * hb + bb_ref[pl.ds(off_b, w), :]
        pb = pb * a
        bb_ref[pl.ds(off_b, w), :] = hb
        ab_ref[pl.ds(off_b, w), :] = pb
        return hf, pf, hb, pb

    zeros = jnp.zeros((w, BLOCK_DIM), F32)
    ones = jnp.ones((w, BLOCK_DIM), F32)
    lax.fori_loop(0, n_rows, scan, (zeros, ones, zeros, ones))

    last = (n_rows - 1) * w

    def carry(n, c):
        cf, cb = c
        cf_ref[pl.ds(n, 1), :] = cf
        cf = bf_ref[pl.ds(last + n, 1), :] + af_ref[pl.ds(last + n, 1), :] * cf
        col = w - 1 - n
        cr_ref[pl.ds(col, 1), :] = cb
        cb = bb_ref[pl.ds(col, 1), :] + ab_ref[pl.ds(col, 1), :] * cb
        return cf, cb

    lax.fori_loop(0, w, carry, (h0f_ref[...], h0b_ref[...]))

    def fix(s, c):
        off = pl.multiple_of(s * tb, tb)
        cf = jnp.concatenate([cf_ref[...]] * rows_per_step, axis=0)
        cr = jnp.concatenate([cr_ref[...]] * rows_per_step, axis=0)
        h = (bf_ref[pl.ds(off, tb), :] + af_ref[pl.ds(off, tb), :] * cf
             + bb_ref[pl.ds(off, tb), :] + ab_ref[pl.ds(off, tb), :] * cr)
        o_ref[pl.ds(off, tb), :] = h.astype(o_ref.dtype)
        return c

    lax.fori_loop(0, n_steps, fix, 0)


def _lru_call(feat, conv_w, conv_b, wg, br, bi, lam, h0f, h0b, *, batch, seq):
    n_rows = seq // GRID_W
    vec = lambda n: pl.BlockSpec((n, BLOCK_DIM), lambda b, c: (0, c))
    h0 = pl.BlockSpec((None, 1, BLOCK_DIM), lambda b, c: (b, 0, c))
    buf = pltpu.VMEM((seq, BLOCK_DIM), F32)
    n_i, n_col, tm, _ = feat.shape
    feat5 = feat.reshape(batch, n_i // batch, n_col, tm, D_MODEL)
    return pl.pallas_call(
        functools.partial(_lru_kernel, n_rows=n_rows, rows_per_step=8),
        grid=(batch, N_BLOCKS),
        in_specs=[pl.BlockSpec((None, n_i // batch, None, tm, BLOCK_DIM), lambda b, c: (b, 0, F_Z5, 0, c)),
                  vec(N_CONV), vec(1),
                  pl.BlockSpec((None, BLOCK_DIM, 4 * BLOCK_DIM), lambda b, c: (c, 0, 0)),
                  vec(2), vec(2), vec(2), h0, h0],
        out_specs=pl.BlockSpec((seq, BLOCK_DIM), lambda b, c: (b, c)),
        out_shape=jax.ShapeDtypeStruct((batch * seq, D_MODEL), BF16),
        scratch_shapes=[pltpu.VMEM((seq + 3 * GRID_W, BLOCK_DIM), F32), buf, buf, buf, buf,
                        pltpu.VMEM((GRID_W, BLOCK_DIM), F32), pltpu.VMEM((GRID_W, BLOCK_DIM), F32)],
        compiler_params=pltpu.CompilerParams(
            dimension_semantics=("arbitrary", "arbitrary"), vmem_limit_bytes=VMEM_LIMIT),
        name="lru",
    )(feat5, conv_w, conv_b, wg, br, bi, lam, h0f, h0b)


def _merge_kernel(oa_ref, hx_ref, z4_ref, z6_ref, z7_ref, z8_ref, x_ref, mod_ref,
                  pa_ref, pb_ref, wo_ref, lg_ref, lbias_ref, o_ref):
    f32 = lambda ref: ref[...].astype(F32)
    o_a = (f32(oa_ref) * _silu(f32(z4_ref))).astype(BF16)
    o_b = (f32(hx_ref) * _silu(f32(z6_ref))).astype(BF16)
    y = (_sigmoid(f32(z7_ref)) * _dot(o_a, pa_ref[...])
         + _sigmoid(f32(z8_ref)) * _dot(o_b, pb_ref[...]))
    y = _dot(y.astype(BF16), wo_ref[...])
    t = DEEPNORM_ALPHA * x_ref[...] + mod_ref[2:3, :] * y
    mu = jnp.mean(t, axis=-1, keepdims=True)
    tc = t - mu
    var = jnp.mean(tc * tc, axis=-1, keepdims=True)
    o_ref[...] = tc * lax.rsqrt(var + LN_EPS) * lg_ref[...] + lbias_ref[...]


def _merge_call(oa, hx, feat_h, x2, mod3, pa, pb, wo, lg, lbias, *, tm, tiles_per_batch):
    n_tok = x2.shape[0]
    tok = pl.BlockSpec((tm, D_MODEL), lambda i: (i, 0))
    per_tile = feat_h.shape[2] // tm
    col = lambda j: pl.BlockSpec((None, None, tm, D_MODEL), lambda i, j=j: (i // per_tile, j, i % per_tile, 0))
    full = lambda shape: pl.BlockSpec(shape, lambda i: (0,) * len(shape))
    return pl.pallas_call(
        _merge_kernel,
        grid=(n_tok // tm,),
        in_specs=[tok, tok, col(H_G4), col(H_G6), col(H_M7), col(H_M8), tok,
                  pl.BlockSpec((None, 3, D_MODEL), lambda i: (i // tiles_per_batch, 0, 0)),
                  full((D_MODEL, D_MODEL)), full((D_MODEL, D_MODEL)),
                  full((D_MODEL, D_MODEL)), full((1, D_MODEL)), full((1, D_MODEL))],
        out_specs=tok,
        out_shape=jax.ShapeDtypeStruct((n_tok, D_MODEL), F32),
        compiler_params=pltpu.CompilerParams(
            dimension_semantics=("arbitrary",), vmem_limit_bytes=VMEM_LIMIT),
        name="merge",
    )(oa, hx, feat_h, feat_h, feat_h, feat_h, x2, mod3, pa, pb, wo, lg, lbias)


def kernel(x, c, ctx, c_ctx, w_mod, b_mod, w_in, b_in, lb_logits, norm_a_g, conv_w, conv_b,
           w_r, b_r, w_i, b_i, lam, p_a, p_b, w_out, ln_g, ln_b):
    batch, seq, d = x.shape
    n_ctx = ctx.shape[1]
    assert d == D_MODEL and seq % GRID_W == 0 and w_in.shape[0] == 1

    def select_cols(cols):
        w3 = jnp.stack([w_in[0][:, j * D_MODEL:(j + 1) * D_MODEL] for j in cols]).astype(BF16)
        b3 = jnp.stack([b_in[0][j * D_MODEL:(j + 1) * D_MODEL] for j in cols])[:, None, :]
        return w3, b3

    wg = jnp.concatenate([w_r[0, 0], w_i[0, 0], w_r[0, 1], w_i[0, 1]], axis=-1).astype(BF16)
    c8 = jnp.zeros((8, D_MODEL), F32).at[:batch].set(c).at[batch].set(c_ctx)

    mod, lb = _mod_call(c8, w_mod[0], b_mod[0][None, :], lb_logits)
    mod3 = mod.reshape(8, 3, D_MODEL)

    x2 = x.reshape(batch * seq, D_MODEL)
    tm = 1024
    feat = _inproj_call(x2, mod3, *select_cols(LATENT_F32), out_dtype=F32, tm=tm, tiles_per_mod=seq // tm)
    feat_h = _inproj_call(x2, mod3, *select_cols(LATENT_BF16), out_dtype=BF16, tm=tm, tiles_per_mod=seq // tm)
    feat_c = _inproj_call(ctx.reshape(batch * n_ctx, D_MODEL), mod3[batch:batch + 1], *select_cols(CTX_F32),
                          out_dtype=F32, tm=n_ctx, tiles_per_mod=batch)

    cw, cb = conv_w[0], conv_b[0][None, :]
    s0f, s0b, h0f, h0b = _ctx_call(feat_c, lb, cw, cb, wg, b_r[0], b_i[0], lam[0], batch=batch, n_ctx=n_ctx)

    o_b = _gla_call(feat, feat_h, lb, F_F1, s0b, reverse=True, batch=batch, seq=seq, rows=8)
    oa = _gla_call(feat, feat_h, lb, F_F0, s0f, o_b, norm_a_g[0][None, :],
                   reverse=False, batch=batch, seq=seq, rows=8)
    hx = _lru_call(feat, cw, cb, wg, b_r[0], b_i[0], lam[0], h0f, h0b, batch=batch, seq=seq)

    out = _merge_call(oa, hx, feat_h, x2, mod3,
                      p_a[0].astype(BF16), p_b[0].astype(BF16), w_out[0].astype(BF16),
                      ln_g[0][None, :], ln_b[0][None, :], tm=512, tiles_per_batch=seq // 512)
    return out.reshape(batch, seq, D_MODEL)
```

```python
import functools

import jax
import jax.numpy as jnp
from jax import lax
from jax.experimental import pallas as pl
from jax.experimental.pallas import tpu as pltpu

F32 = jnp.float32
BF16 = jnp.bfloat16

D_MODEL = 1024
GRID_W = 64
HEAD_DIM = 128
N_HEADS = D_MODEL // HEAD_DIM
N_BLOCKS = 8
BLOCK_DIM = D_MODEL // N_BLOCKS
N_CONV = 4
RG_C = 8.0
LN_EPS = 1e-5
RMS_EPS = 1e-6
DEEPNORM_ALPHA = 2.0 ** 0.25
Q_SCALE = HEAD_DIM ** -0.5

VMEM_LIMIT = 56 * 1024 * 1024

LATENT_F32 = (0, 1, 2, 5)
LATENT_BF16 = (3, 4, 6, 7, 8)
CTX_F32 = (1, 2, 3, 5)
F_Q, F_F0, F_F1, F_Z5 = 0, 1, 2, 3
H_V, H_G4, H_G6, H_M7, H_M8 = 0, 1, 2, 3, 4


def _sigmoid(x):
    return 0.5 * jnp.tanh(0.5 * x) + 0.5


def _silu(x):
    h = 0.5 * x
    return h * jnp.tanh(h) + h


def _log_forget(z, lb):
    return jnp.log((0.5 + 0.5 * lb) + (0.5 - 0.5 * lb) * jnp.tanh(0.5 * z))


def _dot(a, b):
    return jnp.dot(a, b, preferred_element_type=F32)


def _dot_nt(a, b):
    return lax.dot_general(a, b, (((1,), (1,)), ((), ())), preferred_element_type=F32)


def _dot_tn(a, b):
    return lax.dot_general(a, b, (((0,), (0,)), ((), ())), preferred_element_type=F32)


def _chunk_cumsum(x, reverse):
    n, c = x.shape
    nb = n // 8
    y = x.reshape(nb, 8, c)
    sub = lax.broadcasted_iota(jnp.int32, (nb, 8, c), 1)
    for s in (1, 2, 4):
        if reverse:
            y = y + jnp.where(sub < 8 - s, pltpu.roll(y, 8 - s, axis=1), 0.0)
        else:
            y = y + jnp.where(sub >= s, pltpu.roll(y, s, axis=1), 0.0)
    offs = [None] * nb
    acc = jnp.zeros((1, c), F32)
    for b in (reversed(range(nb)) if reverse else range(nb)):
        offs[b] = acc
        acc = acc + (y[b, 0:1, :] if reverse else y[b, 7:8, :])
    return (y + jnp.stack(offs)).reshape(n, c), acc


def _cumsum_rows(mask_bf16, x):
    hi = x.astype(BF16)
    r1 = x - hi.astype(F32)
    mid = r1.astype(BF16)
    lo = (r1 - mid.astype(F32)).astype(BF16)
    return _dot(mask_bf16, hi) + _dot(mask_bf16, mid) + _dot(mask_bf16, lo)


def _mod_kernel(c_ref, w_ref, b_ref, lbl_ref, mod_ref, lb_ref):
    mod_ref[...] = _dot(_silu(c_ref[...]), w_ref[...]) + b_ref[...]
    l = lbl_ref[...]
    e = jnp.exp(l - jnp.max(l, axis=0, keepdims=True))
    lb_ref[...] = e[0] / jnp.sum(e, axis=0)


def _mod_call(c8, w_mod, b_mod, lb_logits):
    return pl.pallas_call(
        _mod_kernel,
        out_shape=(jax.ShapeDtypeStruct((8, 3 * D_MODEL), F32),
                   jax.ShapeDtypeStruct((2, D_MODEL), F32)),
        compiler_params=pltpu.CompilerParams(vmem_limit_bytes=VMEM_LIMIT),
        name="mod",
    )(c8, w_mod, b_mod, lb_logits)


def _inproj_kernel(x_ref, mod_ref, w_ref, b_ref, o_ref, u_ref):
    j = pl.program_id(1)

    @pl.when(j == 0)
    def _():
        u_ref[...] = (x_ref[...] * (1.0 + mod_ref[1:2, :]) + mod_ref[0:1, :]).astype(BF16)

    o_ref[...] = (_dot(u_ref[...], w_ref[j]) + b_ref[j]).astype(o_ref.dtype)


def _inproj_call(x2, mod3, w3, b3, *, out_dtype, tm, tiles_per_mod):
    n_tok = x2.shape[0]
    n_col = w3.shape[0]
    return pl.pallas_call(
        _inproj_kernel,
        grid=(n_tok // tm, n_col),
        in_specs=[
            pl.BlockSpec((tm, D_MODEL), lambda i, j: (i, 0)),
            pl.BlockSpec((None, 3, D_MODEL), lambda i, j: (i // tiles_per_mod, 0, 0)),
            pl.BlockSpec((n_col, D_MODEL, D_MODEL), lambda i, j: (0, 0, 0), pipeline_mode=pl.Buffered(1)),
            pl.BlockSpec((n_col, 1, D_MODEL), lambda i, j: (0, 0, 0)),
        ],
        out_specs=pl.BlockSpec((None, None, tm, D_MODEL), lambda i, j: (i, j, 0, 0)),
        out_shape=jax.ShapeDtypeStruct((n_tok // tm, n_col, tm, D_MODEL), out_dtype),
        scratch_shapes=[pltpu.VMEM((tm, D_MODEL), BF16)],
        compiler_params=pltpu.CompilerParams(
            dimension_semantics=("arbitrary", "arbitrary"), vmem_limit_bytes=VMEM_LIMIT),
        name="inproj",
    )(x2, mod3, w3, b3)


LOG2_E = 1.4426950408889634


def _lru_ab(xc, h_r, h_i, br, bi, sp):
    k = (-0.5 * RG_C * LOG2_E) * sp
    a = jnp.exp2(k * jnp.tanh(h_r + 0.5 * br) + k)
    i = 0.5 * jnp.tanh(h_i + 0.5 * bi) + 0.5
    mult = jnp.sqrt(1.0 - a * a)
    return a, mult * (i * xc)


def _softplus(y):
    return jnp.maximum(y, 0.0) + jnp.log(1.0 + jnp.exp(-jnp.abs(y)))


def _ctx_kernel(zf_ref, zb_ref, v_ref, z5_ref, lb_ref, cw_ref, cb_ref, wg_ref, br_ref, bi_ref, lam_ref,
                sf_ref, sb_ref, hf_ref, hb_ref, zp_ref, a_ref, b_ref):
    n = zf_ref.shape[0]
    ri = lax.broadcasted_iota(jnp.int32, (n, n), 0)
    ci = lax.broadcasted_iota(jnp.int32, (n, n), 1)
    tril = (ci <= ri).astype(F32).astype(BF16)
    v = v_ref[...].astype(BF16)

    lf = _log_forget(zf_ref[...], lb_ref[0:1, :])
    g = _cumsum_rows(tril, lf)
    ke = ((1.0 - jnp.exp(lf)) * jnp.exp(g[n - 1:n, :] - g)).astype(BF16)
    for h in range(N_HEADS):
        sl = slice(h * HEAD_DIM, (h + 1) * HEAD_DIM)
        sf_ref[h] = _dot_tn(v[:, sl], ke[:, sl])
    lf = _log_forget(zb_ref[...], lb_ref[1:2, :])
    g = _cumsum_rows(tril, lf)
    ke = ((1.0 - jnp.exp(lf)) * jnp.exp(g - lf)).astype(BF16)
    for h in range(N_HEADS):
        sl = slice(h * HEAD_DIM, (h + 1) * HEAD_DIM)
        sb_ref[h] = _dot_tn(v[:, sl], ke[:, sl])

    zp_ref[0:8, :] = jnp.zeros((8, D_MODEL), F32)
    zp_ref[8 + n:16 + n, :] = jnp.zeros((8, D_MODEL), F32)
    zp_ref[8:8 + n, :] = z5_ref[...]
    xc = cb_ref[...] + zp_ref[7:7 + n, :] * cw_ref[0:1, :]
    for kk in range(1, N_CONV):
        xc = xc + zp_ref[7 + kk:7 + kk + n, :] * cw_ref[kk:kk + 1, :]

    for d, h_ref in ((0, hf_ref), (1, hb_ref)):
        sp = _softplus(-lam_ref[d:d + 1, :])
        for blk in range(N_BLOCKS):
            sl = slice(blk * BLOCK_DIM, (blk + 1) * BLOCK_DIM)
            g = _dot(xc[:, sl].astype(BF16), wg_ref[blk, :, 2 * d * BLOCK_DIM:2 * (d + 1) * BLOCK_DIM])
            a, b = _lru_ab(xc[:, sl], g[:, :BLOCK_DIM], g[:, BLOCK_DIM:],
                           br_ref[d:d + 1, sl], bi_ref[d:d + 1, sl], sp[:, sl])
            a_ref[:, sl] = a
            b_ref[:, sl] = b

        def step(t, h, d=d):
            tt = (n - 1 - t) if d == 1 else t
            return a_ref[pl.ds(tt, 1), :] * h + b_ref[pl.ds(tt, 1), :]

        h_ref[...] = lax.fori_loop(0, n, step, jnp.zeros((1, D_MODEL), F32))


def _ctx_call(feat_c, lb, conv_w, conv_b, wg, br, bi, lam, *, batch, n_ctx):
    feat = lambda j: pl.BlockSpec((None, None, n_ctx, D_MODEL), lambda b, j=j: (b, j, 0, 0))
    full = lambda shape: pl.BlockSpec(shape, lambda b: (0,) * len(shape))
    state = pl.BlockSpec((None, N_HEADS, HEAD_DIM, HEAD_DIM), lambda b: (b, 0, 0, 0))
    hvec = pl.BlockSpec((None, 1, D_MODEL), lambda b: (b, 0, 0))
    return pl.pallas_call(
        _ctx_kernel,
        grid=(batch,),
        in_specs=[feat(0), feat(1), feat(2), feat(3), full((2, D_MODEL)),
                  full((N_CONV, D_MODEL)), full((1, D_MODEL)),
                  full((N_BLOCKS, BLOCK_DIM, 4 * BLOCK_DIM)),
                  full((2, D_MODEL)), full((2, D_MODEL)), full((2, D_MODEL))],
        out_specs=(state, state, hvec, hvec),
        out_shape=(jax.ShapeDtypeStruct((batch, N_HEADS, HEAD_DIM, HEAD_DIM), F32),
                   jax.ShapeDtypeStruct((batch, N_HEADS, HEAD_DIM, HEAD_DIM), F32),
                   jax.ShapeDtypeStruct((batch, 1, D_MODEL), F32),
                   jax.ShapeDtypeStruct((batch, 1, D_MODEL), F32)),
        scratch_shapes=[pltpu.VMEM((n_ctx + 16, D_MODEL), F32),
                        pltpu.VMEM((n_ctx, D_MODEL), F32),
                        pltpu.VMEM((n_ctx, D_MODEL), F32)],
        compiler_params=pltpu.CompilerParams(
            dimension_semantics=("arbitrary",), vmem_limit_bytes=VMEM_LIMIT),
        name="ctx_states",
    )(feat_c, feat_c, feat_c, feat_c, lb, conv_w, conv_b, wg, br, bi, lam)


def _gla_kernel(*refs, reverse, rows, finish):
    if finish:
        zq_ref, zf_ref, v_ref, lb_ref, s0_ref, other_ref, ng_ref, o_ref, st_ref = refs
    else:
        zq_ref, zf_ref, v_ref, lb_ref, s0_ref, o_ref, st_ref = refs

    @pl.when(pl.program_id(1) == 0)
    def _():
        st_ref[...] = s0_ref[...]

    ri = lax.broadcasted_iota(jnp.int32, (GRID_W, GRID_W), 0)
    ci = lax.broadcasted_iota(jnp.int32, (GRID_W, GRID_W), 1)
    allow = (ci >= ri) if reverse else (ci <= ri)
    mask = allow.astype(F32).astype(BF16)
    edge = 0 if reverse else GRID_W - 1
    edge_d = 1 if reverse else 0

    def body(n, carry):
        r = (rows - 1 - n) if reverse else n
        off = pl.multiple_of(r * GRID_W, GRID_W)
        q = _silu(zq_ref[pl.ds(off, GRID_W), :]) * Q_SCALE
        lf = _log_forget(zf_ref[pl.ds(off, GRID_W), :], lb_ref[edge_d:edge_d + 1, :])
        v = v_ref[pl.ds(off, GRID_W), :]
        g, g_all = _chunk_cumsum(lf, reverse)
        k = 1.0 - jnp.exp(lf)
        qd = (q * jnp.exp(g)).astype(BF16)
        ki = (k * jnp.exp(-g)).astype(BF16)
        ke = (k * jnp.exp(g_all - g)).astype(BF16)
        dec = jnp.exp(g_all)
        for h in range(N_HEADS):
            sl = slice(h * HEAD_DIM, (h + 1) * HEAD_DIM)
            st = st_ref[h]
            p = jnp.where(allow, _dot(qd[:, sl], ki[:, sl].T), 0.0).astype(BF16)
            o = _dot(p, v[:, sl]) + _dot(qd[:, sl], st.astype(BF16).T)
            st_ref[h] = st * dec[:, sl] + _dot_tn(v[:, sl], ke[:, sl])
            if finish:
                o = o + other_ref[pl.ds(off, GRID_W), sl]
                ms = jnp.mean(o * o, axis=-1, keepdims=True)
                o = o * lax.rsqrt(ms + RMS_EPS) * ng_ref[...]
            o_ref[pl.ds(off, GRID_W), sl] = o.astype(o_ref.dtype)
        return carry

    lax.fori_loop(0, rows, body, 0)


def _gla_call(feat, feat_h, lb, f_col, s0, other=None, ng=None, *, reverse, batch, seq, rows):
    tb = rows * GRID_W
    nrb = seq // tb
    finish = other is not None
    per_tile = feat.shape[2] // tb

    def row_block(b, i):
        return b * nrb + ((nrb - 1 - i) if reverse else i)

    def col(c):
        return pl.BlockSpec((None, None, tb, D_MODEL), lambda b, i, c=c: (
            row_block(b, i) // per_tile, c, row_block(b, i) % per_tile, 0))

    tok = pl.BlockSpec((tb, D_MODEL), lambda b, i: (row_block(b, i), 0))
    in_specs = [col(F_Q), col(f_col), col(H_V),
                pl.BlockSpec((2, D_MODEL), lambda b, i: (0, 0)),
                pl.BlockSpec((None, N_HEADS, HEAD_DIM, HEAD_DIM), lambda b, i: (b, 0, 0, 0))]
    args = [feat, feat, feat_h, lb, s0]
    if finish:
        in_specs += [tok, pl.BlockSpec((1, HEAD_DIM), lambda b, i: (0, 0))]
        args += [other, ng]
    return pl.pallas_call(
        functools.partial(_gla_kernel, reverse=reverse, rows=rows, finish=finish),
        grid=(batch, nrb),
        in_specs=in_specs,
        out_specs=tok,
        out_shape=jax.ShapeDtypeStruct((batch * seq, D_MODEL), BF16 if finish else F32),
        scratch_shapes=[pltpu.VMEM((N_HEADS, HEAD_DIM, HEAD_DIM), F32)],
        compiler_params=pltpu.CompilerParams(
            dimension_semantics=("arbitrary", "arbitrary"), vmem_limit_bytes=VMEM_LIMIT),
        name="gla_bwd" if reverse else "gla_fwd",
    )(*args)


def _lru_kernel(z5_ref, cw_ref, cb_ref, wg_ref, br_ref, bi_ref, lam_ref, h0f_ref, h0b_ref, o_ref,
                zp_ref, af_ref, bf_ref, ab_ref, bb_ref, cf_ref, cr_ref, *, n_rows, rows_per_step):
    w = GRID_W
    seq = n_rows * w
    tb = rows_per_step * w
    n_steps = n_rows // rows_per_step

    zp_ref[0:w, :] = jnp.zeros((w, BLOCK_DIM), F32)
    zp_ref[w + seq:, :] = jnp.zeros((2 * w, BLOCK_DIM), F32)
    tile = z5_ref.shape[1]
    for k in range(z5_ref.shape[0]):
        zp_ref[w + k * tile:w + (k + 1) * tile, :] = z5_ref[k]

    sp = [_softplus(-lam_ref[d:d + 1, :]) for d in (0, 1)]

    def gates(s, c):
        off = pl.multiple_of(s * tb, tb)
        xc = cb_ref[...] + zp_ref[pl.ds(off, tb), :] * cw_ref[0:1, :]
        for kk in range(1, N_CONV):
            xc = xc + zp_ref[pl.ds(off + kk * w, tb), :] * cw_ref[kk:kk + 1, :]
        g = _dot(xc.astype(BF16), wg_ref[...])
        for d, a_ref, b_ref in ((0, af_ref, bf_ref), (1, ab_ref, bb_ref)):
            c0 = 2 * d * BLOCK_DIM
            a, b = _lru_ab(xc, g[:, c0:c0 + BLOCK_DIM], g[:, c0 + BLOCK_DIM:c0 + 2 * BLOCK_DIM],
                           br_ref[d:d + 1, :], bi_ref[d:d + 1, :], sp[d])
            a_ref[pl.ds(off, tb), :] = a
            b_ref[pl.ds(off, tb), :] = b
        return c

    lax.fori_loop(0, n_steps, gates, 0)

    def scan(n, carry):
        hf, pf, hb, pb = carry
        off_f = pl.multiple_of(n * w, w)
        off_b = pl.multiple_of((n_rows - 1 - n) * w, w)
        a = af_ref[pl.ds(off_f, w), :]
        hf = a * hf + bf_ref[pl.ds(off_f, w), :]
        pf = pf * a
        bf_ref[pl.ds(off_f, w), :] = hf
        af_ref[pl.ds(off_f, w), :] = pf
        a = ab_ref[pl.ds(off_b, w), :]
        hb = a * hb + bb_ref[pl.ds(off_b, w), :]
        pb = pb * a
        bb_ref[pl.ds(off_b, w), :] = hb
        ab_ref[pl.ds(off_b, w), :] = pb
        return hf, pf, hb, pb

    zeros = jnp.zeros((w, BLOCK_DIM), F32)
    ones = jnp.ones((w, BLOCK_DIM), F32)
    lax.fori_loop(0, n_rows, scan, (zeros, ones, zeros, ones))

    last = (n_rows - 1) * w

    def carry(n, c):
        cf, cb = c
        cf_ref[pl.ds(n, 1), :] = cf
        cf = bf_ref[pl.ds(last + n, 1), :] + af_ref[pl.ds(last + n, 1), :] * cf
        col = w - 1 - n
        cr_ref[pl.ds(col, 1), :] = cb
        cb = bb_ref[pl.ds(col, 1), :] + ab_ref[pl.ds(col, 1), :] * cb
        return cf, cb

    lax.fori_loop(0, w, carry, (h0f_ref[...], h0b_ref[...]))

    def fix(s, c):
        off = pl.multiple_of(s * tb, tb)
        cf = jnp.concatenate([cf_ref[...]] * rows_per_step, axis=0)
        cr = jnp.concatenate([cr_ref[...]] * rows_per_step, axis=0)
        h = (bf_ref[pl.ds(off, tb), :] + af_ref[pl.ds(off, tb), :] * cf
             + bb_ref[pl.ds(off, tb), :] + ab_ref[pl.ds(off, tb), :] * cr)
        o_ref[pl.ds(off, tb), :] = h.astype(o_ref.dtype)
        return c

    lax.fori_loop(0, n_steps, fix, 0)


def _lru_call(feat, conv_w, conv_b, wg, br, bi, lam, h0f, h0b, *, batch, seq):
    n_rows = seq // GRID_W
    vec = lambda n: pl.BlockSpec((n, BLOCK_DIM), lambda b, c: (0, c))
    h0 = pl.BlockSpec((None, 1, BLOCK_DIM), lambda b, c: (b, 0, c))
    buf = pltpu.VMEM((seq, BLOCK_DIM), F32)
    n_i, n_col, tm, _ = feat.shape
    feat5 = feat.reshape(batch, n_i // batch, n_col, tm, D_MODEL)
    return pl.pallas_call(
        functools.partial(_lru_kernel, n_rows=n_rows, rows_per_step=8),
        grid=(batch, N_BLOCKS),
        in_specs=[pl.BlockSpec((None, n_i // batch, None, tm, BLOCK_DIM), lambda b, c: (b, 0, F_Z5, 0, c)),
                  vec(N_CONV), vec(1),
                  pl.BlockSpec((None, BLOCK_DIM, 4 * BLOCK_DIM), lambda b, c: (c, 0, 0)),
                  vec(2), vec(2), vec(2), h0, h0],
        out_specs=pl.BlockSpec((seq, BLOCK_DIM), lambda b, c: (b, c)),
        out_shape=jax.ShapeDtypeStruct((batch * seq, D_MODEL), BF16),
        scratch_shapes=[pltpu.VMEM((seq + 3 * GRID_W, BLOCK_DIM), F32), buf, buf, buf, buf,
                        pltpu.VMEM((GRID_W, BLOCK_DIM), F32), pltpu.VMEM((GRID_W, BLOCK_DIM), F32)],
        compiler_params=pltpu.CompilerParams(
            dimension_semantics=("arbitrary", "arbitrary"), vmem_limit_bytes=VMEM_LIMIT),
        name="lru",
    )(feat5, conv_w, conv_b, wg, br, bi, lam, h0f, h0b)


def _merge_kernel(oa_ref, hx_ref, z4_ref, z6_ref, z7_ref, z8_ref, x_ref, mod_ref,
                  pa_ref, pb_ref, wo_ref, lg_ref, lbias_ref, o_ref):
    f32 = lambda ref: ref[...].astype(F32)
    o_a = (f32(oa_ref) * _silu(f32(z4_ref))).astype(BF16)
    o_b = (f32(hx_ref) * _silu(f32(z6_ref))).astype(BF16)
    y = (_sigmoid(f32(z7_ref)) * _dot(o_a, pa_ref[...])
         + _sigmoid(f32(z8_ref)) * _dot(o_b, pb_ref[...]))
    y = _dot(y.astype(BF16), wo_ref[...])
    t = DEEPNORM_ALPHA * x_ref[...] + mod_ref[2:3, :] * y
    mu = jnp.mean(t, axis=-1, keepdims=True)
    tc = t - mu
    var = jnp.mean(tc * tc, axis=-1, keepdims=True)
    o_ref[...] = tc * lax.rsqrt(var + LN_EPS) * lg_ref[...] + lbias_ref[...]


def _merge_call(oa, hx, feat_h, x2, mod3, pa, pb, wo, lg, lbias, *, tm, tiles_per_batch):
    n_tok = x2.shape[0]
    tok = pl.BlockSpec((tm, D_MODEL), lambda i: (i, 0))
    per_tile = feat_h.shape[2] // tm
    col = lambda j: pl.BlockSpec((None, None, tm, D_MODEL), lambda i, j=j: (i // per_tile, j, i % per_tile, 0))
    full = lambda shape: pl.BlockSpec(shape, lambda i: (0,) * len(shape))
    return pl.pallas_call(
        _merge_kernel,
        grid=(n_tok // tm,),
        in_specs=[tok, tok, col(H_G4), col(H_G6), col(H_M7), col(H_M8), tok,
                  pl.BlockSpec((None, 3, D_MODEL), lambda i: (i // tiles_per_batch, 0, 0)),
                  full((D_MODEL, D_MODEL)), full((D_MODEL, D_MODEL)),
                  full((D_MODEL, D_MODEL)), full((1, D_MODEL)), full((1, D_MODEL))],
        out_specs=tok,
        out_shape=jax.ShapeDtypeStruct((n_tok, D_MODEL), F32),
        compiler_params=pltpu.CompilerParams(
            dimension_semantics=("arbitrary",), vmem_limit_bytes=VMEM_LIMIT),
        name="merge",
    )(oa, hx, feat_h, feat_h, feat_h, feat_h, x2, mod3, pa, pb, wo, lg, lbias)


def kernel(x, c, ctx, c_ctx, w_mod, b_mod, w_in, b_in, lb_logits, norm_a_g, conv_w, conv_b,
           w_r, b_r, w_i, b_i, lam, p_a, p_b, w_out, ln_g, ln_b):
    batch, seq, d = x.shape
    n_ctx = ctx.shape[1]
    assert d == D_MODEL and seq % GRID_W == 0 and w_in.shape[0] == 1

    def select_cols(cols):
        w3 = jnp.stack([w_in[0][:, j * D_MODEL:(j + 1) * D_MODEL] for j in cols]).astype(BF16)
        b3 = jnp.stack([b_in[0][j * D_MODEL:(j + 1) * D_MODEL] for j in cols])[:, None, :]
        return w3, b3

    wg = (0.5 * jnp.concatenate([w_r[0, 0], w_i[0, 0], w_r[0, 1], w_i[0, 1]], axis=-1)).astype(BF16)
    c8 = jnp.zeros((8, D_MODEL), F32).at[:batch].set(c).at[batch].set(c_ctx)

    mod, lb = _mod_call(c8, w_mod[0], b_mod[0][None, :], lb_logits)
    mod3 = mod.reshape(8, 3, D_MODEL)

    x2 = x.reshape(batch * seq, D_MODEL)
    tm = 2048
    feat =_inproj_call(x2, mod3, *select_cols(LATENT_F32), out_dtype=F32, tm=tm, tiles_per_mod=seq // tm)
    feat_h = _inproj_call(x2, mod3, *select_cols(LATENT_BF16), out_dtype=BF16, tm=tm, tiles_per_mod=seq // tm)
    feat_c = _inproj_call(ctx.reshape(batch * n_ctx, D_MODEL), mod3[batch:batch + 1], *select_cols(CTX_F32),
                          out_dtype=F32, tm=n_ctx, tiles_per_mod=batch)

    cw, cb = conv_w[0], conv_b[0][None, :]
    s0f, s0b, h0f, h0b = _ctx_call(feat_c, lb, cw, cb, wg, b_r[0], b_i[0], lam[0], batch=batch, n_ctx=n_ctx)

    o_b = _gla_call(feat, feat_h, lb, F_F1, s0b, reverse=True, batch=batch, seq=seq, rows=8)
    oa = _gla_call(feat, feat_h, lb, F_F0, s0f, o_b, norm_a_g[0][None, :],
                   reverse=False, batch=batch, seq=seq, rows=8)
    hx = _lru_call(feat, cw, cb, wg, b_r[0], b_i[0], lam[0], h0f, h0b, batch=batch, seq=seq)

    out = _merge_call(oa, hx, feat_h, x2, mod3,
                      p_a[0].astype(BF16), p_b[0].astype(BF16), w_out[0].astype(BF16),
                      ln_g[0][None, :], ln_b[0][None, :], tm=512, tiles_per_batch=seq // 512)
    return out.reshape(batch, seq, D_MODEL)
```

```python
import functools

import jax
import jax.numpy as jnp
from jax import lax
from jax.experimental import pallas as pl
from jax.experimental.pallas import tpu as pltpu

F32 = jnp.float32
BF16 = jnp.bfloat16

D_MODEL = 1024
GRID_W = 64
HEAD_DIM = 128
N_HEADS = D_MODEL // HEAD_DIM
N_BLOCKS = 8
BLOCK_DIM = D_MODEL // N_BLOCKS
N_CONV = 4
RG_C = 8.0
LN_EPS = 1e-5
RMS_EPS = 1e-6
DEEPNORM_ALPHA = 2.0 ** 0.25
Q_SCALE = HEAD_DIM ** -0.5

VMEM_LIMIT = 56 * 1024 * 1024

LATENT_F32 = (0, 1, 2, 5)
LATENT_BF16 = (3, 4, 6, 7, 8)
CTX_F32 = (1, 2, 3, 5)
F_Q, F_F0, F_F1, F_Z5 = 0, 1, 2, 3
H_V, H_G4, H_G6, H_M7, H_M8 = 0, 1, 2, 3, 4


def _sigmoid(x):
    return 0.5 * jnp.tanh(0.5 * x) + 0.5


def _silu(x):
    h = 0.5 * x
    return h * jnp.tanh(h) + h


def _log_forget(z, lb):
    return jnp.log((0.5 + 0.5 * lb) + (0.5 - 0.5 * lb) * jnp.tanh(0.5 * z))


def _dot(a, b):
    return jnp.dot(a, b, preferred_element_type=F32)


def _dot_nt(a, b):
    return lax.dot_general(a, b, (((1,), (1,)), ((), ())), preferred_element_type=F32)


def _dot_tn(a, b):
    return lax.dot_general(a, b, (((0,), (0,)), ((), ())), preferred_element_type=F32)


def _chunk_cumprod(x, reverse):
    n, c = x.shape
    nb = n // 8
    y = x.reshape(nb, 8, c)
    sub = lax.broadcasted_iota(jnp.int32, (nb, 8, c), 1)
    for s in (1, 2, 4):
        if reverse:
            y = y * jnp.where(sub < 8 - s, pltpu.roll(y, 8 - s, axis=1), 1.0)
        else:
            y = y * jnp.where(sub >= s, pltpu.roll(y, s, axis=1), 1.0)
    offs = [None] * nb
    acc = jnp.ones((1, c), F32)
    for b in (reversed(range(nb)) if reverse else range(nb)):
        offs[b] = acc
        acc = acc * (y[b, 0:1, :] if reverse else y[b, 7:8, :])
    return (y * jnp.stack(offs)).reshape(n, c), acc


def _cumsum_rows(mask_bf16, x):
    hi = x.astype(BF16)
    r1 = x - hi.astype(F32)
    mid = r1.astype(BF16)
    lo = (r1 - mid.astype(F32)).astype(BF16)
    return _dot(mask_bf16, hi) + _dot(mask_bf16, mid) + _dot(mask_bf16, lo)


def _mod_kernel(c_ref, w_ref, b_ref, lbl_ref, mod_ref, lb_ref):
    mod_ref[...] = _dot(_silu(c_ref[...]), w_ref[...]) + b_ref[...]
    l = lbl_ref[...]
    e = jnp.exp(l - jnp.max(l, axis=0, keepdims=True))
    lb_ref[...] = e[0] / jnp.sum(e, axis=0)


def _mod_call(c8, w_mod, b_mod, lb_logits):
    return pl.pallas_call(
        _mod_kernel,
        out_shape=(jax.ShapeDtypeStruct((8, 3 * D_MODEL), F32),
                   jax.ShapeDtypeStruct((2, D_MODEL), F32)),
        compiler_params=pltpu.CompilerParams(vmem_limit_bytes=VMEM_LIMIT),
        name="mod",
    )(c8, w_mod, b_mod, lb_logits)


def _inproj_kernel(x_ref, mod_ref, w_ref, b_ref, o_ref, u_ref):
    j = pl.program_id(1)

    @pl.when(j == 0)
    def _():
        u_ref[...] = (x_ref[...] * (1.0 + mod_ref[1:2, :]) + mod_ref[0:1, :]).astype(BF16)

    o_ref[...] = (_dot(u_ref[...], w_ref[j]) + b_ref[j]).astype(o_ref.dtype)


def _inproj_call(x2, mod3, w3, b3, *, out_dtype, tm, tiles_per_mod):
    n_tok = x2.shape[0]
    n_col = w3.shape[0]
    return pl.pallas_call(
        _inproj_kernel,
        grid=(n_tok // tm, n_col),
        in_specs=[
            pl.BlockSpec((tm, D_MODEL), lambda i, j: (i, 0)),
            pl.BlockSpec((None, 3, D_MODEL), lambda i, j: (i // tiles_per_mod, 0, 0)),
            pl.BlockSpec((n_col, D_MODEL, D_MODEL), lambda i, j: (0, 0, 0), pipeline_mode=pl.Buffered(1)),
            pl.BlockSpec((n_col, 1, D_MODEL), lambda i, j: (0, 0, 0)),
        ],
        out_specs=pl.BlockSpec((None, None, tm, D_MODEL), lambda i, j: (i, j, 0, 0)),
        out_shape=jax.ShapeDtypeStruct((n_tok // tm, n_col, tm, D_MODEL), out_dtype),
        scratch_shapes=[pltpu.VMEM((tm, D_MODEL), BF16)],
        compiler_params=pltpu.CompilerParams(
            dimension_semantics=("arbitrary", "arbitrary"), vmem_limit_bytes=VMEM_LIMIT),
        name="inproj",
    )(x2, mod3, w3, b3)


LOG2_E = 1.4426950408889634


def _lru_ab(xc, h_r, h_i, br, bi, sp):
    k = (-0.5 * RG_C * LOG2_E) * sp
    a = jnp.exp2(k * jnp.tanh(h_r + 0.5 * br) + k)
    i = 0.5 * jnp.tanh(h_i + 0.5 * bi) + 0.5
    mult = jnp.sqrt(1.0 - a * a)
    return a, mult * (i * xc)


def _softplus(y):
    return jnp.maximum(y, 0.0) + jnp.log(1.0 + jnp.exp(-jnp.abs(y)))


def _ctx_kernel(zf_ref, zb_ref, v_ref, z5_ref, lb_ref, cw_ref, cb_ref, wg_ref, br_ref, bi_ref, lam_ref,
                sf_ref, sb_ref, hf_ref, hb_ref, zp_ref, a_ref, b_ref):
    n = zf_ref.shape[0]
    ri = lax.broadcasted_iota(jnp.int32, (n, n), 0)
    ci = lax.broadcasted_iota(jnp.int32, (n, n), 1)
    tril = (ci <= ri).astype(F32).astype(BF16)
    v = v_ref[...].astype(BF16)

    lf = _log_forget(zf_ref[...], lb_ref[0:1, :])
    g = _cumsum_rows(tril, lf)
    ke = ((1.0 - jnp.exp(lf)) * jnp.exp(g[n - 1:n, :] - g)).astype(BF16)
    for h in range(N_HEADS):
        sl = slice(h * HEAD_DIM, (h + 1) * HEAD_DIM)
        sf_ref[h] = _dot_tn(v[:, sl], ke[:, sl])
    lf = _log_forget(zb_ref[...], lb_ref[1:2, :])
    g = _cumsum_rows(tril, lf)
    ke = ((1.0 - jnp.exp(lf)) * jnp.exp(g - lf)).astype(BF16)
    for h in range(N_HEADS):
        sl = slice(h * HEAD_DIM, (h + 1) * HEAD_DIM)
        sb_ref[h] = _dot_tn(v[:, sl], ke[:, sl])

    zp_ref[0:8, :] = jnp.zeros((8, D_MODEL), F32)
    zp_ref[8 + n:16 + n, :] = jnp.zeros((8, D_MODEL), F32)
    zp_ref[8:8 + n, :] = z5_ref[...]
    xc = cb_ref[...] + zp_ref[7:7 + n, :] * cw_ref[0:1, :]
    for kk in range(1, N_CONV):
        xc = xc + zp_ref[7 + kk:7 + kk + n, :] * cw_ref[kk:kk + 1, :]

    for d, h_ref in ((0, hf_ref), (1, hb_ref)):
        sp = _softplus(-lam_ref[d:d + 1, :])
        for blk in range(N_BLOCKS):
            sl = slice(blk * BLOCK_DIM, (blk + 1) * BLOCK_DIM)
            g = _dot(xc[:, sl].astype(BF16), wg_ref[blk, :, 2 * d * BLOCK_DIM:2 * (d + 1) * BLOCK_DIM])
            a, b = _lru_ab(xc[:, sl], g[:, :BLOCK_DIM], g[:, BLOCK_DIM:],
                           br_ref[d:d + 1, sl], bi_ref[d:d + 1, sl], sp[:, sl])
            a_ref[:, sl] = a
            b_ref[:, sl] = b

        def step(t, h, d=d):
            tt = (n - 1 - t) if d == 1 else t
            return a_ref[pl.ds(tt, 1), :] * h + b_ref[pl.ds(tt, 1), :]

        h_ref[...] = lax.fori_loop(0, n, step, jnp.zeros((1, D_MODEL), F32))


def _ctx_call(feat_c, lb, conv_w, conv_b, wg, br, bi, lam, *, batch, n_ctx):
    feat = lambda j: pl.BlockSpec((None, None, n_ctx, D_MODEL), lambda b, j=j: (b, j, 0, 0))
    full = lambda shape: pl.BlockSpec(shape, lambda b: (0,) * len(shape))
    state = pl.BlockSpec((None, N_HEADS, HEAD_DIM, HEAD_DIM), lambda b: (b, 0, 0, 0))
    hvec = pl.BlockSpec((None, 1, D_MODEL), lambda b: (b, 0, 0))
    return pl.pallas_call(
        _ctx_kernel,
        grid=(batch,),
        in_specs=[feat(0), feat(1), feat(2), feat(3), full((2, D_MODEL)),
                  full((N_CONV, D_MODEL)), full((1, D_MODEL)),
                  full((N_BLOCKS, BLOCK_DIM, 4 * BLOCK_DIM)),
                  full((2, D_MODEL)), full((2, D_MODEL)), full((2, D_MODEL))],
        out_specs=(state, state, hvec, hvec),
        out_shape=(jax.ShapeDtypeStruct((batch, N_HEADS, HEAD_DIM, HEAD_DIM), F32),
                   jax.ShapeDtypeStruct((batch, N_HEADS, HEAD_DIM, HEAD_DIM), F32),
                   jax.ShapeDtypeStruct((batch, 1, D_MODEL), F32),
                   jax.ShapeDtypeStruct((batch, 1, D_MODEL), F32)),
        scratch_shapes=[pltpu.VMEM((n_ctx + 16, D_MODEL), F32),
                        pltpu.VMEM((n_ctx, D_MODEL), F32),
                        pltpu.VMEM((n_ctx, D_MODEL), F32)],
        compiler_params=pltpu.CompilerParams(
            dimension_semantics=("arbitrary",), vmem_limit_bytes=VMEM_LIMIT),
        name="ctx_states",
    )(feat_c, feat_c, feat_c, feat_c, lb, conv_w, conv_b, wg, br, bi, lam)


def _gla_kernel(*refs, reverse, rows, finish):
    if finish:
        (zq_ref, zf_ref, v_ref, lb_ref, s0_ref, other_ref, ng_ref, o_ref,
         st_ref, sn_ref, qd_ref, kit_ref, ke_ref, dec_ref) = refs
    else:
        zq_ref, zf_ref, v_ref, lb_ref, s0_ref, o_ref, st_ref, sn_ref, qd_ref, kit_ref, ke_ref, dec_ref = refs

    @pl.when(pl.program_id(1) == 0)
    def _():
        st_ref[...] = s0_ref[...]
        for h in range(N_HEADS):
            sn_ref[h] = s0_ref[h].astype(BF16).T

    ri = lax.broadcasted_iota(jnp.int32, (GRID_W, GRID_W), 0)
    ci = lax.broadcasted_iota(jnp.int32, (GRID_W, GRID_W), 1)
    allow = (ci >= ri) if reverse else (ci <= ri)
    lb_dir = 1 if reverse else 0

    def row_offset(n):
        n = jnp.minimum(n, rows - 1)
        return pl.multiple_of(((rows - 1 - n) if reverse else n) * GRID_W, GRID_W)

    def prepare(n, slot):
        off = row_offset(n)
        zq = zq_ref[pl.ds(off, GRID_W), :]
        hq = (0.5 * Q_SCALE) * zq
        q = hq * jnp.tanh(0.5 * zq) + hq
        lb = lb_ref[lb_dir:lb_dir + 1, :]
        f = (0.5 + 0.5 * lb) + (0.5 - 0.5 * lb) * jnp.tanh(0.5 * zf_ref[pl.ds(off, GRID_W), :])
        dg, dec = _chunk_cumprod(f, reverse)
        ki = (1.0 - f) / dg
        qd_ref[slot] = (q * dg).astype(BF16)
        kit_ref[slot] = ki.astype(BF16).T
        ke_ref[slot] = (ki * dec).astype(BF16)
        dec_ref[slot] = dec

    def contract(n, slot):
        off = row_offset(n)
        for h in range(N_HEADS):
            sl = slice(h * HEAD_DIM, (h + 1) * HEAD_DIM)
            qd = qd_ref[slot, :, sl]
            v = v_ref[pl.ds(off, GRID_W), sl]
            p = jnp.where(allow, _dot(qd, kit_ref[slot, sl, :]), 0.0).astype(BF16)
            o = _dot(p, v) + _dot(qd, sn_ref[h])
            st = st_ref[h] * dec_ref[slot, :, sl] + _dot_tn(v, ke_ref[slot, :, sl])
            st_ref[h] = st
            sn_ref[h] = st.astype(BF16).T
            if finish:
                o = o + other_ref[pl.ds(off, GRID_W), sl]
                ms = jnp.mean(o * o, axis=-1, keepdims=True)
                o = o * lax.rsqrt(ms + RMS_EPS) * ng_ref[...]
            o_ref[pl.ds(off, GRID_W), sl] = o.astype(o_ref.dtype)

    prepare(0, 0)

    def body(m, carry):
        n = 2 * m
        contract(n, 0)
        prepare(n + 1, 1)
        contract(n + 1, 1)
        prepare(n + 2, 0)
        return carry

    lax.fori_loop(0, rows // 2, body, 0)


def _gla_call(feat, feat_h, lb, f_col, s0, other=None, ng=None, *, reverse, batch, seq, rows):
    tb = rows * GRID_W
    nrb = seq // tb
    finish = other is not None
    per_tile = feat.shape[2] // tb

    def row_block(b, i):
        return b * nrb + ((nrb - 1 - i) if reverse else i)

    def col(c):
        return pl.BlockSpec((None, None, tb, D_MODEL), lambda b, i, c=c: (
            row_block(b, i) // per_tile, c, row_block(b, i) % per_tile, 0))

    tok = pl.BlockSpec((tb, D_MODEL), lambda b, i: (row_block(b, i), 0))
    in_specs = [col(F_Q), col(f_col), col(H_V),
                pl.BlockSpec((2, D_MODEL), lambda b, i: (0, 0)),
                pl.BlockSpec((None, N_HEADS, HEAD_DIM, HEAD_DIM), lambda b, i: (b, 0, 0, 0))]
    args = [feat, feat, feat_h, lb, s0]
    if finish:
        in_specs += [tok, pl.BlockSpec((1, HEAD_DIM), lambda b, i: (0, 0))]
        args += [other, ng]
    return pl.pallas_call(
        functools.partial(_gla_kernel, reverse=reverse, rows=rows, finish=finish),
        grid=(batch, nrb),
        in_specs=in_specs,
        out_specs=tok,
        out_shape=jax.ShapeDtypeStruct((batch * seq, D_MODEL), BF16 if finish else F32),
        scratch_shapes=[pltpu.VMEM((N_HEADS, HEAD_DIM, HEAD_DIM), F32)]
        + [pltpu.VMEM((N_HEADS, HEAD_DIM, HEAD_DIM), BF16), pltpu.VMEM((2, GRID_W, D_MODEL), BF16),
           pltpu.VMEM((2, D_MODEL, GRID_W), BF16), pltpu.VMEM((2, GRID_W, D_MODEL), BF16),
           pltpu.VMEM((2, 1, D_MODEL), F32)],
        compiler_params=pltpu.CompilerParams(
            dimension_semantics=("arbitrary", "arbitrary"), vmem_limit_bytes=VMEM_LIMIT),
        name="gla_bwd" if reverse else "gla_fwd",
    )(*args)


def _lru_kernel(z5_ref, cw_ref, cb_ref, wg_ref, br_ref, bi_ref, lam_ref, h0f_ref, h0b_ref, o_ref,
                zp_ref, af_ref, bf_ref, ab_ref, bb_ref, cf_ref, cr_ref, *, n_rows, rows_per_step):
    w = GRID_W
    seq = n_rows * w
    tb = rows_per_step * w
    n_steps = n_rows // rows_per_step

    zp_ref[0:w, :] = jnp.zeros((w, BLOCK_DIM), F32)
    zp_ref[w + seq:, :] = jnp.zeros((2 * w, BLOCK_DIM), F32)
    tile = z5_ref.shape[1]
    for k in range(z5_ref.shape[0]):
        zp_ref[w + k * tile:w + (k + 1) * tile, :] = z5_ref[k]

    sp = [_softplus(-lam_ref[d:d + 1, :]) for d in (0, 1)]

    def gates(s, c):
        off = pl.multiple_of(s * tb, tb)
        xc = cb_ref[...] + zp_ref[pl.ds(off, tb), :] * cw_ref[0:1, :]
        for kk in range(1, N_CONV):
            xc = xc + zp_ref[pl.ds(off + kk * w, tb), :] * cw_ref[kk:kk + 1, :]
        g = _dot(xc.astype(BF16), wg_ref[...])
        for d, a_ref, b_ref in ((0, af_ref, bf_ref), (1, ab_ref, bb_ref)):
            c0 = 2 * d * BLOCK_DIM
            a, b = _lru_ab(xc, g[:, c0:c0 + BLOCK_DIM], g[:, c0 + BLOCK_DIM:c0 + 2 * BLOCK_DIM],
                           br_ref[d:d + 1, :], bi_ref[d:d + 1, :], sp[d])
            a_ref[pl.ds(off, tb), :] = a
            b_ref[pl.ds(off, tb), :] = b
        return c

    lax.fori_loop(0, n_steps, gates, 0)

    def scan(n, carry):
        hf, pf, hb, pb = carry
        off_f = pl.multiple_of(n * w, w)
        off_b = pl.multiple_of((n_rows - 1 - n) * w, w)
        a = af_ref[pl.ds(off_f, w), :]
        hf = a * hf + bf_ref[pl.ds(off_f, w), :]
        pf = pf * a
        bf_ref[pl.ds(off_f, w), :] = hf
        af_ref[pl.ds(off_f, w), :] = pf
        a = ab_ref[pl.ds(off_b, w), :]
        hb = a * hb + bb_ref[pl.ds(off_b, w), :]
        pb = pb * a
        bb_ref[pl.ds(off_b, w), :] = hb
        ab_ref[pl.ds(off_b, w), :] = pb
        return hf, pf, hb, pb

    zeros = jnp.zeros((w, BLOCK_DIM), F32)
    ones = jnp.ones((w, BLOCK_DIM), F32)
    lax.fori_loop(0, n_rows, scan, (zeros, ones, zeros, ones))

    last = (n_rows - 1) * w

    def carry(n, c):
        cf, cb = c
        cf_ref[pl.ds(n, 1), :] = cf
        cf = bf_ref[pl.ds(last + n, 1), :] + af_ref[pl.ds(last + n, 1), :] * cf
        col = w - 1 - n
        cr_ref[pl.ds(col, 1), :] = cb
        cb = bb_ref[pl.ds(col, 1), :] + ab_ref[pl.ds(col, 1), :] * cb
        return cf, cb

    lax.fori_loop(0, w, carry, (h0f_ref[...], h0b_ref[...]))

    def fix(s, c):
        off = pl.multiple_of(s * tb, tb)
        cf = jnp.concatenate([cf_ref[...]] * rows_per_step, axis=0)
        cr = jnp.concatenate([cr_ref[...]] * rows_per_step, axis=0)
        h = (bf_ref[pl.ds(off, tb), :] + af_ref[pl.ds(off, tb), :] * cf
             + bb_ref[pl.ds(off, tb), :] + ab_ref[pl.ds(off, tb), :] * cr)
        o_ref[pl.ds(off, tb), :] = h.astype(o_ref.dtype)
        return c

    lax.fori_loop(0, n_steps, fix, 0)


def _lru_call(feat, conv_w, conv_b, wg, br, bi, lam, h0f, h0b, *, batch, seq):
    n_rows = seq // GRID_W
    vec = lambda n: pl.BlockSpec((n, BLOCK_DIM), lambda b, c: (0, c))
    h0 = pl.BlockSpec((None, 1, BLOCK_DIM), lambda b, c: (b, 0, c))
    buf = pltpu.VMEM((seq, BLOCK_DIM), F32)
    n_i, n_col, tm, _ = feat.shape
    feat5 = feat.reshape(batch, n_i // batch, n_col, tm, D_MODEL)
    return pl.pallas_call(
        functools.partial(_lru_kernel, n_rows=n_rows, rows_per_step=8),
        grid=(batch, N_BLOCKS),
        in_specs=[pl.BlockSpec((None, n_i // batch, None, tm, BLOCK_DIM), lambda b, c: (b, 0, F_Z5, 0, c)),
                  vec(N_CONV), vec(1),
                  pl.BlockSpec((None, BLOCK_DIM, 4 * BLOCK_DIM), lambda b, c: (c, 0, 0)),
                  vec(2), vec(2), vec(2), h0, h0],
        out_specs=pl.BlockSpec((seq, BLOCK_DIM), lambda b, c: (b, c)),
        out_shape=jax.ShapeDtypeStruct((batch * seq, D_MODEL), BF16),
        scratch_shapes=[pltpu.VMEM((seq + 3 * GRID_W, BLOCK_DIM), F32), buf, buf, buf, buf,
                        pltpu.VMEM((GRID_W, BLOCK_DIM), F32), pltpu.VMEM((GRID_W, BLOCK_DIM), F32)],
        compiler_params=pltpu.CompilerParams(
            dimension_semantics=("arbitrary", "arbitrary"), vmem_limit_bytes=VMEM_LIMIT),
        name="lru",
    )(feat5, conv_w, conv_b, wg, br, bi, lam, h0f, h0b)


def _merge_kernel(oa_ref, hx_ref, z4_ref, z6_ref, z7_ref, z8_ref, x_ref, mod_ref,
                  pa_ref, pb_ref, wo_ref, lg_ref, lbias_ref, o_ref):
    f32 = lambda ref: ref[...].astype(F32)
    o_a = (f32(oa_ref) * _silu(f32(z4_ref))).astype(BF16)
    o_b = (f32(hx_ref) * _silu(f32(z6_ref))).astype(BF16)
    y = (_sigmoid(f32(z7_ref)) * _dot(o_a, pa_ref[...])
         + _sigmoid(f32(z8_ref)) * _dot(o_b, pb_ref[...]))
    y = _dot(y.astype(BF16), wo_ref[...])
    t = DEEPNORM_ALPHA * x_ref[...] + mod_ref[2:3, :] * y
    mu = jnp.mean(t, axis=-1, keepdims=True)
    tc = t - mu
    var = jnp.mean(tc * tc, axis=-1, keepdims=True)
    o_ref[...] = tc * lax.rsqrt(var + LN_EPS) * lg_ref[...] + lbias_ref[...]


def _merge_call(oa, hx, feat_h, x2, mod3, pa, pb, wo, lg, lbias, *, tm, tiles_per_batch):
    n_tok = x2.shape[0]
    tok = pl.BlockSpec((tm, D_MODEL), lambda i: (i, 0))
    per_tile = feat_h.shape[2] // tm
    col = lambda j: pl.BlockSpec((None, None, tm, D_MODEL), lambda i, j=j: (i // per_tile, j, i % per_tile, 0))
    full = lambda shape: pl.BlockSpec(shape, lambda i: (0,) * len(shape))
    return pl.pallas_call(
        _merge_kernel,
        grid=(n_tok // tm,),
        in_specs=[tok, tok, col(H_G4), col(H_G6), col(H_M7), col(H_M8), tok,
                  pl.BlockSpec((None, 3, D_MODEL), lambda i: (i // tiles_per_batch, 0, 0)),
                  full((D_MODEL, D_MODEL)), full((D_MODEL, D_MODEL)),
                  full((D_MODEL, D_MODEL)), full((1, D_MODEL)), full((1, D_MODEL))],
        out_specs=tok,
        out_shape=jax.ShapeDtypeStruct((n_tok, D_MODEL), F32),
        compiler_params=pltpu.CompilerParams(
            dimension_semantics=("arbitrary",), vmem_limit_bytes=VMEM_LIMIT),
        name="merge",
    )(oa, hx, feat_h, feat_h, feat_h, feat_h, x2, mod3, pa, pb, wo, lg, lbias)


def kernel(x, c, ctx, c_ctx, w_mod, b_mod, w_in, b_in, lb_logits, norm_a_g, conv_w, conv_b,
           w_r, b_r, w_i, b_i, lam, p_a, p_b, w_out, ln_g, ln_b):
    batch, seq, d = x.shape
    n_ctx = ctx.shape[1]
    assert d == D_MODEL and seq % GRID_W == 0 and w_in.shape[0] == 1

    def select_cols(cols):
        w3 = jnp.stack([w_in[0][:, j * D_MODEL:(j + 1) * D_MODEL] for j in cols]).astype(BF16)
        b3 = jnp.stack([b_in[0][j * D_MODEL:(j + 1) * D_MODEL] for j in cols])[:, None, :]
        return w3, b3

    wg = (0.5 * jnp.concatenate([w_r[0, 0], w_i[0, 0], w_r[0, 1], w_i[0, 1]], axis=-1)).astype(BF16)
    c8 = jnp.zeros((8, D_MODEL), F32).at[:batch].set(c).at[batch].set(c_ctx)

    mod, lb = _mod_call(c8, w_mod[0], b_mod[0][None, :], lb_logits)
    mod3 = mod.reshape(8, 3, D_MODEL)

    x2 = x.reshape(batch * seq, D_MODEL)
    tm = 2048
    feat =_inproj_call(x2, mod3, *select_cols(LATENT_F32), out_dtype=F32, tm=tm, tiles_per_mod=seq // tm)
    feat_h = _inproj_call(x2, mod3, *select_cols(LATENT_BF16), out_dtype=BF16, tm=tm, tiles_per_mod=seq // tm)
    feat_c = _inproj_call(ctx.reshape(batch * n_ctx, D_MODEL), mod3[batch:batch + 1], *select_cols(CTX_F32),
                          out_dtype=F32, tm=n_ctx, tiles_per_mod=batch)

    cw, cb = conv_w[0], conv_b[0][None, :]
    s0f, s0b, h0f, h0b = _ctx_call(feat_c, lb, cw, cb, wg, b_r[0], b_i[0], lam[0], batch=batch, n_ctx=n_ctx)

    o_b = _gla_call(feat, feat_h, lb, F_F1, s0b, reverse=True, batch=batch, seq=seq, rows=16)
    oa = _gla_call(feat, feat_h, lb, F_F0, s0f, o_b, norm_a_g[0][None, :],
                   reverse=False, batch=batch, seq=seq, rows=16)
    hx = _lru_call(feat, cw, cb, wg, b_r[0], b_i[0], lam[0], h0f, h0b, batch=batch, seq=seq)

    out = _merge_call(oa, hx, feat_h, x2, mod3,
                      p_a[0].astype(BF16), p_b[0].astype(BF16), w_out[0].astype(BF16),
                      ln_g[0][None, :], ln_b[0][None, :], tm=512, tiles_per_batch=seq // 512)
    return out.reshape(batch, seq, D_MODEL)
```

```python
import functools

import jax
import jax.numpy as jnp
from jax import lax
from jax.experimental import pallas as pl
from jax.experimental.pallas import tpu as pltpu

F32 = jnp.float32
BF16 = jnp.bfloat16

D_MODEL = 1024
GRID_W = 64
HEAD_DIM = 128
N_HEADS = D_MODEL // HEAD_DIM
N_BLOCKS = 8
BLOCK_DIM = D_MODEL // N_BLOCKS
N_CONV = 4
RG_C = 8.0
LN_EPS = 1e-5
RMS_EPS = 1e-6
DEEPNORM_ALPHA = 2.0 ** 0.25
Q_SCALE = HEAD_DIM ** -0.5

VMEM_LIMIT = 56 * 1024 * 1024

LATENT_F32 = (0, 1, 2, 5)
LATENT_BF16 = (3, 4, 6, 7, 8)
CTX_F32 = (1, 2, 3, 5)
F_Q, F_F0, F_F1, F_Z5 = 0, 1, 2, 3
H_V, H_G4, H_G6, H_M7, H_M8 = 0, 1, 2, 3, 4


def _sigmoid(x):
    return 0.5 * jnp.tanh(0.5 * x) + 0.5


def _silu(x):
    h = 0.5 * x
    return h * jnp.tanh(h) + h


def _log_forget(z, lb):
    return jnp.log((0.5 + 0.5 * lb) + (0.5 - 0.5 * lb) * jnp.tanh(0.5 * z))


def _dot(a, b):
    return jnp.dot(a, b, preferred_element_type=F32)


def _dot_nt(a, b):
    return lax.dot_general(a, b, (((1,), (1,)), ((), ())), preferred_element_type=F32)


def _dot_tn(a, b):
    return lax.dot_general(a, b, (((0,), (0,)), ((), ())), preferred_element_type=F32)


def _chunk_cumprod(x, reverse):
    n, c = x.shape
    nb = n // 8
    y = x.reshape(nb, 8, c)
    sub = lax.broadcasted_iota(jnp.int32, (nb, 8, c), 1)
    for s in (1, 2, 4):
        if reverse:
            y = y * jnp.where(sub < 8 - s, pltpu.roll(y, 8 - s, axis=1), 1.0)
        else:
            y = y * jnp.where(sub >= s, pltpu.roll(y, s, axis=1), 1.0)
    offs = [None] * nb
    acc = jnp.ones((1, c), F32)
    for b in (reversed(range(nb)) if reverse else range(nb)):
        offs[b] = acc
        acc = acc * (y[b, 0:1, :] if reverse else y[b, 7:8, :])
    return (y * jnp.stack(offs)).reshape(n, c), acc


def _cumsum_rows(mask_bf16, x):
    hi = x.astype(BF16)
    r1 = x - hi.astype(F32)
    mid = r1.astype(BF16)
    lo = (r1 - mid.astype(F32)).astype(BF16)
    return _dot(mask_bf16, hi) + _dot(mask_bf16, mid) + _dot(mask_bf16, lo)


def _mod_kernel(c_ref, w_ref, b_ref, lbl_ref, mod_ref, lb_ref):
    mod_ref[...] = _dot(_silu(c_ref[...]), w_ref[...]) + b_ref[...]
    l = lbl_ref[...]
    e = jnp.exp(l - jnp.max(l, axis=0, keepdims=True))
    lb_ref[...] = e[0] / jnp.sum(e, axis=0)


def _mod_call(c8, w_mod, b_mod, lb_logits):
    return pl.pallas_call(
        _mod_kernel,
        out_shape=(jax.ShapeDtypeStruct((8, 3 * D_MODEL), F32),
                   jax.ShapeDtypeStruct((2, D_MODEL), F32)),
        compiler_params=pltpu.CompilerParams(vmem_limit_bytes=VMEM_LIMIT),
        name="mod",
    )(c8, w_mod, b_mod, lb_logits)


def _inproj_kernel(x_ref, mod_ref, w_ref, b_ref, o_ref, u_ref):
    j = pl.program_id(1)

    @pl.when(j == 0)
    def _():
        u_ref[...] = (x_ref[...] * (1.0 + mod_ref[1:2, :]) + mod_ref[0:1, :]).astype(BF16)

    o_ref[...] = (_dot(u_ref[...], w_ref[j]) + b_ref[j]).astype(o_ref.dtype)


def _inproj_call(x2, mod3, w3, b3, *, out_dtype, tm, tiles_per_mod):
    n_tok = x2.shape[0]
    n_col = w3.shape[0]
    return pl.pallas_call(
        _inproj_kernel,
        grid=(n_tok // tm, n_col),
        in_specs=[
            pl.BlockSpec((tm, D_MODEL), lambda i, j: (i, 0)),
            pl.BlockSpec((None, 3, D_MODEL), lambda i, j: (i // tiles_per_mod, 0, 0)),
            pl.BlockSpec((n_col, D_MODEL, D_MODEL), lambda i, j: (0, 0, 0), pipeline_mode=pl.Buffered(1)),
            pl.BlockSpec((n_col, 1, D_MODEL), lambda i, j: (0, 0, 0)),
        ],
        out_specs=pl.BlockSpec((None, None, tm, D_MODEL), lambda i, j: (i, j, 0, 0)),
        out_shape=jax.ShapeDtypeStruct((n_tok // tm, n_col, tm, D_MODEL), out_dtype),
        scratch_shapes=[pltpu.VMEM((tm, D_MODEL), BF16)],
        compiler_params=pltpu.CompilerParams(
            dimension_semantics=("arbitrary", "arbitrary"), vmem_limit_bytes=VMEM_LIMIT),
        name="inproj",
    )(x2, mod3, w3, b3)


LOG2_E = 1.4426950408889634


def _lru_ab(xc, h_r, h_i, br, bi, sp):
    k = (-0.5 * RG_C * LOG2_E) * sp
    a = jnp.exp2(k * jnp.tanh(h_r + 0.5 * br) + k)
    i = 0.5 * jnp.tanh(h_i + 0.5 * bi) + 0.5
    mult = jnp.sqrt(1.0 - a * a)
    return a, mult * (i * xc)


def _softplus(y):
    return jnp.maximum(y, 0.0) + jnp.log(1.0 + jnp.exp(-jnp.abs(y)))


def _ctx_kernel(zf_ref, zb_ref, v_ref, z5_ref, lb_ref, cw_ref, cb_ref, wg_ref, br_ref, bi_ref, lam_ref,
                sf_ref, sb_ref, hf_ref, hb_ref, zp_ref, a_ref, b_ref):
    n = zf_ref.shape[0]
    ri = lax.broadcasted_iota(jnp.int32, (n, n), 0)
    ci = lax.broadcasted_iota(jnp.int32, (n, n), 1)
    tril = (ci <= ri).astype(F32).astype(BF16)
    v = v_ref[...].astype(BF16)

    lf = _log_forget(zf_ref[...], lb_ref[0:1, :])
    g = _cumsum_rows(tril, lf)
    ke = ((1.0 - jnp.exp(lf)) * jnp.exp(g[n - 1:n, :] - g)).astype(BF16)
    for h in range(N_HEADS):
        sl = slice(h * HEAD_DIM, (h + 1) * HEAD_DIM)
        sf_ref[h] = _dot_tn(v[:, sl], ke[:, sl])
    lf = _log_forget(zb_ref[...], lb_ref[1:2, :])
    g = _cumsum_rows(tril, lf)
    ke = ((1.0 - jnp.exp(lf)) * jnp.exp(g - lf)).astype(BF16)
    for h in range(N_HEADS):
        sl = slice(h * HEAD_DIM, (h + 1) * HEAD_DIM)
        sb_ref[h] = _dot_tn(v[:, sl], ke[:, sl])

    zp_ref[0:8, :] = jnp.zeros((8, D_MODEL), F32)
    zp_ref[8 + n:16 + n, :] = jnp.zeros((8, D_MODEL), F32)
    zp_ref[8:8 + n, :] = z5_ref[...]
    xc = cb_ref[...] + zp_ref[7:7 + n, :] * cw_ref[0:1, :]
    for kk in range(1, N_CONV):
        xc = xc + zp_ref[7 + kk:7 + kk + n, :] * cw_ref[kk:kk + 1, :]

    for d, h_ref in ((0, hf_ref), (1, hb_ref)):
        sp = _softplus(-lam_ref[d:d + 1, :])
        for blk in range(N_BLOCKS):
            sl = slice(blk * BLOCK_DIM, (blk + 1) * BLOCK_DIM)
            g = _dot(xc[:, sl].astype(BF16), wg_ref[blk, :, 2 * d * BLOCK_DIM:2 * (d + 1) * BLOCK_DIM])
            a, b = _lru_ab(xc[:, sl], g[:, :BLOCK_DIM], g[:, BLOCK_DIM:],
                           br_ref[d:d + 1, sl], bi_ref[d:d + 1, sl], sp[:, sl])
            a_ref[:, sl] = a
            b_ref[:, sl] = b

        def step(t, h, d=d):
            tt = (n - 1 - t) if d == 1 else t
            return a_ref[pl.ds(tt, 1), :] * h + b_ref[pl.ds(tt, 1), :]

        h_ref[...] = lax.fori_loop(0, n, step, jnp.zeros((1, D_MODEL), F32))


def _ctx_call(feat_c, lb, conv_w, conv_b, wg, br, bi, lam, *, batch, n_ctx):
    feat = lambda j: pl.BlockSpec((None, None, n_ctx, D_MODEL), lambda b, j=j: (b, j, 0, 0))
    full = lambda shape: pl.BlockSpec(shape, lambda b: (0,) * len(shape))
    state = pl.BlockSpec((None, N_HEADS, HEAD_DIM, HEAD_DIM), lambda b: (b, 0, 0, 0))
    hvec = pl.BlockSpec((None, 1, D_MODEL), lambda b: (b, 0, 0))
    return pl.pallas_call(
        _ctx_kernel,
        grid=(batch,),
        in_specs=[feat(0), feat(1), feat(2), feat(3), full((2, D_MODEL)),
                  full((N_CONV, D_MODEL)), full((1, D_MODEL)),
                  full((N_BLOCKS, BLOCK_DIM, 4 * BLOCK_DIM)),
                  full((2, D_MODEL)), full((2, D_MODEL)), full((2, D_MODEL))],
        out_specs=(state, state, hvec, hvec),
        out_shape=(jax.ShapeDtypeStruct((batch, N_HEADS, HEAD_DIM, HEAD_DIM), F32),
                   jax.ShapeDtypeStruct((batch, N_HEADS, HEAD_DIM, HEAD_DIM), F32),
                   jax.ShapeDtypeStruct((batch, 1, D_MODEL), F32),
                   jax.ShapeDtypeStruct((batch, 1, D_MODEL), F32)),
        scratch_shapes=[pltpu.VMEM((n_ctx + 16, D_MODEL), F32),
                        pltpu.VMEM((n_ctx, D_MODEL), F32),
                        pltpu.VMEM((n_ctx, D_MODEL), F32)],
        compiler_params=pltpu.CompilerParams(
            dimension_semantics=("arbitrary",), vmem_limit_bytes=VMEM_LIMIT),
        name="ctx_states",
    )(feat_c, feat_c, feat_c, feat_c, lb, conv_w, conv_b, wg, br, bi, lam)


GLA_ROWS_PER_ITER = 8


def _gla_kernel(*refs, reverse, rows, finish):
    if finish:
        (zq_ref, zf_ref, v_ref, lb_ref, s0_ref, other_ref, o_ref,
         st_ref, sn_ref, qd_ref, kit_ref, ke_ref, dec_ref) = refs
    else:
        zq_ref, zf_ref, v_ref, lb_ref, s0_ref, o_ref, st_ref, sn_ref, qd_ref, kit_ref, ke_ref, dec_ref = refs

    @pl.when(pl.program_id(1) == 0)
    def _():
        st_ref[...] = s0_ref[...]
        for h in range(N_HEADS):
            sn_ref[h] = s0_ref[h].astype(BF16).T

    ri = lax.broadcasted_iota(jnp.int32, (GRID_W, GRID_W), 0)
    ci = lax.broadcasted_iota(jnp.int32, (GRID_W, GRID_W), 1)
    allow = (ci >= ri) if reverse else (ci <= ri)
    lb_dir = 1 if reverse else 0

    def row_offset(n):
        n = jnp.minimum(n, rows - 1)
        return pl.multiple_of(((rows - 1 - n) if reverse else n) * GRID_W, GRID_W)

    def prepare(n, slot):
        off = row_offset(n)
        zq = zq_ref[pl.ds(off, GRID_W), :]
        hq = (0.5 * Q_SCALE) * zq
        q = hq * jnp.tanh(0.5 * zq) + hq
        lb = lb_ref[lb_dir:lb_dir + 1, :]
        f = (0.5 + 0.5 * lb) + (0.5 - 0.5 * lb) * jnp.tanh(0.5 * zf_ref[pl.ds(off, GRID_W), :])
        dg, dec = _chunk_cumprod(f, reverse)
        ki = (1.0 - f) / dg
        qd_ref[slot] = (q * dg).astype(BF16)
        kit_ref[slot] = ki.astype(BF16).T
        ke_ref[slot] = (ki * dec).astype(BF16)
        dec_ref[slot] = dec

    def contract(n, slot):
        off = row_offset(n)
        for h in range(N_HEADS):
            sl = slice(h * HEAD_DIM, (h + 1) * HEAD_DIM)
            qd = qd_ref[slot, :, sl]
            v = v_ref[pl.ds(off, GRID_W), sl]
            p = jnp.where(allow, _dot(qd, kit_ref[slot, sl, :]), 0.0).astype(BF16)
            o = _dot(p, v) + _dot(qd, sn_ref[h])
            st = st_ref[h] * dec_ref[slot, :, sl] + _dot_tn(v, ke_ref[slot, :, sl])
            st_ref[h] = st
            sn_ref[h] = st.astype(BF16).T
            if finish:
                o = o + other_ref[pl.ds(off, GRID_W), sl]
            o_ref[pl.ds(off, GRID_W), sl] = o.astype(o_ref.dtype)

    prepare(0, 0)

    def body(m, carry):
        n = GLA_ROWS_PER_ITER * m
        for i in range(GLA_ROWS_PER_ITER):
            contract(n + i, i % 2)
            prepare(n + i + 1, (i + 1) % 2)
        return carry

    lax.fori_loop(0, rows // GLA_ROWS_PER_ITER, body, 0)


def _gla_call(feat, feat_h, lb, f_col, s0, other=None, *, reverse, batch, seq, rows):
    tb = rows * GRID_W
    nrb = seq // tb
    finish = other is not None
    per_tile = feat.shape[2] // tb

    def row_block(b, i):
        return b * nrb + ((nrb - 1 - i) if reverse else i)

    def col(c):
        return pl.BlockSpec((None, None, tb, D_MODEL), lambda b, i, c=c: (
            row_block(b, i) // per_tile, c, row_block(b, i) % per_tile, 0))

    tok = pl.BlockSpec((tb, D_MODEL), lambda b, i: (row_block(b, i), 0))
    in_specs = [col(F_Q), col(f_col), col(H_V),
                pl.BlockSpec((2, D_MODEL), lambda b, i: (0, 0)),
                pl.BlockSpec((None, N_HEADS, HEAD_DIM, HEAD_DIM), lambda b, i: (b, 0, 0, 0))]
    args = [feat, feat, feat_h, lb, s0]
    if finish:
        in_specs += [tok]
        args += [other]
    return pl.pallas_call(
        functools.partial(_gla_kernel, reverse=reverse, rows=rows, finish=finish),
        grid=(batch, nrb),
        in_specs=in_specs,
        out_specs=tok,
        out_shape=jax.ShapeDtypeStruct((batch * seq, D_MODEL), BF16 if finish else F32),
        scratch_shapes=[pltpu.VMEM((N_HEADS, HEAD_DIM, HEAD_DIM), F32)]
        + [pltpu.VMEM((N_HEADS, HEAD_DIM, HEAD_DIM), BF16), pltpu.VMEM((2, GRID_W, D_MODEL), BF16),
           pltpu.VMEM((2, D_MODEL, GRID_W), BF16), pltpu.VMEM((2, GRID_W, D_MODEL), BF16),
           pltpu.VMEM((2, 1, D_MODEL), F32)],
        compiler_params=pltpu.CompilerParams(
            dimension_semantics=("arbitrary", "arbitrary"), vmem_limit_bytes=VMEM_LIMIT),
        name="gla_bwd" if reverse else "gla_fwd",
    )(*args)


def _lru_kernel(z5_ref, cw_ref, cb_ref, wg_ref, br_ref, bi_ref, lam_ref, h0f_ref, h0b_ref, o_ref,
                zp_ref, af_ref, bf_ref, ab_ref, bb_ref, cf_ref, cr_ref, *, n_rows, rows_per_step):
    w = GRID_W
    seq = n_rows * w
    tb = rows_per_step * w
    n_steps = n_rows // rows_per_step

    zp_ref[0:w, :] = jnp.zeros((w, BLOCK_DIM), F32)
    zp_ref[w + seq:, :] = jnp.zeros((2 * w, BLOCK_DIM), F32)
    tile = z5_ref.shape[1]
    for k in range(z5_ref.shape[0]):
        zp_ref[w + k * tile:w + (k + 1) * tile, :] = z5_ref[k]

    sp = [_softplus(-lam_ref[d:d + 1, :]) for d in (0, 1)]

    def gates(s, c):
        off = pl.multiple_of(s * tb, tb)
        xc = cb_ref[...] + zp_ref[pl.ds(off, tb), :] * cw_ref[0:1, :]
        for kk in range(1, N_CONV):
            xc = xc + zp_ref[pl.ds(off + kk * w, tb), :] * cw_ref[kk:kk + 1, :]
        g = _dot(xc.astype(BF16), wg_ref[...])
        for d, a_ref, b_ref in ((0, af_ref, bf_ref), (1, ab_ref, bb_ref)):
            c0 = 2 * d * BLOCK_DIM
            a, b = _lru_ab(xc, g[:, c0:c0 + BLOCK_DIM], g[:, c0 + BLOCK_DIM:c0 + 2 * BLOCK_DIM],
                           br_ref[d:d + 1, :], bi_ref[d:d + 1, :], sp[d])
            a_ref[pl.ds(off, tb), :] = a
            b_ref[pl.ds(off, tb), :] = b
        return c

    lax.fori_loop(0, n_steps, gates, 0)

    def scan(n, carry):
        hf, pf, hb, pb = carry
        off_f = pl.multiple_of(n * w, w)
        off_b = pl.multiple_of((n_rows - 1 - n) * w, w)
        a = af_ref[pl.ds(off_f, w), :]
        hf = a * hf + bf_ref[pl.ds(off_f, w), :]
        pf = pf * a
        bf_ref[pl.ds(off_f, w), :] = hf
        af_ref[pl.ds(off_f, w), :] = pf
        a = ab_ref[pl.ds(off_b, w), :]
        hb = a * hb + bb_ref[pl.ds(off_b, w), :]
        pb = pb * a
        bb_ref[pl.ds(off_b, w), :] = hb
        ab_ref[pl.ds(off_b, w), :] = pb
        return hf, pf, hb, pb

    zeros = jnp.zeros((w, BLOCK_DIM), F32)
    ones = jnp.ones((w, BLOCK_DIM), F32)
    lax.fori_loop(0, n_rows, scan, (zeros, ones, zeros, ones))

    last = (n_rows - 1) * w

    def carry(n, c):
        cf, cb = c
        cf_ref[pl.ds(n, 1), :] = cf
        cf = bf_ref[pl.ds(last + n, 1), :] + af_ref[pl.ds(last + n, 1), :] * cf
        col = w - 1 - n
        cr_ref[pl.ds(col, 1), :] = cb
        cb = bb_ref[pl.ds(col, 1), :] + ab_ref[pl.ds(col, 1), :] * cb
        return cf, cb

    lax.fori_loop(0, w, carry, (h0f_ref[...], h0b_ref[...]))

    def fix(s, c):
        off = pl.multiple_of(s * tb, tb)
        cf = jnp.concatenate([cf_ref[...]] * rows_per_step, axis=0)
        cr = jnp.concatenate([cr_ref[...]] * rows_per_step, axis=0)
        h = (bf_ref[pl.ds(off, tb), :] + af_ref[pl.ds(off, tb), :] * cf
             + bb_ref[pl.ds(off, tb), :] + ab_ref[pl.ds(off, tb), :] * cr)
        o_ref[pl.ds(off, tb), :] = h.astype(o_ref.dtype)
        return c

    lax.fori_loop(0, n_steps, fix, 0)


def _lru_call(feat, conv_w, conv_b, wg, br, bi, lam, h0f, h0b, *, batch, seq):
    n_rows = seq // GRID_W
    vec = lambda n: pl.BlockSpec((n, BLOCK_DIM), lambda b, c: (0, c))
    h0 = pl.BlockSpec((None, 1, BLOCK_DIM), lambda b, c: (b, 0, c))
    buf = pltpu.VMEM((seq, BLOCK_DIM), F32)
    n_i, n_col, tm, _ = feat.shape
    feat5 = feat.reshape(batch, n_i // batch, n_col, tm, D_MODEL)
    return pl.pallas_call(
        functools.partial(_lru_kernel, n_rows=n_rows, rows_per_step=8),
        grid=(batch, N_BLOCKS),
        in_specs=[pl.BlockSpec((None, n_i // batch, None, tm, BLOCK_DIM), lambda b, c: (b, 0, F_Z5, 0, c)),
                  vec(N_CONV), vec(1),
                  pl.BlockSpec((None, BLOCK_DIM, 4 * BLOCK_DIM), lambda b, c: (c, 0, 0)),
                  vec(2), vec(2), vec(2), h0, h0],
        out_specs=pl.BlockSpec((seq, BLOCK_DIM), lambda b, c: (b, c)),
        out_shape=jax.ShapeDtypeStruct((batch * seq, D_MODEL), BF16),
        scratch_shapes=[pltpu.VMEM((seq + 3 * GRID_W, BLOCK_DIM), F32), buf, buf, buf, buf,
                        pltpu.VMEM((GRID_W, BLOCK_DIM), F32), pltpu.VMEM((GRID_W, BLOCK_DIM), F32)],
        compiler_params=pltpu.CompilerParams(
            dimension_semantics=("arbitrary", "arbitrary"), vmem_limit_bytes=VMEM_LIMIT),
        name="lru",
    )(feat5, conv_w, conv_b, wg, br, bi, lam, h0f, h0b)


def _merge_kernel(oa_ref, hx_ref, z4_ref, z6_ref, z7_ref, z8_ref, x_ref, mod_ref, ng_ref,
                  pa_ref, pb_ref, wo_ref, lg_ref, lbias_ref, o_ref):
    f32 = lambda ref: ref[...].astype(F32)
    o_b = (f32(hx_ref) * _silu(f32(z6_ref))).astype(BF16)
    y = _sigmoid(f32(z8_ref)) * _dot(o_b, pb_ref[...])
    y_a = None
    for pair in range(N_HEADS // 2):
        o_h = []
        for h in (2 * pair, 2 * pair + 1):
            sl = slice(h * HEAD_DIM, (h + 1) * HEAD_DIM)
            t = oa_ref[:, sl].astype(F32)
            ms = jnp.mean(t * t, axis=-1, keepdims=True)
            o_h.append((t * lax.rsqrt(ms + RMS_EPS) * ng_ref[...] * _silu(z4_ref[:, sl].astype(F32))).astype(BF16))
        rows = slice(2 * pair * HEAD_DIM, 2 * (pair + 1) * HEAD_DIM)
        part = _dot(jnp.concatenate(o_h, axis=-1), pa_ref[rows, :])
        y_a = part if y_a is None else y_a + part
    y = y + _sigmoid(f32(z7_ref)) * y_a
    y = _dot(y.astype(BF16), wo_ref[...])
    t = DEEPNORM_ALPHA * x_ref[...] + mod_ref[2:3, :] * y
    mu = jnp.mean(t, axis=-1, keepdims=True)
    tc = t - mu
    var = jnp.mean(tc * tc, axis=-1, keepdims=True)
    o_ref[...] = tc * lax.rsqrt(var + LN_EPS) * lg_ref[...] + lbias_ref[...]


def _merge_call(oa, hx, feat_h, x2, mod3, ng, pa, pb, wo, lg, lbias, *, tm, tiles_per_batch):
    n_tok = x2.shape[0]
    tok = pl.BlockSpec((tm, D_MODEL), lambda i: (i, 0))
    per_tile = feat_h.shape[2] // tm
    col = lambda j: pl.BlockSpec((None, None, tm, D_MODEL), lambda i, j=j: (i // per_tile, j, i % per_tile, 0))
    full = lambda shape: pl.BlockSpec(shape, lambda i: (0,) * len(shape), pipeline_mode=pl.Buffered(1))
    return pl.pallas_call(
        _merge_kernel,
        grid=(n_tok // tm,),
        in_specs=[tok, tok, col(H_G4), col(H_G6), col(H_M7), col(H_M8), tok,
                  pl.BlockSpec((None, 3, D_MODEL), lambda i: (i // tiles_per_batch, 0, 0)),
                  full((1, HEAD_DIM)), full((D_MODEL, D_MODEL)), full((D_MODEL, D_MODEL)),
                  full((D_MODEL, D_MODEL)), full((1, D_MODEL)), full((1, D_MODEL))],
        out_specs=tok,
        out_shape=jax.ShapeDtypeStruct((n_tok, D_MODEL), F32),
        compiler_params=pltpu.CompilerParams(
            dimension_semantics=("arbitrary",), vmem_limit_bytes=VMEM_LIMIT),
        name="merge",
    )(oa, hx, feat_h, feat_h, feat_h, feat_h, x2, mod3, ng, pa, pb, wo, lg, lbias)


def kernel(x, c, ctx, c_ctx, w_mod, b_mod, w_in, b_in, lb_logits, norm_a_g, conv_w, conv_b,
           w_r, b_r, w_i, b_i, lam, p_a, p_b, w_out, ln_g, ln_b):
    batch, seq, d = x.shape
    n_ctx = ctx.shape[1]
    assert d == D_MODEL and seq % GRID_W == 0 and w_in.shape[0] == 1

    def select_cols(cols):
        w3 = jnp.stack([w_in[0][:, j * D_MODEL:(j + 1) * D_MODEL] for j in cols]).astype(BF16)
        b3 = jnp.stack([b_in[0][j * D_MODEL:(j + 1) * D_MODEL] for j in cols])[:, None, :]
        return w3, b3

    wg = (0.5 * jnp.concatenate([w_r[0, 0], w_i[0, 0], w_r[0, 1], w_i[0, 1]], axis=-1)).astype(BF16)
    c8 = jnp.zeros((8, D_MODEL), F32).at[:batch].set(c).at[batch].set(c_ctx)

    mod, lb = _mod_call(c8, w_mod[0], b_mod[0][None, :], lb_logits)
    mod3 = mod.reshape(8, 3, D_MODEL)

    x2 = x.reshape(batch * seq, D_MODEL)
    tm = 2048
    feat =_inproj_call(x2, mod3, *select_cols(LATENT_F32), out_dtype=F32, tm=tm, tiles_per_mod=seq // tm)
    feat_h = _inproj_call(x2, mod3, *select_cols(LATENT_BF16), out_dtype=BF16, tm=tm, tiles_per_mod=seq // tm)
    feat_c = _inproj_call(ctx.reshape(batch * n_ctx, D_MODEL), mod3[batch:batch + 1], *select_cols(CTX_F32),
                          out_dtype=F32, tm=n_ctx, tiles_per_mod=batch)

    cw, cb = conv_w[0], conv_b[0][None, :]
    s0f, s0b, h0f, h0b = _ctx_call(feat_c, lb, cw, cb, wg, b_r[0], b_i[0], lam[0], batch=batch, n_ctx=n_ctx)

    o_b = _gla_call(feat, feat_h, lb, F_F1, s0b, reverse=True, batch=batch, seq=seq, rows=16)
    oa = _gla_call(feat, feat_h, lb, F_F0, s0f, o_b, reverse=False, batch=batch, seq=seq, rows=16)
    hx = _lru_call(feat, cw, cb, wg, b_r[0], b_i[0], lam[0], h0f, h0b, batch=batch, seq=seq)

    out = _merge_call(oa, hx, feat_h, x2, mod3, norm_a_g[0][None, :],
                      p_a[0].astype(BF16), p_b[0].astype(BF16), w_out[0].astype(BF16),
                      ln_g[0][None, :], ln_b[0][None, :], tm=512, tiles_per_batch=seq // 512)
    return out.reshape(batch, seq, D_MODEL)
```

```python
import functools

import jax
import jax.numpy as jnp
from jax import lax
from jax.experimental import pallas as pl
from jax.experimental.pallas import tpu as pltpu

F32 = jnp.float32
BF16 = jnp.bfloat16

D_MODEL = 1024
GRID_W = 64
HEAD_DIM = 128
N_HEADS = D_MODEL // HEAD_DIM
N_BLOCKS = 8
BLOCK_DIM = D_MODEL // N_BLOCKS
N_CONV = 4
RG_C = 8.0
LN_EPS = 1e-5
RMS_EPS = 1e-6
DEEPNORM_ALPHA = 2.0 ** 0.25
Q_SCALE = HEAD_DIM ** -0.5

VMEM_LIMIT = 56 * 1024 * 1024

LATENT_F32 = (0, 1, 2, 5)
LATENT_BF16 = (3, 4, 6, 7, 8)
CTX_F32 = (1, 2, 3, 5)
F_Q, F_F0, F_F1, F_Z5 = 0, 1, 2, 3
H_V, H_G4, H_G6, H_M7, H_M8 = 0, 1, 2, 3, 4


def _sigmoid(x):
    return 0.5 * jnp.tanh(0.5 * x) + 0.5


def _silu(x):
    h = 0.5 * x
    return h * jnp.tanh(h) + h


def _log_forget(z, lb):
    return jnp.log((0.5 + 0.5 * lb) + (0.5 - 0.5 * lb) * jnp.tanh(0.5 * z))


def _dot(a, b):
    return jnp.dot(a, b, preferred_element_type=F32)


def _dot_nt(a, b):
    return lax.dot_general(a, b, (((1,), (1,)), ((), ())), preferred_element_type=F32)


def _dot_tn(a, b):
    return lax.dot_general(a, b, (((0,), (0,)), ((), ())), preferred_element_type=F32)


def _chunk_cumprod(x, reverse):
    n, c = x.shape
    nb = n // 8
    y = x.reshape(nb, 8, c)
    sub = lax.broadcasted_iota(jnp.int32, (nb, 8, c), 1)
    for s in (1, 2, 4):
        if reverse:
            y = y * jnp.where(sub < 8 - s, pltpu.roll(y, 8 - s, axis=1), 1.0)
        else:
            y = y * jnp.where(sub >= s, pltpu.roll(y, s, axis=1), 1.0)
    offs = [None] * nb
    acc = jnp.ones((1, c), F32)
    for b in (reversed(range(nb)) if reverse else range(nb)):
        offs[b] = acc
        acc = acc * (y[b, 0:1, :] if reverse else y[b, 7:8, :])
    return (y * jnp.stack(offs)).reshape(n, c), acc


def _cumsum_rows(mask_bf16, x):
    hi = x.astype(BF16)
    r1 = x - hi.astype(F32)
    mid = r1.astype(BF16)
    lo = (r1 - mid.astype(F32)).astype(BF16)
    return _dot(mask_bf16, hi) + _dot(mask_bf16, mid) + _dot(mask_bf16, lo)


def _mod_kernel(c_ref, w_ref, b_ref, lbl_ref, mod_ref, lb_ref):
    mod_ref[...] = _dot(_silu(c_ref[...]), w_ref[...]) + b_ref[...]
    l = lbl_ref[...]
    e = jnp.exp(l - jnp.max(l, axis=0, keepdims=True))
    lb_ref[...] = e[0] / jnp.sum(e, axis=0)


def _mod_call(c8, w_mod, b_mod, lb_logits):
    return pl.pallas_call(
        _mod_kernel,
        out_shape=(jax.ShapeDtypeStruct((8, 3 * D_MODEL), F32),
                   jax.ShapeDtypeStruct((2, D_MODEL), F32)),
        compiler_params=pltpu.CompilerParams(vmem_limit_bytes=VMEM_LIMIT),
        name="mod",
    )(c8, w_mod, b_mod, lb_logits)


def _inproj_kernel(x_ref, mod_ref, w_ref, b_ref, o_ref, u_ref):
    j = pl.program_id(1)

    @pl.when(j == 0)
    def _():
        u_ref[...] = (x_ref[...] * (1.0 + mod_ref[1:2, :]) + mod_ref[0:1, :]).astype(BF16)

    o_ref[...] = (_dot(u_ref[...], w_ref[j]) + b_ref[j]).astype(o_ref.dtype)


def _inproj_call(x2, mod3, w3, b3, *, out_dtype, tm, tiles_per_mod):
    n_tok = x2.shape[0]
    n_col = w3.shape[0]
    return pl.pallas_call(
        _inproj_kernel,
        grid=(n_tok // tm, n_col),
        in_specs=[
            pl.BlockSpec((tm, D_MODEL), lambda i, j: (i, 0)),
            pl.BlockSpec((None, 3, D_MODEL), lambda i, j: (i // tiles_per_mod, 0, 0)),
            pl.BlockSpec((n_col, D_MODEL, D_MODEL), lambda i, j: (0, 0, 0), pipeline_mode=pl.Buffered(1)),
            pl.BlockSpec((n_col, 1, D_MODEL), lambda i, j: (0, 0, 0)),
        ],
        out_specs=pl.BlockSpec((None, None, tm, D_MODEL), lambda i, j: (i, j, 0, 0)),
        out_shape=jax.ShapeDtypeStruct((n_tok // tm, n_col, tm, D_MODEL), out_dtype),
        scratch_shapes=[pltpu.VMEM((tm, D_MODEL), BF16)],
        compiler_params=pltpu.CompilerParams(
            dimension_semantics=("arbitrary", "arbitrary"), vmem_limit_bytes=VMEM_LIMIT),
        name="inproj",
    )(x2, mod3, w3, b3)


LOG2_E = 1.4426950408889634


def _lru_ab(xc, h_r, h_i, br, bi, sp):
    k = (-0.5 * RG_C * LOG2_E) * sp
    a = jnp.exp2(k * jnp.tanh(h_r + 0.5 * br) + k)
    i = 0.5 * jnp.tanh(h_i + 0.5 * bi) + 0.5
    y = 1.0 - a * a
    mult = jnp.where(y > 0.0, y * lax.rsqrt(y), 0.0)
    return a, mult * (i * xc)


def _softplus(y):
    return jnp.maximum(y, 0.0) + jnp.log(1.0 + jnp.exp(-jnp.abs(y)))


def _ctx_kernel(zf_ref, zb_ref, v_ref, z5_ref, lb_ref, cw_ref, cb_ref, wg_ref, br_ref, bi_ref, lam_ref,
                sf_ref, sb_ref, hf_ref, hb_ref, zp_ref, a_ref, b_ref):
    n = zf_ref.shape[0]
    ri = lax.broadcasted_iota(jnp.int32, (n, n), 0)
    ci = lax.broadcasted_iota(jnp.int32, (n, n), 1)
    tril = (ci <= ri).astype(F32).astype(BF16)
    v = v_ref[...].astype(BF16)

    lf = _log_forget(zf_ref[...], lb_ref[0:1, :])
    g = _cumsum_rows(tril, lf)
    ke = ((1.0 - jnp.exp(lf)) * jnp.exp(g[n - 1:n, :] - g)).astype(BF16)
    for h in range(N_HEADS):
        sl = slice(h * HEAD_DIM, (h + 1) * HEAD_DIM)
        sf_ref[h] = _dot_tn(v[:, sl], ke[:, sl])
    lf = _log_forget(zb_ref[...], lb_ref[1:2, :])
    g = _cumsum_rows(tril, lf)
    ke = ((1.0 - jnp.exp(lf)) * jnp.exp(g - lf)).astype(BF16)
    for h in range(N_HEADS):
        sl = slice(h * HEAD_DIM, (h + 1) * HEAD_DIM)
        sb_ref[h] = _dot_tn(v[:, sl], ke[:, sl])

    zp_ref[0:8, :] = jnp.zeros((8, D_MODEL), F32)
    zp_ref[8 + n:16 + n, :] = jnp.zeros((8, D_MODEL), F32)
    zp_ref[8:8 + n, :] = z5_ref[...]
    xc = cb_ref[...] + zp_ref[7:7 + n, :] * cw_ref[0:1, :]
    for kk in range(1, N_CONV):
        xc = xc + zp_ref[7 + kk:7 + kk + n, :] * cw_ref[kk:kk + 1, :]

    for d, h_ref in ((0, hf_ref), (1, hb_ref)):
        sp = _softplus(-lam_ref[d:d + 1, :])
        for blk in range(N_BLOCKS):
            sl = slice(blk * BLOCK_DIM, (blk + 1) * BLOCK_DIM)
            g = _dot(xc[:, sl].astype(BF16), wg_ref[blk, :, 2 * d * BLOCK_DIM:2 * (d + 1) * BLOCK_DIM])
            a, b = _lru_ab(xc[:, sl], g[:, :BLOCK_DIM], g[:, BLOCK_DIM:],
                           br_ref[d:d + 1, sl], bi_ref[d:d + 1, sl], sp[:, sl])
            a_ref[:, sl] = a
            b_ref[:, sl] = b

        def step(t, h, d=d):
            tt = (n - 1 - t) if d == 1 else t
            return a_ref[pl.ds(tt, 1), :] * h + b_ref[pl.ds(tt, 1), :]

        h_ref[...] = lax.fori_loop(0, n, step, jnp.zeros((1, D_MODEL), F32))


def _ctx_call(feat_c, lb, conv_w, conv_b, wg, br, bi, lam, *, batch, n_ctx):
    feat = lambda j: pl.BlockSpec((None, None, n_ctx, D_MODEL), lambda b, j=j: (b, j, 0, 0))
    full = lambda shape: pl.BlockSpec(shape, lambda b: (0,) * len(shape))
    state = pl.BlockSpec((None, N_HEADS, HEAD_DIM, HEAD_DIM), lambda b: (b, 0, 0, 0))
    hvec = pl.BlockSpec((None, 1, D_MODEL), lambda b: (b, 0, 0))
    return pl.pallas_call(
        _ctx_kernel,
        grid=(batch,),
        in_specs=[feat(0), feat(1), feat(2), feat(3), full((2, D_MODEL)),
                  full((N_CONV, D_MODEL)), full((1, D_MODEL)),
                  full((N_BLOCKS, BLOCK_DIM, 4 * BLOCK_DIM)),
                  full((2, D_MODEL)), full((2, D_MODEL)), full((2, D_MODEL))],
        out_specs=(state, state, hvec, hvec),
        out_shape=(jax.ShapeDtypeStruct((batch, N_HEADS, HEAD_DIM, HEAD_DIM), F32),
                   jax.ShapeDtypeStruct((batch, N_HEADS, HEAD_DIM, HEAD_DIM), F32),
                   jax.ShapeDtypeStruct((batch, 1, D_MODEL), F32),
                   jax.ShapeDtypeStruct((batch, 1, D_MODEL), F32)),
        scratch_shapes=[pltpu.VMEM((n_ctx + 16, D_MODEL), F32),
                        pltpu.VMEM((n_ctx, D_MODEL), F32),
                        pltpu.VMEM((n_ctx, D_MODEL), F32)],
        compiler_params=pltpu.CompilerParams(
            dimension_semantics=("arbitrary",), vmem_limit_bytes=VMEM_LIMIT),
        name="ctx_states",
    )(feat_c, feat_c, feat_c, feat_c, lb, conv_w, conv_b, wg, br, bi, lam)


GLA_ROWS_PER_ITER = 8


def _gla_kernel(*refs, reverse, rows, finish):
    if finish:
        (zq_ref, zf_ref, v_ref, lb_ref, s0_ref, other_ref, o_ref,
         st_ref, sn_ref, qd_ref, kit_ref, ke_ref, dec_ref) = refs
    else:
        zq_ref, zf_ref, v_ref, lb_ref, s0_ref, o_ref, st_ref, sn_ref, qd_ref, kit_ref, ke_ref, dec_ref = refs

    @pl.when(pl.program_id(1) == 0)
    def _():
        st_ref[...] = s0_ref[...]
        for h in range(N_HEADS):
            sn_ref[h] = s0_ref[h].astype(BF16).T

    ri = lax.broadcasted_iota(jnp.int32, (GRID_W, GRID_W), 0)
    ci = lax.broadcasted_iota(jnp.int32, (GRID_W, GRID_W), 1)
    allow = (ci >= ri) if reverse else (ci <= ri)
    lb_dir = 1 if reverse else 0

    def row_offset(n):
        n = jnp.minimum(n, rows - 1)
        return pl.multiple_of(((rows - 1 - n) if reverse else n) * GRID_W, GRID_W)

    def prepare(n, slot):
        off = row_offset(n)
        zq = zq_ref[pl.ds(off, GRID_W), :]
        hq = (0.5 * Q_SCALE) * zq
        q = hq * jnp.tanh(0.5 * zq) + hq
        lb = lb_ref[lb_dir:lb_dir + 1, :]
        f = (0.5 + 0.5 * lb) + (0.5 - 0.5 * lb) * jnp.tanh(0.5 * zf_ref[pl.ds(off, GRID_W), :])
        dg, dec = _chunk_cumprod(f, reverse)
        ki = (1.0 - f) / dg
        qd = (q * dg).astype(BF16)
        ke = (ki * dec).astype(BF16)
        kit_ref[slot] = ki.astype(BF16).T
        dec_ref[slot] = dec
        for h in range(N_HEADS):
            sl = slice(h * HEAD_DIM, (h + 1) * HEAD_DIM)
            qd_ref[slot, h] = qd[:, sl]
            ke_ref[slot, h] = ke[:, sl]

    def contract(n, slot):
        off = row_offset(n)
        for h in range(N_HEADS):
            sl = slice(h * HEAD_DIM, (h + 1) * HEAD_DIM)
            qd = qd_ref[slot, h]
            v = v_ref[pl.ds(off, GRID_W), sl]
            p = jnp.where(allow, _dot(qd, kit_ref[slot, sl, :]), 0.0).astype(BF16)
            o = _dot(p, v) + _dot(qd, sn_ref[h])
            st = st_ref[h] * dec_ref[slot, :, sl] + _dot_tn(v, ke_ref[slot, h])
            st_ref[h] = st
            sn_ref[h] = st.astype(BF16).T
            if finish:
                o = o + other_ref[pl.ds(off, GRID_W), sl]
            o_ref[pl.ds(off, GRID_W), sl] = o.astype(o_ref.dtype)

    prepare(0, 0)

    def body(m, carry):
        n = GLA_ROWS_PER_ITER * m
        for i in range(GLA_ROWS_PER_ITER):
            contract(n + i, i % 2)
            prepare(n + i + 1, (i + 1) % 2)
        return carry

    lax.fori_loop(0, rows // GLA_ROWS_PER_ITER, body, 0)


def _gla_call(feat, feat_h, lb, f_col, s0, other=None, *, reverse, batch, seq, rows):
    tb = rows * GRID_W
    nrb = seq // tb
    finish = other is not None
    per_tile = feat.shape[2] // tb

    def row_block(b, i):
        return b * nrb + ((nrb - 1 - i) if reverse else i)

    def col(c):
        return pl.BlockSpec((None, None, tb, D_MODEL), lambda b, i, c=c: (
            row_block(b, i) // per_tile, c, row_block(b, i) % per_tile, 0))

    tok = pl.BlockSpec((tb, D_MODEL), lambda b, i: (row_block(b, i), 0))
    in_specs = [col(F_Q), col(f_col), col(H_V),
                pl.BlockSpec((2, D_MODEL), lambda b, i: (0, 0)),
                pl.BlockSpec((None, N_HEADS, HEAD_DIM, HEAD_DIM), lambda b, i: (b, 0, 0, 0))]
    args = [feat, feat, feat_h, lb, s0]
    if finish:
        in_specs += [tok]
        args += [other]
    return pl.pallas_call(
        functools.partial(_gla_kernel, reverse=reverse, rows=rows, finish=finish),
        grid=(batch, nrb),
        in_specs=in_specs,
        out_specs=tok,
        out_shape=jax.ShapeDtypeStruct((batch * seq, D_MODEL), BF16 if finish else F32),
        scratch_shapes=[pltpu.VMEM((N_HEADS, HEAD_DIM, HEAD_DIM), F32)]
        + [pltpu.VMEM((N_HEADS, HEAD_DIM, HEAD_DIM), BF16), pltpu.VMEM((2, N_HEADS, GRID_W, HEAD_DIM), BF16),
           pltpu.VMEM((2, D_MODEL, GRID_W), BF16), pltpu.VMEM((2, N_HEADS, GRID_W, HEAD_DIM), BF16),
           pltpu.VMEM((2, 1, D_MODEL), F32)],
        compiler_params=pltpu.CompilerParams(
            dimension_semantics=("arbitrary", "arbitrary"), vmem_limit_bytes=VMEM_LIMIT),
        name="gla_bwd" if reverse else "gla_fwd",
    )(*args)


def _lru_kernel(z5_ref, cw_ref, cb_ref, wg_ref, br_ref, bi_ref, lam_ref, h0f_ref, h0b_ref, o_ref,
                zp_ref, af_ref, bf_ref, ab_ref, bb_ref, cf_ref, cr_ref, *, n_rows, rows_per_step):
    w = GRID_W
    seq = n_rows * w
    tb = rows_per_step * w
    n_steps = n_rows // rows_per_step

    zp_ref[0:w, :] = jnp.zeros((w, BLOCK_DIM), F32)
    zp_ref[w + seq:, :] = jnp.zeros((2 * w, BLOCK_DIM), F32)
    tile = z5_ref.shape[1]
    for k in range(z5_ref.shape[0]):
        zp_ref[w + k * tile:w + (k + 1) * tile, :] = z5_ref[k]

    sp = [_softplus(-lam_ref[d:d + 1, :]) for d in (0, 1)]

    def gates(s, c):
        off = pl.multiple_of(s * tb, tb)
        xc = cb_ref[...] + zp_ref[pl.ds(off, tb), :] * cw_ref[0:1, :]
        for kk in range(1, N_CONV):
            xc = xc + zp_ref[pl.ds(off + kk * w, tb), :] * cw_ref[kk:kk + 1, :]
        g = _dot(xc.astype(BF16), wg_ref[...])
        for d, a_ref, b_ref in ((0, af_ref, bf_ref), (1, ab_ref, bb_ref)):
            c0 = 2 * d * BLOCK_DIM
            a, b = _lru_ab(xc, g[:, c0:c0 + BLOCK_DIM], g[:, c0 + BLOCK_DIM:c0 + 2 * BLOCK_DIM],
                           br_ref[d:d + 1, :], bi_ref[d:d + 1, :], sp[d])
            a_ref[pl.ds(off, tb), :] = a
            b_ref[pl.ds(off, tb), :] = b
        return c

    lax.fori_loop(0, n_steps, gates, 0)

    def scan(n, carry):
        hf, pf, hb, pb = carry
        off_f = pl.multiple_of(n * w, w)
        off_b = pl.multiple_of((n_rows - 1 - n) * w, w)
        a = af_ref[pl.ds(off_f, w), :]
        hf = a * hf + bf_ref[pl.ds(off_f, w), :]
        pf = pf * a
        bf_ref[pl.ds(off_f, w), :] = hf
        af_ref[pl.ds(off_f, w), :] = pf
        a = ab_ref[pl.ds(off_b, w), :]
        hb = a * hb + bb_ref[pl.ds(off_b, w), :]
        pb = pb * a
        bb_ref[pl.ds(off_b, w), :] = hb
        ab_ref[pl.ds(off_b, w), :] = pb
        return hf, pf, hb, pb

    zeros = jnp.zeros((w, BLOCK_DIM), F32)
    ones = jnp.ones((w, BLOCK_DIM), F32)
    lax.fori_loop(0, n_rows, scan, (zeros, ones, zeros, ones))

    last = (n_rows - 1) * w

    def carry(n, c):
        cf, cb = c
        cf_ref[pl.ds(n, 1), :] = cf
        cf = bf_ref[pl.ds(last + n, 1), :] + af_ref[pl.ds(last + n, 1), :] * cf
        col = w - 1 - n
        cr_ref[pl.ds(col, 1), :] = cb
        cb = bb_ref[pl.ds(col, 1), :] + ab_ref[pl.ds(col, 1), :] * cb
        return cf, cb

    lax.fori_loop(0, w, carry, (h0f_ref[...], h0b_ref[...]))

    def fix(s, c):
        off = pl.multiple_of(s * tb, tb)
        cf = jnp.concatenate([cf_ref[...]] * rows_per_step, axis=0)
        cr = jnp.concatenate([cr_ref[...]] * rows_per_step, axis=0)
        h = (bf_ref[pl.ds(off, tb), :] + af_ref[pl.ds(off, tb), :] * cf
             + bb_ref[pl.ds(off, tb), :] + ab_ref[pl.ds(off, tb), :] * cr)
        o_ref[pl.ds(off, tb), :] = h.astype(o_ref.dtype)
        return c

    lax.fori_loop(0, n_steps, fix, 0)


def _lru_call(feat, conv_w, conv_b, wg, br, bi, lam, h0f, h0b, *, batch, seq):
    n_rows = seq // GRID_W
    vec = lambda n: pl.BlockSpec((n, BLOCK_DIM), lambda b, c: (0, c))
    h0 = pl.BlockSpec((None, 1, BLOCK_DIM), lambda b, c: (b, 0, c))
    buf = pltpu.VMEM((seq, BLOCK_DIM), F32)
    n_i, n_col, tm, _ = feat.shape
    feat5 = feat.reshape(batch, n_i // batch, n_col, tm, D_MODEL)
    return pl.pallas_call(
        functools.partial(_lru_kernel, n_rows=n_rows, rows_per_step=8),
        grid=(batch, N_BLOCKS),
        in_specs=[pl.BlockSpec((None, n_i // batch, None, tm, BLOCK_DIM), lambda b, c: (b, 0, F_Z5, 0, c)),
                  vec(N_CONV), vec(1),
                  pl.BlockSpec((None, BLOCK_DIM, 4 * BLOCK_DIM), lambda b, c: (c, 0, 0)),
                  vec(2), vec(2), vec(2), h0, h0],
        out_specs=pl.BlockSpec((seq, BLOCK_DIM), lambda b, c: (b, c)),
        out_shape=jax.ShapeDtypeStruct((batch * seq, D_MODEL), BF16),
        scratch_shapes=[pltpu.VMEM((seq + 3 * GRID_W, BLOCK_DIM), F32), buf, buf, buf, buf,
                        pltpu.VMEM((GRID_W, BLOCK_DIM), F32), pltpu.VMEM((GRID_W, BLOCK_DIM), F32)],
        compiler_params=pltpu.CompilerParams(
            dimension_semantics=("arbitrary", "arbitrary"), vmem_limit_bytes=VMEM_LIMIT),
        name="lru",
    )(feat5, conv_w, conv_b, wg, br, bi, lam, h0f, h0b)


def _merge_kernel(oa_ref, hx_ref, z4_ref, z6_ref, z7_ref, z8_ref, x_ref, mod_ref, ng_ref,
                  pa_ref, pb_ref, wo_ref, lg_ref, lbias_ref, o_ref):
    f32 = lambda ref: ref[...].astype(F32)
    o_b = (f32(hx_ref) * _silu(f32(z6_ref))).astype(BF16)
    y = _sigmoid(f32(z8_ref)) * _dot(o_b, pb_ref[...])
    y_a = None
    for pair in range(N_HEADS // 2):
        o_h = []
        for h in (2 * pair, 2 * pair + 1):
            sl = slice(h * HEAD_DIM, (h + 1) * HEAD_DIM)
            t = oa_ref[:, sl].astype(F32)
            ms = jnp.mean(t * t, axis=-1, keepdims=True)
            o_h.append((t * lax.rsqrt(ms + RMS_EPS) * ng_ref[...] * _silu(z4_ref[:, sl].astype(F32))).astype(BF16))
        rows = slice(2 * pair * HEAD_DIM, 2 * (pair + 1) * HEAD_DIM)
        part = _dot(jnp.concatenate(o_h, axis=-1), pa_ref[rows, :])
        y_a = part if y_a is None else y_a + part
    y = y + _sigmoid(f32(z7_ref)) * y_a
    y = _dot(y.astype(BF16), wo_ref[...])
    t = DEEPNORM_ALPHA * x_ref[...] + mod_ref[2:3, :] * y
    mu = jnp.mean(t, axis=-1, keepdims=True)
    tc = t - mu
    var = jnp.mean(tc * tc, axis=-1, keepdims=True)
    o_ref[...] = tc * lax.rsqrt(var + LN_EPS) * lg_ref[...] + lbias_ref[...]


def _merge_call(oa, hx, feat_h, x2, mod3, ng, pa, pb, wo, lg, lbias, *, tm, tiles_per_batch):
    n_tok = x2.shape[0]
    tok = pl.BlockSpec((tm, D_MODEL), lambda i: (i, 0))
    per_tile = feat_h.shape[2] // tm
    col = lambda j: pl.BlockSpec((None, None, tm, D_MODEL), lambda i, j=j: (i // per_tile, j, i % per_tile, 0))
    full = lambda shape: pl.BlockSpec(shape, lambda i: (0,) * len(shape), pipeline_mode=pl.Buffered(1))
    return pl.pallas_call(
        _merge_kernel,
        grid=(n_tok // tm,),
        in_specs=[tok, tok, col(H_G4), col(H_G6), col(H_M7), col(H_M8), tok,
                  pl.BlockSpec((None, 3, D_MODEL), lambda i: (i // tiles_per_batch, 0, 0)),
                  full((1, HEAD_DIM)), full((D_MODEL, D_MODEL)), full((D_MODEL, D_MODEL)),
                  full((D_MODEL, D_MODEL)), full((1, D_MODEL)), full((1, D_MODEL))],
        out_specs=tok,
        out_shape=jax.ShapeDtypeStruct((n_tok, D_MODEL), F32),
        compiler_params=pltpu.CompilerParams(
            dimension_semantics=("arbitrary",), vmem_limit_bytes=VMEM_LIMIT),
        name="merge",
    )(oa, hx, feat_h, feat_h, feat_h, feat_h, x2, mod3, ng, pa, pb, wo, lg, lbias)


def kernel(x, c, ctx, c_ctx, w_mod, b_mod, w_in, b_in, lb_logits, norm_a_g, conv_w, conv_b,
           w_r, b_r, w_i, b_i, lam, p_a, p_b, w_out, ln_g, ln_b):
    batch, seq, d = x.shape
    n_ctx = ctx.shape[1]
    assert d == D_MODEL and seq % GRID_W == 0 and w_in.shape[0] == 1

    def select_cols(cols):
        w3 = jnp.stack([w_in[0][:, j * D_MODEL:(j + 1) * D_MODEL] for j in cols]).astype(BF16)
        b3 = jnp.stack([b_in[0][j * D_MODEL:(j + 1) * D_MODEL] for j in cols])[:, None, :]
        return w3, b3

    wg = (0.5 * jnp.concatenate([w_r[0, 0], w_i[0, 0], w_r[0, 1], w_i[0, 1]], axis=-1)).astype(BF16)
    c8 = jnp.zeros((8, D_MODEL), F32).at[:batch].set(c).at[batch].set(c_ctx)

    mod, lb = _mod_call(c8, w_mod[0], b_mod[0][None, :], lb_logits)
    mod3 = mod.reshape(8, 3, D_MODEL)

    x2 = x.reshape(batch * seq, D_MODEL)
    tm = 2048
    feat =_inproj_call(x2, mod3, *select_cols(LATENT_F32), out_dtype=F32, tm=tm, tiles_per_mod=seq // tm)
    feat_h = _inproj_call(x2, mod3, *select_cols(LATENT_BF16), out_dtype=BF16, tm=tm, tiles_per_mod=seq // tm)
    feat_c = _inproj_call(ctx.reshape(batch * n_ctx, D_MODEL), mod3[batch:batch + 1], *select_cols(CTX_F32),
                          out_dtype=F32, tm=n_ctx, tiles_per_mod=batch)

    cw, cb = conv_w[0], conv_b[0][None, :]
    s0f, s0b, h0f, h0b = _ctx_call(feat_c, lb, cw, cb, wg, b_r[0], b_i[0], lam[0], batch=batch, n_ctx=n_ctx)

    o_b = _gla_call(feat, feat_h, lb, F_F1, s0b, reverse=True, batch=batch, seq=seq, rows=16)
    oa = _gla_call(feat, feat_h, lb, F_F0, s0f, o_b, reverse=False, batch=batch, seq=seq, rows=16)
    hx = _lru_call(feat, cw, cb, wg, b_r[0], b_i[0], lam[0], h0f, h0b, batch=batch, seq=seq)

    out = _merge_call(oa, hx, feat_h, x2, mod3, norm_a_g[0][None, :],
                      p_a[0].astype(BF16), p_b[0].astype(BF16), w_out[0].astype(BF16),
                      ln_g[0][None, :], ln_b[0][None, :], tm=512, tiles_per_batch=seq // 512)
    return out.reshape(batch, seq, D_MODEL)
```

```python
import functools

import jax
import jax.numpy as jnp
from jax import lax
from jax.experimental import pallas as pl
from jax.experimental.pallas import tpu as pltpu

F32 = jnp.float32
BF16 = jnp.bfloat16

D_MODEL = 1024
GRID_W = 64
HEAD_DIM = 128
N_HEADS = D_MODEL // HEAD_DIM
N_BLOCKS = 8
BLOCK_DIM = D_MODEL // N_BLOCKS
N_CONV = 4
RG_C = 8.0
LN_EPS = 1e-5
RMS_EPS = 1e-6
DEEPNORM_ALPHA = 2.0 ** 0.25
Q_SCALE = HEAD_DIM ** -0.5

VMEM_LIMIT = 56 * 1024 * 1024

LATENT_F32 = (0, 1, 2, 5)
LATENT_BF16 = (3, 4, 6, 7, 8)
CTX_F32 = (1, 2, 3, 5)
F_Q, F_F0, F_F1, F_Z5 = 0, 1, 2, 3
H_V, H_G4, H_G6, H_M7, H_M8 = 0, 1, 2, 3, 4


def _sigmoid(x):
    return 0.5 * jnp.tanh(0.5 * x) + 0.5


def _silu(x):
    h = 0.5 * x
    return h * jnp.tanh(h) + h


def _log_forget(z, lb):
    return jnp.log((0.5 + 0.5 * lb) + (0.5 - 0.5 * lb) * jnp.tanh(0.5 * z))


def _dot(a, b):
    return jnp.dot(a, b, preferred_element_type=F32)


def _dot_nt(a, b):
    return lax.dot_general(a, b, (((1,), (1,)), ((), ())), preferred_element_type=F32)


def _dot_tn(a, b):
    return lax.dot_general(a, b, (((0,), (0,)), ((), ())), preferred_element_type=F32)


def _chunk_cumprod(x, reverse):
    n, c = x.shape
    nb = n // 8
    y = x.reshape(nb, 8, c)
    sub = lax.broadcasted_iota(jnp.int32, (nb, 8, c), 1)
    for s in (1, 2, 4):
        if reverse:
            y = y * jnp.where(sub < 8 - s, pltpu.roll(y, 8 - s, axis=1), 1.0)
        else:
            y = y * jnp.where(sub >= s, pltpu.roll(y, s, axis=1), 1.0)
    offs = [None] * nb
    acc = jnp.ones((1, c), F32)
    for b in (reversed(range(nb)) if reverse else range(nb)):
        offs[b] = acc
        acc = acc * (y[b, 0:1, :] if reverse else y[b, 7:8, :])
    return (y * jnp.stack(offs)).reshape(n, c), acc


def _cumsum_rows(mask_bf16, x):
    hi = x.astype(BF16)
    r1 = x - hi.astype(F32)
    mid = r1.astype(BF16)
    lo = (r1 - mid.astype(F32)).astype(BF16)
    return _dot(mask_bf16, hi) + _dot(mask_bf16, mid) + _dot(mask_bf16, lo)


def _mod_kernel(c_ref, w_ref, b_ref, lbl_ref, mod_ref, lb_ref):
    mod_ref[...] = _dot(_silu(c_ref[...]), w_ref[...]) + b_ref[...]
    l = lbl_ref[...]
    e = jnp.exp(l - jnp.max(l, axis=0, keepdims=True))
    lb_ref[...] = e[0] / jnp.sum(e, axis=0)


def _mod_call(c8, w_mod, b_mod, lb_logits):
    return pl.pallas_call(
        _mod_kernel,
        out_shape=(jax.ShapeDtypeStruct((8, 3 * D_MODEL), F32),
                   jax.ShapeDtypeStruct((2, D_MODEL), F32)),
        compiler_params=pltpu.CompilerParams(vmem_limit_bytes=VMEM_LIMIT),
        name="mod",
    )(c8, w_mod, b_mod, lb_logits)


def _inproj_kernel(x_ref, mod_ref, w_ref, b_ref, o_ref):
    u = (x_ref[...] * (1.0 + mod_ref[1:2, :]) + mod_ref[0:1, :]).astype(BF16)
    for j in range(w_ref.shape[0]):
        o_ref[j] = (_dot(u, w_ref[j]) + b_ref[j]).astype(o_ref.dtype)


def _inproj_call(x2, mod3, w3, b3, *, out_dtype, tm, tiles_per_mod):
    n_tok = x2.shape[0]
    n_col = w3.shape[0]
    const = lambda shape: pl.BlockSpec(shape, lambda i: (0,) * len(shape), pipeline_mode=pl.Buffered(1))
    return pl.pallas_call(
        _inproj_kernel,
        grid=(n_tok // tm,),
        in_specs=[
            pl.BlockSpec((tm, D_MODEL), lambda i: (i, 0)),
            pl.BlockSpec((None, 3, D_MODEL), lambda i: (i // tiles_per_mod, 0, 0)),
            const((n_col, D_MODEL, D_MODEL)),
            const((n_col, 1, D_MODEL)),
        ],
        out_specs=pl.BlockSpec((None, n_col, tm, D_MODEL), lambda i: (i, 0, 0, 0)),
        out_shape=jax.ShapeDtypeStruct((n_tok // tm, n_col, tm, D_MODEL), out_dtype),
        compiler_params=pltpu.CompilerParams(
            dimension_semantics=("arbitrary",), vmem_limit_bytes=VMEM_LIMIT),
        name="inproj",
    )(x2, mod3, w3, b3)


LOG2_E = 1.4426950408889634


def _lru_ab(xc, h_r, h_i, br, bi, sp):
    k = (-0.5 * RG_C * LOG2_E) * sp
    a = jnp.exp2(k * jnp.tanh(h_r + 0.5 * br) + k)
    i = 0.5 * jnp.tanh(h_i + 0.5 * bi) + 0.5
    y = 1.0 - a * a
    mult = jnp.where(y > 0.0, y * lax.rsqrt(y), 0.0)
    return a, mult * (i * xc)


def _softplus(y):
    return jnp.maximum(y, 0.0) + jnp.log(1.0 + jnp.exp(-jnp.abs(y)))


def _ctx_kernel(zf_ref, zb_ref, v_ref, z5_ref, lb_ref, cw_ref, cb_ref, wg_ref, br_ref, bi_ref, lam_ref,
                sf_ref, sb_ref, hf_ref, hb_ref, zp_ref, a_ref, b_ref):
    n = zf_ref.shape[0]
    ri = lax.broadcasted_iota(jnp.int32, (n, n), 0)
    ci = lax.broadcasted_iota(jnp.int32, (n, n), 1)
    tril = (ci <= ri).astype(F32).astype(BF16)
    v = v_ref[...].astype(BF16)

    lf = _log_forget(zf_ref[...], lb_ref[0:1, :])
    g = _cumsum_rows(tril, lf)
    ke = ((1.0 - jnp.exp(lf)) * jnp.exp(g[n - 1:n, :] - g)).astype(BF16)
    for h in range(N_HEADS):
        sl = slice(h * HEAD_DIM, (h + 1) * HEAD_DIM)
        sf_ref[h] = _dot_tn(v[:, sl], ke[:, sl])
    lf = _log_forget(zb_ref[...], lb_ref[1:2, :])
    g = _cumsum_rows(tril, lf)
    ke = ((1.0 - jnp.exp(lf)) * jnp.exp(g - lf)).astype(BF16)
    for h in range(N_HEADS):
        sl = slice(h * HEAD_DIM, (h + 1) * HEAD_DIM)
        sb_ref[h] = _dot_tn(v[:, sl], ke[:, sl])

    zp_ref[0:8, :] = jnp.zeros((8, D_MODEL), F32)
    zp_ref[8 + n:16 + n, :] = jnp.zeros((8, D_MODEL), F32)
    zp_ref[8:8 + n, :] = z5_ref[...]
    xc = cb_ref[...] + zp_ref[7:7 + n, :] * cw_ref[0:1, :]
    for kk in range(1, N_CONV):
        xc = xc + zp_ref[7 + kk:7 + kk + n, :] * cw_ref[kk:kk + 1, :]

    for d, h_ref in ((0, hf_ref), (1, hb_ref)):
        sp = _softplus(-lam_ref[d:d + 1, :])
        for blk in range(N_BLOCKS):
            sl = slice(blk * BLOCK_DIM, (blk + 1) * BLOCK_DIM)
            g = _dot(xc[:, sl].astype(BF16), wg_ref[blk, :, 2 * d * BLOCK_DIM:2 * (d + 1) * BLOCK_DIM])
            a, b = _lru_ab(xc[:, sl], g[:, :BLOCK_DIM], g[:, BLOCK_DIM:],
                           br_ref[d:d + 1, sl], bi_ref[d:d + 1, sl], sp[:, sl])
            a_ref[:, sl] = a
            b_ref[:, sl] = b

        def step(t, h, d=d):
            tt = (n - 1 - t) if d == 1 else t
            return a_ref[pl.ds(tt, 1), :] * h + b_ref[pl.ds(tt, 1), :]

        h_ref[...] = lax.fori_loop(0, n, step, jnp.zeros((1, D_MODEL), F32))


def _ctx_call(feat_c, lb, conv_w, conv_b, wg, br, bi, lam, *, batch, n_ctx):
    feat = lambda j: pl.BlockSpec((None, None, n_ctx, D_MODEL), lambda b, j=j: (b, j, 0, 0))
    full = lambda shape: pl.BlockSpec(shape, lambda b: (0,) * len(shape))
    state = pl.BlockSpec((None, N_HEADS, HEAD_DIM, HEAD_DIM), lambda b: (b, 0, 0, 0))
    hvec = pl.BlockSpec((None, 1, D_MODEL), lambda b: (b, 0, 0))
    return pl.pallas_call(
        _ctx_kernel,
        grid=(batch,),
        in_specs=[feat(0), feat(1), feat(2), feat(3), full((2, D_MODEL)),
                  full((N_CONV, D_MODEL)), full((1, D_MODEL)),
                  full((N_BLOCKS, BLOCK_DIM, 4 * BLOCK_DIM)),
                  full((2, D_MODEL)), full((2, D_MODEL)), full((2, D_MODEL))],
        out_specs=(state, state, hvec, hvec),
        out_shape=(jax.ShapeDtypeStruct((batch, N_HEADS, HEAD_DIM, HEAD_DIM), F32),
                   jax.ShapeDtypeStruct((batch, N_HEADS, HEAD_DIM, HEAD_DIM), F32),
                   jax.ShapeDtypeStruct((batch, 1, D_MODEL), F32),
                   jax.ShapeDtypeStruct((batch, 1, D_MODEL), F32)),
        scratch_shapes=[pltpu.VMEM((n_ctx + 16, D_MODEL), F32),
                        pltpu.VMEM((n_ctx, D_MODEL), F32),
                        pltpu.VMEM((n_ctx, D_MODEL), F32)],
        compiler_params=pltpu.CompilerParams(
            dimension_semantics=("arbitrary",), vmem_limit_bytes=VMEM_LIMIT),
        name="ctx_states",
    )(feat_c, feat_c, feat_c, feat_c, lb, conv_w, conv_b, wg, br, bi, lam)


GLA_ROWS_PER_ITER = 8


def _gla_kernel(*refs, reverse, rows, finish):
    if finish:
        (zq_ref, zf_ref, v_ref, lb_ref, s0_ref, other_ref, o_ref,
         st_ref, sn_ref, qd_ref, kit_ref, ke_ref, dec_ref) = refs
    else:
        zq_ref, zf_ref, v_ref, lb_ref, s0_ref, o_ref, st_ref, sn_ref, qd_ref, kit_ref, ke_ref, dec_ref = refs

    @pl.when(pl.program_id(1) == 0)
    def _():
        st_ref[...] = s0_ref[...]
        for h in range(N_HEADS):
            sn_ref[h] = s0_ref[h].astype(BF16).T

    ri = lax.broadcasted_iota(jnp.int32, (GRID_W, GRID_W), 0)
    ci = lax.broadcasted_iota(jnp.int32, (GRID_W, GRID_W), 1)
    allow = (ci >= ri) if reverse else (ci <= ri)
    lb_dir = 1 if reverse else 0

    def row_offset(n):
        n = jnp.minimum(n, rows - 1)
        return pl.multiple_of(((rows - 1 - n) if reverse else n) * GRID_W, GRID_W)

    def prepare(n, slot):
        off = row_offset(n)
        zq = zq_ref[pl.ds(off, GRID_W), :]
        hq = (0.5 * Q_SCALE) * zq
        q = hq * jnp.tanh(0.5 * zq) + hq
        lb = lb_ref[lb_dir:lb_dir + 1, :]
        f = (0.5 + 0.5 * lb) + (0.5 - 0.5 * lb) * jnp.tanh(0.5 * zf_ref[pl.ds(off, GRID_W), :])
        dg, dec = _chunk_cumprod(f, reverse)
        ki = (1.0 - f) / dg
        qd = (q * dg).astype(BF16)
        ke = (ki * dec).astype(BF16)
        kit_ref[slot] = ki.astype(BF16).T
        dec_ref[slot] = dec
        for h in range(N_HEADS):
            sl = slice(h * HEAD_DIM, (h + 1) * HEAD_DIM)
            qd_ref[slot, h] = qd[:, sl]
            ke_ref[slot, h] = ke[:, sl]

    def contract(n, slot):
        off = row_offset(n)
        for h in range(N_HEADS):
            sl = slice(h * HEAD_DIM, (h + 1) * HEAD_DIM)
            qd = qd_ref[slot, h]
            v = v_ref[pl.ds(off, GRID_W), sl]
            p = jnp.where(allow, _dot(qd, kit_ref[slot, sl, :]), 0.0).astype(BF16)
            o = _dot(p, v) + _dot(qd, sn_ref[h])
            st = st_ref[h] * dec_ref[slot, :, sl] + _dot_tn(v, ke_ref[slot, h])
            st_ref[h] = st
            sn_ref[h] = st.astype(BF16).T
            if finish:
                o = o + other_ref[pl.ds(off, GRID_W), sl]
            o_ref[pl.ds(off, GRID_W), sl] = o.astype(o_ref.dtype)

    prepare(0, 0)

    def body(m, carry):
        n = GLA_ROWS_PER_ITER * m
        for i in range(GLA_ROWS_PER_ITER):
            contract(n + i, i % 2)
            prepare(n + i + 1, (i + 1) % 2)
        return carry

    lax.fori_loop(0, rows // GLA_ROWS_PER_ITER, body, 0)


def _gla_call(feat, feat_h, lb, f_col, s0, other=None, *, reverse, batch, seq, rows):
    tb = rows * GRID_W
    nrb = seq // tb
    finish = other is not None
    per_tile = feat.shape[2] // tb

    def row_block(b, i):
        return b * nrb + ((nrb - 1 - i) if reverse else i)

    def col(c):
        return pl.BlockSpec((None, None, tb, D_MODEL), lambda b, i, c=c: (
            row_block(b, i) // per_tile, c, row_block(b, i) % per_tile, 0))

    tok = pl.BlockSpec((tb, D_MODEL), lambda b, i: (row_block(b, i), 0))
    in_specs = [col(F_Q), col(f_col), col(H_V),
                pl.BlockSpec((2, D_MODEL), lambda b, i: (0, 0)),
                pl.BlockSpec((None, N_HEADS, HEAD_DIM, HEAD_DIM), lambda b, i: (b, 0, 0, 0))]
    args = [feat, feat, feat_h, lb, s0]
    if finish:
        in_specs += [tok]
        args += [other]
    return pl.pallas_call(
        functools.partial(_gla_kernel, reverse=reverse, rows=rows, finish=finish),
        grid=(batch, nrb),
        in_specs=in_specs,
        out_specs=tok,
        out_shape=jax.ShapeDtypeStruct((batch * seq, D_MODEL), BF16 if finish else F32),
        scratch_shapes=[pltpu.VMEM((N_HEADS, HEAD_DIM, HEAD_DIM), F32)]
        + [pltpu.VMEM((N_HEADS, HEAD_DIM, HEAD_DIM), BF16), pltpu.VMEM((2, N_HEADS, GRID_W, HEAD_DIM), BF16),
           pltpu.VMEM((2, D_MODEL, GRID_W), BF16), pltpu.VMEM((2, N_HEADS, GRID_W, HEAD_DIM), BF16),
           pltpu.VMEM((2, 1, D_MODEL), F32)],
        compiler_params=pltpu.CompilerParams(
            dimension_semantics=("arbitrary", "arbitrary"), vmem_limit_bytes=VMEM_LIMIT),
        name="gla_bwd" if reverse else "gla_fwd",
    )(*args)


def _lru_kernel(z5_ref, cw_ref, cb_ref, wg_ref, br_ref, bi_ref, lam_ref, h0f_ref, h0b_ref, o_ref,
                zp_ref, af_ref, bf_ref, ab_ref, bb_ref, cf_ref, cr_ref, *, n_rows, rows_per_step):
    w = GRID_W
    seq = n_rows * w
    tb = rows_per_step * w
    n_steps = n_rows // rows_per_step

    zp_ref[0:w, :] = jnp.zeros((w, BLOCK_DIM), F32)
    zp_ref[w + seq:, :] = jnp.zeros((2 * w, BLOCK_DIM), F32)
    tile = z5_ref.shape[1]
    for k in range(z5_ref.shape[0]):
        zp_ref[w + k * tile:w + (k + 1) * tile, :] = z5_ref[k]

    sp = [_softplus(-lam_ref[d:d + 1, :]) for d in (0, 1)]

    def gates(s, c):
        off = pl.multiple_of(s * tb, tb)
        xc = cb_ref[...] + zp_ref[pl.ds(off, tb), :] * cw_ref[0:1, :]
        for kk in range(1, N_CONV):
            xc = xc + zp_ref[pl.ds(off + kk * w, tb), :] * cw_ref[kk:kk + 1, :]
        g = _dot(xc.astype(BF16), wg_ref[...])
        for d, a_ref, b_ref in ((0, af_ref, bf_ref), (1, ab_ref, bb_ref)):
            c0 = 2 * d * BLOCK_DIM
            a, b = _lru_ab(xc, g[:, c0:c0 + BLOCK_DIM], g[:, c0 + BLOCK_DIM:c0 + 2 * BLOCK_DIM],
                           br_ref[d:d + 1, :], bi_ref[d:d + 1, :], sp[d])
            a_ref[pl.ds(off, tb), :] = a
            b_ref[pl.ds(off, tb), :] = b
        return c

    lax.fori_loop(0, n_steps, gates, 0)

    def scan(n, carry):
        hf, pf, hb, pb = carry
        off_f = pl.multiple_of(n * w, w)
        off_b = pl.multiple_of((n_rows - 1 - n) * w, w)
        a = af_ref[pl.ds(off_f, w), :]
        hf = a * hf + bf_ref[pl.ds(off_f, w), :]
        pf = pf * a
        bf_ref[pl.ds(off_f, w), :] = hf
        af_ref[pl.ds(off_f, w), :] = pf
        a = ab_ref[pl.ds(off_b, w), :]
        hb = a * hb + bb_ref[pl.ds(off_b, w), :]
        pb = pb * a
        bb_ref[pl.ds(off_b, w), :] = hb
        ab_ref[pl.ds(off_b, w), :] = pb
        return hf, pf, hb, pb

    zeros = jnp.zeros((w, BLOCK_DIM), F32)
    ones = jnp.ones((w, BLOCK_DIM), F32)
    lax.fori_loop(0, n_rows, scan, (zeros, ones, zeros, ones))

    last = (n_rows - 1) * w

    def carry(n, c):
        cf, cb = c
        cf_ref[pl.ds(n, 1), :] = cf
        cf = bf_ref[pl.ds(last + n, 1), :] + af_ref[pl.ds(last + n, 1), :] * cf
        col = w - 1 - n
        cr_ref[pl.ds(col, 1), :] = cb
        cb = bb_ref[pl.ds(col, 1), :] + ab_ref[pl.ds(col, 1), :] * cb
        return cf, cb

    lax.fori_loop(0, w, carry, (h0f_ref[...], h0b_ref[...]))

    def fix(s, c):
        off = pl.multiple_of(s * tb, tb)
        cf = jnp.concatenate([cf_ref[...]] * rows_per_step, axis=0)
        cr = jnp.concatenate([cr_ref[...]] * rows_per_step, axis=0)
        h = (bf_ref[pl.ds(off, tb), :] + af_ref[pl.ds(off, tb), :] * cf
             + bb_ref[pl.ds(off, tb), :] + ab_ref[pl.ds(off, tb), :] * cr)
        o_ref[pl.ds(off, tb), :] = h.astype(o_ref.dtype)
        return c

    lax.fori_loop(0, n_steps, fix, 0)


def _lru_call(feat, conv_w, conv_b, wg, br, bi, lam, h0f, h0b, *, batch, seq):
    n_rows = seq // GRID_W
    vec = lambda n: pl.BlockSpec((n, BLOCK_DIM), lambda b, c: (0, c))
    h0 = pl.BlockSpec((None, 1, BLOCK_DIM), lambda b, c: (b, 0, c))
    buf = pltpu.VMEM((seq, BLOCK_DIM), F32)
    n_i, n_col, tm, _ = feat.shape
    feat5 = feat.reshape(batch, n_i // batch, n_col, tm, D_MODEL)
    return pl.pallas_call(
        functools.partial(_lru_kernel, n_rows=n_rows, rows_per_step=8),
        grid=(batch, N_BLOCKS),
        in_specs=[pl.BlockSpec((None, n_i // batch, None, tm, BLOCK_DIM), lambda b, c: (b, 0, F_Z5, 0, c)),
                  vec(N_CONV), vec(1),
                  pl.BlockSpec((None, BLOCK_DIM, 4 * BLOCK_DIM), lambda b, c: (c, 0, 0)),
                  vec(2), vec(2), vec(2), h0, h0],
        out_specs=pl.BlockSpec((seq, BLOCK_DIM), lambda b, c: (b, c)),
        out_shape=jax.ShapeDtypeStruct((batch * seq, D_MODEL), BF16),
        scratch_shapes=[pltpu.VMEM((seq + 3 * GRID_W, BLOCK_DIM), F32), buf, buf, buf, buf,
                        pltpu.VMEM((GRID_W, BLOCK_DIM), F32), pltpu.VMEM((GRID_W, BLOCK_DIM), F32)],
        compiler_params=pltpu.CompilerParams(
            dimension_semantics=("arbitrary", "arbitrary"), vmem_limit_bytes=VMEM_LIMIT),
        name="lru",
    )(feat5, conv_w, conv_b, wg, br, bi, lam, h0f, h0b)


def _merge_kernel(oa_ref, hx_ref, z4_ref, z6_ref, z7_ref, z8_ref, x_ref, mod_ref, ng_ref,
                  pa_ref, pb_ref, wo_ref, lg_ref, lbias_ref, o_ref):
    f32 = lambda ref: ref[...].astype(F32)
    o_b = (f32(hx_ref) * _silu(f32(z6_ref))).astype(BF16)
    y = _sigmoid(f32(z8_ref)) * _dot(o_b, pb_ref[...])
    y_a = None
    for pair in range(N_HEADS // 2):
        o_h = []
        for h in (2 * pair, 2 * pair + 1):
            sl = slice(h * HEAD_DIM, (h + 1) * HEAD_DIM)
            t = oa_ref[:, sl].astype(F32)
            ms = jnp.mean(t * t, axis=-1, keepdims=True)
            o_h.append((t * lax.rsqrt(ms + RMS_EPS) * ng_ref[...] * _silu(z4_ref[:, sl].astype(F32))).astype(BF16))
        rows = slice(2 * pair * HEAD_DIM, 2 * (pair + 1) * HEAD_DIM)
        part = _dot(jnp.concatenate(o_h, axis=-1), pa_ref[rows, :])
        y_a = part if y_a is None else y_a + part
    y = y + _sigmoid(f32(z7_ref)) * y_a
    y = _dot(y.astype(BF16), wo_ref[...])
    t = DEEPNORM_ALPHA * x_ref[...] + mod_ref[2:3, :] * y
    mu = jnp.mean(t, axis=-1, keepdims=True)
    tc = t - mu
    var = jnp.mean(tc * tc, axis=-1, keepdims=True)
    o_ref[...] = tc * lax.rsqrt(var + LN_EPS) * lg_ref[...] + lbias_ref[...]


def _merge_call(oa, hx, feat_h, x2, mod3, ng, pa, pb, wo, lg, lbias, *, tm, tiles_per_batch):
    n_tok = x2.shape[0]
    tok = pl.BlockSpec((tm, D_MODEL), lambda i: (i, 0))
    per_tile = feat_h.shape[2] // tm
    col = lambda j: pl.BlockSpec((None, None, tm, D_MODEL), lambda i, j=j: (i // per_tile, j, i % per_tile, 0))
    full = lambda shape: pl.BlockSpec(shape, lambda i: (0,) * len(shape), pipeline_mode=pl.Buffered(1))
    return pl.pallas_call(
        _merge_kernel,
        grid=(n_tok // tm,),
        in_specs=[tok, tok, col(H_G4), col(H_G6), col(H_M7), col(H_M8), tok,
                  pl.BlockSpec((None, 3, D_MODEL), lambda i: (i // tiles_per_batch, 0, 0)),
                  full((1, HEAD_DIM)), full((D_MODEL, D_MODEL)), full((D_MODEL, D_MODEL)),
                  full((D_MODEL, D_MODEL)), full((1, D_MODEL)), full((1, D_MODEL))],
        out_specs=tok,
        out_shape=jax.ShapeDtypeStruct((n_tok, D_MODEL), F32),
        compiler_params=pltpu.CompilerParams(
            dimension_semantics=("arbitrary",), vmem_limit_bytes=VMEM_LIMIT),
        name="merge",
    )(oa, hx, feat_h, feat_h, feat_h, feat_h, x2, mod3, ng, pa, pb, wo, lg, lbias)


def kernel(x, c, ctx, c_ctx, w_mod, b_mod, w_in, b_in, lb_logits, norm_a_g, conv_w, conv_b,
           w_r, b_r, w_i, b_i, lam, p_a, p_b, w_out, ln_g, ln_b):
    batch, seq, d = x.shape
    n_ctx = ctx.shape[1]
    assert d == D_MODEL and seq % GRID_W == 0 and w_in.shape[0] == 1

    def select_cols(cols):
        w3 = jnp.stack([w_in[0][:, j * D_MODEL:(j + 1) * D_MODEL] for j in cols]).astype(BF16)
        b3 = jnp.stack([b_in[0][j * D_MODEL:(j + 1) * D_MODEL] for j in cols])[:, None, :]
        return w3, b3

    wg = (0.5 * jnp.concatenate([w_r[0, 0], w_i[0, 0], w_r[0, 1], w_i[0, 1]], axis=-1)).astype(BF16)
    c8 = jnp.zeros((8, D_MODEL), F32).at[:batch].set(c).at[batch].set(c_ctx)

    mod, lb = _mod_call(c8, w_mod[0], b_mod[0][None, :], lb_logits)
    mod3 = mod.reshape(8, 3, D_MODEL)

    x2 = x.reshape(batch * seq, D_MODEL)
    tm = 1024
    feat =_inproj_call(x2, mod3, *select_cols(LATENT_F32), out_dtype=F32, tm=tm, tiles_per_mod=seq // tm)
    feat_h = _inproj_call(x2, mod3, *select_cols(LATENT_BF16), out_dtype=BF16, tm=tm, tiles_per_mod=seq // tm)
    feat_c = _inproj_call(ctx.reshape(batch * n_ctx, D_MODEL), mod3[batch:batch + 1], *select_cols(CTX_F32),
                          out_dtype=F32, tm=n_ctx, tiles_per_mod=batch)

    cw, cb = conv_w[0], conv_b[0][None, :]
    s0f, s0b, h0f, h0b = _ctx_call(feat_c, lb, cw, cb, wg, b_r[0], b_i[0], lam[0], batch=batch, n_ctx=n_ctx)

    o_b = _gla_call(feat, feat_h, lb, F_F1, s0b, reverse=True, batch=batch, seq=seq, rows=16)
    oa = _gla_call(feat, feat_h, lb, F_F0, s0f, o_b, reverse=False, batch=batch, seq=seq, rows=16)
    hx = _lru_call(feat, cw, cb, wg, b_r[0], b_i[0], lam[0], h0f, h0b, batch=batch, seq=seq)

    out = _merge_call(oa, hx, feat_h, x2, mod3, norm_a_g[0][None, :],
                      p_a[0].astype(BF16), p_b[0].astype(BF16), w_out[0].astype(BF16),
                      ln_g[0][None, :], ln_b[0][None, :], tm=512, tiles_per_batch=seq // 512)
    return out.reshape(batch, seq, D_MODEL)
```

```python
import functools

import jax
import jax.numpy as jnp
from jax import lax
from jax.experimental import pallas as pl
from jax.experimental.pallas import tpu as pltpu

F32 = jnp.float32
BF16 = jnp.bfloat16

D_MODEL = 1024
GRID_W = 64
HEAD_DIM = 128
N_HEADS = D_MODEL // HEAD_DIM
N_BLOCKS = 8
BLOCK_DIM = D_MODEL // N_BLOCKS
N_CONV = 4
RG_C = 8.0
LN_EPS = 1e-5
RMS_EPS = 1e-6
DEEPNORM_ALPHA = 2.0 ** 0.25
Q_SCALE = HEAD_DIM ** -0.5

VMEM_LIMIT = 56 * 1024 * 1024

LATENT_F32 = (0, 1, 2, 5)
LATENT_BF16 = (3, 4, 6, 7, 8)
CTX_F32 = (1, 2, 3, 5)
F_Q, F_F0, F_F1, F_Z5 = 0, 1, 2, 3
H_V, H_G4, H_G6, H_M7, H_M8 = 0, 1, 2, 3, 4


def _sigmoid(x):
    return 0.5 * jnp.tanh(0.5 * x) + 0.5


def _silu(x):
    h = 0.5 * x
    return h * jnp.tanh(h) + h


def _log_forget(z, lb):
    return jnp.log((0.5 + 0.5 * lb) + (0.5 - 0.5 * lb) * jnp.tanh(0.5 * z))


def _dot(a, b):
    return jnp.dot(a, b, preferred_element_type=F32)


def _dot_nt(a, b):
    return lax.dot_general(a, b, (((1,), (1,)), ((), ())), preferred_element_type=F32)


def _dot_tn(a, b):
    return lax.dot_general(a, b, (((0,), (0,)), ((), ())), preferred_element_type=F32)


def _chunk_cumprod(x, reverse):
    n, c = x.shape
    nb = n // 8
    y = x.reshape(nb, 8, c)
    sub = lax.broadcasted_iota(jnp.int32, (nb, 8, c), 1)
    for s in (1, 2, 4):
        if reverse:
            y = y * jnp.where(sub < 8 - s, pltpu.roll(y, 8 - s, axis=1), 1.0)
        else:
            y = y * jnp.where(sub >= s, pltpu.roll(y, s, axis=1), 1.0)
    offs = [None] * nb
    acc = jnp.ones((1, c), F32)
    for b in (reversed(range(nb)) if reverse else range(nb)):
        offs[b] = acc
        acc = acc * (y[b, 0:1, :] if reverse else y[b, 7:8, :])
    return (y * jnp.stack(offs)).reshape(n, c), acc


def _cumsum_rows(mask_bf16, x):
    hi = x.astype(BF16)
    r1 = x - hi.astype(F32)
    mid = r1.astype(BF16)
    lo = (r1 - mid.astype(F32)).astype(BF16)
    return _dot(mask_bf16, hi) + _dot(mask_bf16, mid) + _dot(mask_bf16, lo)


def _mod_kernel(c_ref, w_ref, b_ref, lbl_ref, mod_ref, lb_ref):
    mod_ref[...] = _dot(_silu(c_ref[...]), w_ref[...]) + b_ref[...]
    l = lbl_ref[...]
    e = jnp.exp(l - jnp.max(l, axis=0, keepdims=True))
    lb_ref[...] = e[0] / jnp.sum(e, axis=0)


def _mod_call(c8, w_mod, b_mod, lb_logits):
    return pl.pallas_call(
        _mod_kernel,
        out_shape=(jax.ShapeDtypeStruct((8, 3 * D_MODEL), F32),
                   jax.ShapeDtypeStruct((2, D_MODEL), F32)),
        compiler_params=pltpu.CompilerParams(vmem_limit_bytes=VMEM_LIMIT),
        name="mod",
    )(c8, w_mod, b_mod, lb_logits)


def _inproj_kernel(x_ref, mod_ref, w_ref, b_ref, o_ref, *, head_major):
    u = (x_ref[...] * (1.0 + mod_ref[1:2, :]) + mod_ref[0:1, :]).astype(BF16)
    for j in range(w_ref.shape[0]):
        z = (_dot(u, w_ref[j]) + b_ref[j]).astype(o_ref.dtype)
        if head_major:
            for h in range(N_HEADS):
                o_ref[j, h] = z[:, h * HEAD_DIM:(h + 1) * HEAD_DIM]
        else:
            o_ref[j] = z


def _inproj_call(x2, mod3, w3, b3, *, out_dtype, tm, tiles_per_mod, head_major=False):
    n_tok = x2.shape[0]
    n_col = w3.shape[0]
    tile = (N_HEADS, tm, HEAD_DIM) if head_major else (tm, D_MODEL)
    const = lambda shape: pl.BlockSpec(shape, lambda i: (0,) * len(shape), pipeline_mode=pl.Buffered(1))
    return pl.pallas_call(
        functools.partial(_inproj_kernel, head_major=head_major),
        grid=(n_tok // tm,),
        in_specs=[
            pl.BlockSpec((tm, D_MODEL), lambda i: (i, 0)),
            pl.BlockSpec((None, 3, D_MODEL), lambda i: (i // tiles_per_mod, 0, 0)),
            const((n_col, D_MODEL, D_MODEL)),
            const((n_col, 1, D_MODEL)),
        ],
        out_specs=pl.BlockSpec((None, n_col) + tile, lambda i: (i, 0) + (0,) * len(tile)),
        out_shape=jax.ShapeDtypeStruct((n_tok // tm, n_col) + tile, out_dtype),
        compiler_params=pltpu.CompilerParams(
            dimension_semantics=("arbitrary",), vmem_limit_bytes=VMEM_LIMIT),
        name="inproj",
    )(x2, mod3, w3, b3)


LOG2_E = 1.4426950408889634


def _lru_ab(xc, h_r, h_i, br, bi, sp):
    k = (-0.5 * RG_C * LOG2_E) * sp
    a = jnp.exp2(k * jnp.tanh(h_r + 0.5 * br) + k)
    i = 0.5 * jnp.tanh(h_i + 0.5 * bi) + 0.5
    y = 1.0 - a * a
    mult = jnp.where(y > 0.0, y * lax.rsqrt(y), 0.0)
    return a, mult * (i * xc)


def _softplus(y):
    return jnp.maximum(y, 0.0) + jnp.log(1.0 + jnp.exp(-jnp.abs(y)))


def _ctx_kernel(zf_ref, zb_ref, v_ref, z5_ref, lb_ref, cw_ref, cb_ref, wg_ref, br_ref, bi_ref, lam_ref,
                sf_ref, sb_ref, hf_ref, hb_ref, zp_ref, a_ref, b_ref):
    n = zf_ref.shape[0]
    ri = lax.broadcasted_iota(jnp.int32, (n, n), 0)
    ci = lax.broadcasted_iota(jnp.int32, (n, n), 1)
    tril = (ci <= ri).astype(F32).astype(BF16)
    v = v_ref[...].astype(BF16)

    lf = _log_forget(zf_ref[...], lb_ref[0:1, :])
    g = _cumsum_rows(tril, lf)
    ke = ((1.0 - jnp.exp(lf)) * jnp.exp(g[n - 1:n, :] - g)).astype(BF16)
    for h in range(N_HEADS):
        sl = slice(h * HEAD_DIM, (h + 1) * HEAD_DIM)
        sf_ref[h] = _dot_tn(v[:, sl], ke[:, sl])
    lf = _log_forget(zb_ref[...], lb_ref[1:2, :])
    g = _cumsum_rows(tril, lf)
    ke = ((1.0 - jnp.exp(lf)) * jnp.exp(g - lf)).astype(BF16)
    for h in range(N_HEADS):
        sl = slice(h * HEAD_DIM, (h + 1) * HEAD_DIM)
        sb_ref[h] = _dot_tn(v[:, sl], ke[:, sl])

    zp_ref[0:8, :] = jnp.zeros((8, D_MODEL), F32)
    zp_ref[8 + n:16 + n, :] = jnp.zeros((8, D_MODEL), F32)
    zp_ref[8:8 + n, :] = z5_ref[...]
    xc = cb_ref[...] + zp_ref[7:7 + n, :] * cw_ref[0:1, :]
    for kk in range(1, N_CONV):
        xc = xc + zp_ref[7 + kk:7 + kk + n, :] * cw_ref[kk:kk + 1, :]

    for d, h_ref in ((0, hf_ref), (1, hb_ref)):
        sp = _softplus(-lam_ref[d:d + 1, :])
        for blk in range(N_BLOCKS):
            sl = slice(blk * BLOCK_DIM, (blk + 1) * BLOCK_DIM)
            g = _dot(xc[:, sl].astype(BF16), wg_ref[blk, :, 2 * d * BLOCK_DIM:2 * (d + 1) * BLOCK_DIM])
            a, b = _lru_ab(xc[:, sl], g[:, :BLOCK_DIM], g[:, BLOCK_DIM:],
                           br_ref[d:d + 1, sl], bi_ref[d:d + 1, sl], sp[:, sl])
            a_ref[:, sl] = a
            b_ref[:, sl] = b

        def step(t, h, d=d):
            tt = (n - 1 - t) if d == 1 else t
            return a_ref[pl.ds(tt, 1), :] * h + b_ref[pl.ds(tt, 1), :]

        h_ref[...] = lax.fori_loop(0, n, step, jnp.zeros((1, D_MODEL), F32))


def _ctx_call(feat_c, lb, conv_w, conv_b, wg, br, bi, lam, *, batch, n_ctx):
    feat = lambda j: pl.BlockSpec((None, None, n_ctx, D_MODEL), lambda b, j=j: (b, j, 0, 0))
    full = lambda shape: pl.BlockSpec(shape, lambda b: (0,) * len(shape))
    state = pl.BlockSpec((None, N_HEADS, HEAD_DIM, HEAD_DIM), lambda b: (b, 0, 0, 0))
    hvec = pl.BlockSpec((None, 1, D_MODEL), lambda b: (b, 0, 0))
    return pl.pallas_call(
        _ctx_kernel,
        grid=(batch,),
        in_specs=[feat(0), feat(1), feat(2), feat(3), full((2, D_MODEL)),
                  full((N_CONV, D_MODEL)), full((1, D_MODEL)),
                  full((N_BLOCKS, BLOCK_DIM, 4 * BLOCK_DIM)),
                  full((2, D_MODEL)), full((2, D_MODEL)), full((2, D_MODEL))],
        out_specs=(state, state, hvec, hvec),
        out_shape=(jax.ShapeDtypeStruct((batch, N_HEADS, HEAD_DIM, HEAD_DIM), F32),
                   jax.ShapeDtypeStruct((batch, N_HEADS, HEAD_DIM, HEAD_DIM), F32),
                   jax.ShapeDtypeStruct((batch, 1, D_MODEL), F32),
                   jax.ShapeDtypeStruct((batch, 1, D_MODEL), F32)),
        scratch_shapes=[pltpu.VMEM((n_ctx + 16, D_MODEL), F32),
                        pltpu.VMEM((n_ctx, D_MODEL), F32),
                        pltpu.VMEM((n_ctx, D_MODEL), F32)],
        compiler_params=pltpu.CompilerParams(
            dimension_semantics=("arbitrary",), vmem_limit_bytes=VMEM_LIMIT),
        name="ctx_states",
    )(feat_c, feat_c, feat_c, feat_c, lb, conv_w, conv_b, wg, br, bi, lam)


GLA_ROWS_PER_ITER = 8


def _gla_kernel(*refs, reverse, rows, finish):
    if finish:
        (zq_ref, zf_ref, v_ref, lb_ref, s0_ref, other_ref, o_ref,
         st_ref, sn_ref, qd_ref, kit_ref, ke_ref, dec_ref) = refs
    else:
        zq_ref, zf_ref, v_ref, lb_ref, s0_ref, o_ref, st_ref, sn_ref, qd_ref, kit_ref, ke_ref, dec_ref = refs

    @pl.when(pl.program_id(1) == 0)
    def _():
        st_ref[...] = s0_ref[...]
        for h in range(N_HEADS):
            sn_ref[h] = s0_ref[h].astype(BF16).T

    ri = lax.broadcasted_iota(jnp.int32, (GRID_W, GRID_W), 0)
    ci = lax.broadcasted_iota(jnp.int32, (GRID_W, GRID_W), 1)
    allow = (ci >= ri) if reverse else (ci <= ri)
    lb_dir = 1 if reverse else 0

    def row_offset(n):
        n = jnp.minimum(n, rows - 1)
        return pl.multiple_of(((rows - 1 - n) if reverse else n) * GRID_W, GRID_W)

    def prepare(n, slot):
        off = row_offset(n)
        zq = zq_ref[pl.ds(off, GRID_W), :]
        hq = (0.5 * Q_SCALE) * zq
        q = hq * jnp.tanh(0.5 * zq) + hq
        lb = lb_ref[lb_dir:lb_dir + 1, :]
        f = (0.5 + 0.5 * lb) + (0.5 - 0.5 * lb) * jnp.tanh(0.5 * zf_ref[pl.ds(off, GRID_W), :])
        dg, dec = _chunk_cumprod(f, reverse)
        ki = (1.0 - f) / dg
        qd = (q * dg).astype(BF16)
        ke = (ki * dec).astype(BF16)
        kit_ref[slot] = ki.astype(BF16).T
        dec_ref[slot] = dec
        for h in range(N_HEADS):
            sl = slice(h * HEAD_DIM, (h + 1) * HEAD_DIM)
            qd_ref[slot, h] = qd[:, sl]
            ke_ref[slot, h] = ke[:, sl]

    def contract(n, slot):
        off = row_offset(n)
        for h in range(N_HEADS):
            sl = slice(h * HEAD_DIM, (h + 1) * HEAD_DIM)
            qd = qd_ref[slot, h]
            v = v_ref[h, pl.ds(off, GRID_W), :]
            p = jnp.where(allow, _dot(qd, kit_ref[slot, sl, :]), 0.0).astype(BF16)
            o = _dot(p, v) + _dot(qd, sn_ref[h])
            st = st_ref[h] * dec_ref[slot, :, sl] + _dot_tn(v, ke_ref[slot, h])
            st_ref[h] = st
            sn_ref[h] = st.astype(BF16).T
            if finish:
                o = o + other_ref[h, pl.ds(off, GRID_W), :].astype(F32)
            o_ref[h, pl.ds(off, GRID_W), :] = o.astype(o_ref.dtype)

    prepare(0, 0)

    def body(m, carry):
        n = GLA_ROWS_PER_ITER * m
        for i in range(GLA_ROWS_PER_ITER):
            contract(n + i, i % 2)
            prepare(n + i + 1, (i + 1) % 2)
        return carry

    lax.fori_loop(0, rows // GLA_ROWS_PER_ITER, body, 0)


def _gla_call(feat, feat_h, lb, f_col, s0, other=None, *, reverse, batch, seq, rows):
    tb = rows * GRID_W
    nrb = seq // tb
    finish = other is not None
    per_tile = feat.shape[2] // tb

    def row_block(b, i):
        return b * nrb + ((nrb - 1 - i) if reverse else i)

    def col(c):
        return pl.BlockSpec((None, None, tb, D_MODEL), lambda b, i, c=c: (
            row_block(b, i) // per_tile, c, row_block(b, i) % per_tile, 0))

    tok = pl.BlockSpec((N_HEADS, tb, HEAD_DIM), lambda b, i: (0, row_block(b, i), 0))
    v_spec = pl.BlockSpec((None, None, N_HEADS, tb, HEAD_DIM), lambda b, i: (
        row_block(b, i) // per_tile, H_V, 0, row_block(b, i) % per_tile, 0))
    in_specs = [col(F_Q), col(f_col), v_spec,
                pl.BlockSpec((2, D_MODEL), lambda b, i: (0, 0)),
                pl.BlockSpec((None, N_HEADS, HEAD_DIM, HEAD_DIM), lambda b, i: (b, 0, 0, 0))]
    args = [feat, feat, feat_h, lb, s0]
    if finish:
        in_specs += [tok]
        args += [other]
    return pl.pallas_call(
        functools.partial(_gla_kernel, reverse=reverse, rows=rows, finish=finish),
        grid=(batch, nrb),
        in_specs=in_specs,
        out_specs=tok,
        out_shape=jax.ShapeDtypeStruct((N_HEADS, batch * seq, HEAD_DIM), BF16),
        scratch_shapes=[pltpu.VMEM((N_HEADS, HEAD_DIM, HEAD_DIM), F32)]
        + [pltpu.VMEM((N_HEADS, HEAD_DIM, HEAD_DIM), BF16), pltpu.VMEM((2, N_HEADS, GRID_W, HEAD_DIM), BF16),
           pltpu.VMEM((2, D_MODEL, GRID_W), BF16), pltpu.VMEM((2, N_HEADS, GRID_W, HEAD_DIM), BF16),
           pltpu.VMEM((2, 1, D_MODEL), F32)],
        compiler_params=pltpu.CompilerParams(
            dimension_semantics=("arbitrary", "arbitrary"), vmem_limit_bytes=VMEM_LIMIT),
        name="gla_bwd" if reverse else "gla_fwd",
    )(*args)


def _lru_kernel(z5_ref, cw_ref, cb_ref, wg_ref, br_ref, bi_ref, lam_ref, h0f_ref, h0b_ref, o_ref,
                zp_ref, af_ref, bf_ref, ab_ref, bb_ref, cf_ref, cr_ref, *, n_rows, rows_per_step):
    w = GRID_W
    seq = n_rows * w
    tb = rows_per_step * w
    n_steps = n_rows // rows_per_step

    zp_ref[0:w, :] = jnp.zeros((w, BLOCK_DIM), F32)
    zp_ref[w + seq:, :] = jnp.zeros((2 * w, BLOCK_DIM), F32)
    tile = z5_ref.shape[1]
    for k in range(z5_ref.shape[0]):
        zp_ref[w + k * tile:w + (k + 1) * tile, :] = z5_ref[k]

    sp = [_softplus(-lam_ref[d:d + 1, :]) for d in (0, 1)]

    def gates(s, c):
        off = pl.multiple_of(s * tb, tb)
        xc = cb_ref[...] + zp_ref[pl.ds(off, tb), :] * cw_ref[0:1, :]
        for kk in range(1, N_CONV):
            xc = xc + zp_ref[pl.ds(off + kk * w, tb), :] * cw_ref[kk:kk + 1, :]
        g = _dot(xc.astype(BF16), wg_ref[...])
        for d, a_ref, b_ref in ((0, af_ref, bf_ref), (1, ab_ref, bb_ref)):
            c0 = 2 * d * BLOCK_DIM
            a, b = _lru_ab(xc, g[:, c0:c0 + BLOCK_DIM], g[:, c0 + BLOCK_DIM:c0 + 2 * BLOCK_DIM],
                           br_ref[d:d + 1, :], bi_ref[d:d + 1, :], sp[d])
            a_ref[pl.ds(off, tb), :] = a
            b_ref[pl.ds(off, tb), :] = b
        return c

    lax.fori_loop(0, n_steps, gates, 0)

    def scan(n, carry):
        hf, pf, hb, pb = carry
        off_f = pl.multiple_of(n * w, w)
        off_b = pl.multiple_of((n_rows - 1 - n) * w, w)
        a = af_ref[pl.ds(off_f, w), :]
        hf = a * hf + bf_ref[pl.ds(off_f, w), :]
        pf = pf * a
        bf_ref[pl.ds(off_f, w), :] = hf
        af_ref[pl.ds(off_f, w), :] = pf
        a = ab_ref[pl.ds(off_b, w), :]
        hb = a * hb + bb_ref[pl.ds(off_b, w), :]
        pb = pb * a
        bb_ref[pl.ds(off_b, w), :] = hb
        ab_ref[pl.ds(off_b, w), :] = pb
        return hf, pf, hb, pb

    zeros = jnp.zeros((w, BLOCK_DIM), F32)
    ones = jnp.ones((w, BLOCK_DIM), F32)
    lax.fori_loop(0, n_rows, scan, (zeros, ones, zeros, ones))

    last = (n_rows - 1) * w

    def carry(n, c):
        cf, cb = c
        cf_ref[pl.ds(n, 1), :] = cf
        cf = bf_ref[pl.ds(last + n, 1), :] + af_ref[pl.ds(last + n, 1), :] * cf
        col = w - 1 - n
        cr_ref[pl.ds(col, 1), :] = cb
        cb = bb_ref[pl.ds(col, 1), :] + ab_ref[pl.ds(col, 1), :] * cb
        return cf, cb

    lax.fori_loop(0, w, carry, (h0f_ref[...], h0b_ref[...]))

    def fix(s, c):
        off = pl.multiple_of(s * tb, tb)
        cf = jnp.concatenate([cf_ref[...]] * rows_per_step, axis=0)
        cr = jnp.concatenate([cr_ref[...]] * rows_per_step, axis=0)
        h = (bf_ref[pl.ds(off, tb), :] + af_ref[pl.ds(off, tb), :] * cf
             + bb_ref[pl.ds(off, tb), :] + ab_ref[pl.ds(off, tb), :] * cr)
        o_ref[pl.ds(off, tb), :] = h.astype(o_ref.dtype)
        return c

    lax.fori_loop(0, n_steps, fix, 0)


def _lru_call(feat, conv_w, conv_b, wg, br, bi, lam, h0f, h0b, *, batch, seq):
    n_rows = seq // GRID_W
    vec = lambda n: pl.BlockSpec((n, BLOCK_DIM), lambda b, c: (0, c))
    h0 = pl.BlockSpec((None, 1, BLOCK_DIM), lambda b, c: (b, 0, c))
    buf = pltpu.VMEM((seq, BLOCK_DIM), F32)
    n_i, n_col, tm, _ = feat.shape
    feat5 = feat.reshape(batch, n_i // batch, n_col, tm, D_MODEL)
    return pl.pallas_call(
        functools.partial(_lru_kernel, n_rows=n_rows, rows_per_step=8),
        grid=(batch, N_BLOCKS),
        in_specs=[pl.BlockSpec((None, n_i // batch, None, tm, BLOCK_DIM), lambda b, c: (b, 0, F_Z5, 0, c)),
                  vec(N_CONV), vec(1),
                  pl.BlockSpec((None, BLOCK_DIM, 4 * BLOCK_DIM), lambda b, c: (c, 0, 0)),
                  vec(2), vec(2), vec(2), h0, h0],
        out_specs=pl.BlockSpec((None, seq, BLOCK_DIM), lambda b, c: (c, b, 0)),
        out_shape=jax.ShapeDtypeStruct((N_BLOCKS, batch * seq, BLOCK_DIM), BF16),
        scratch_shapes=[pltpu.VMEM((seq + 3 * GRID_W, BLOCK_DIM), F32), buf, buf, buf, buf,
                        pltpu.VMEM((GRID_W, BLOCK_DIM), F32), pltpu.VMEM((GRID_W, BLOCK_DIM), F32)],
        compiler_params=pltpu.CompilerParams(
            dimension_semantics=("arbitrary", "arbitrary"), vmem_limit_bytes=VMEM_LIMIT),
        name="lru",
    )(feat5, conv_w, conv_b, wg, br, bi, lam, h0f, h0b)


def _merge_kernel(oa_ref, hx_ref, z4_ref, z6_ref, z7_ref, z8_ref, x_ref, mod_ref, ng_ref,
                  pa_ref, pb_ref, wo_ref, lg_ref, lbias_ref, o_ref):
    f32 = lambda ref: jnp.concatenate([ref[h] for h in range(N_HEADS)], axis=-1).astype(F32)
    o_b = (f32(hx_ref) * _silu(f32(z6_ref))).astype(BF16)
    y = _sigmoid(f32(z8_ref)) * _dot(o_b, pb_ref[...])
    y_a = None
    for pair in range(N_HEADS // 2):
        o_h = []
        for h in (2 * pair, 2 * pair + 1):
            t = oa_ref[h].astype(F32)
            ms = jnp.mean(t * t, axis=-1, keepdims=True)
            o_h.append((t * lax.rsqrt(ms + RMS_EPS) * ng_ref[...] * _silu(z4_ref[h].astype(F32))).astype(BF16))
        rows = slice(2 * pair * HEAD_DIM, 2 * (pair + 1) * HEAD_DIM)
        part = _dot(jnp.concatenate(o_h, axis=-1), pa_ref[rows, :])
        y_a = part if y_a is None else y_a + part
    y = y + _sigmoid(f32(z7_ref)) * y_a
    y = _dot(y.astype(BF16), wo_ref[...])
    t = DEEPNORM_ALPHA * x_ref[...] + mod_ref[2:3, :] * y
    mu = jnp.mean(t, axis=-1, keepdims=True)
    tc = t - mu
    var = jnp.mean(tc * tc, axis=-1, keepdims=True)
    o_ref[...] = tc * lax.rsqrt(var + LN_EPS) * lg_ref[...] + lbias_ref[...]


def _merge_call(oa, hx, feat_h, x2, mod3, ng, pa, pb, wo, lg, lbias, *, tm, tiles_per_batch):
    n_tok = x2.shape[0]
    tok = pl.BlockSpec((tm, D_MODEL), lambda i: (i, 0))
    heads = pl.BlockSpec((N_HEADS, tm, HEAD_DIM), lambda i: (0, i, 0))
    per_tile = feat_h.shape[3] // tm
    col = lambda j: pl.BlockSpec((None, None, N_HEADS, tm, HEAD_DIM),
                                 lambda i, j=j: (i // per_tile, j, 0, i % per_tile, 0))
    full = lambda shape: pl.BlockSpec(shape, lambda i: (0,) * len(shape), pipeline_mode=pl.Buffered(1))
    return pl.pallas_call(
        _merge_kernel,
        grid=(n_tok // tm,),
        in_specs=[heads, heads, col(H_G4), col(H_G6), col(H_M7), col(H_M8), tok,
                  pl.BlockSpec((None, 3, D_MODEL), lambda i: (i // tiles_per_batch, 0, 0)),
                  full((1, HEAD_DIM)), full((D_MODEL, D_MODEL)), full((D_MODEL, D_MODEL)),
                  full((D_MODEL, D_MODEL)), full((1, D_MODEL)), full((1, D_MODEL))],
        out_specs=tok,
        out_shape=jax.ShapeDtypeStruct((n_tok, D_MODEL), F32),
        compiler_params=pltpu.CompilerParams(
            dimension_semantics=("arbitrary",), vmem_limit_bytes=VMEM_LIMIT),
        name="merge",
    )(oa, hx, feat_h, feat_h, feat_h, feat_h, x2, mod3, ng, pa, pb, wo, lg, lbias)


def kernel(x, c, ctx, c_ctx, w_mod, b_mod, w_in, b_in, lb_logits, norm_a_g, conv_w, conv_b,
           w_r, b_r, w_i, b_i, lam, p_a, p_b, w_out, ln_g, ln_b):
    batch, seq, d = x.shape
    n_ctx = ctx.shape[1]
    assert d == D_MODEL and seq % GRID_W == 0 and w_in.shape[0] == 1

    def select_cols(cols):
        w3 = jnp.stack([w_in[0][:, j * D_MODEL:(j + 1) * D_MODEL] for j in cols]).astype(BF16)
        b3 = jnp.stack([b_in[0][j * D_MODEL:(j + 1) * D_MODEL] for j in cols])[:, None, :]
        return w3, b3

    wg = (0.5 * jnp.concatenate([w_r[0, 0], w_i[0, 0], w_r[0, 1], w_i[0, 1]], axis=-1)).astype(BF16)
    c8 = jnp.zeros((8, D_MODEL), F32).at[:batch].set(c).at[batch].set(c_ctx)

    mod, lb = _mod_call(c8, w_mod[0], b_mod[0][None, :], lb_logits)
    mod3 = mod.reshape(8, 3, D_MODEL)

    x2 = x.reshape(batch * seq, D_MODEL)
    tm = 1024
    feat =_inproj_call(x2, mod3, *select_cols(LATENT_F32), out_dtype=F32, tm=tm, tiles_per_mod=seq // tm)
    feat_h = _inproj_call(x2, mod3, *select_cols(LATENT_BF16), out_dtype=BF16, tm=tm, tiles_per_mod=seq // tm,
                          head_major=True)
    feat_c = _inproj_call(ctx.reshape(batch * n_ctx, D_MODEL), mod3[batch:batch + 1], *select_cols(CTX_F32),
                          out_dtype=F32, tm=n_ctx, tiles_per_mod=batch)

    cw, cb = conv_w[0], conv_b[0][None, :]
    s0f, s0b, h0f, h0b = _ctx_call(feat_c, lb, cw, cb, wg, b_r[0], b_i[0], lam[0], batch=batch, n_ctx=n_ctx)

    o_b = _gla_call(feat, feat_h, lb, F_F1, s0b, reverse=True, batch=batch, seq=seq, rows=16)
    oa = _gla_call(feat, feat_h, lb, F_F0, s0f, o_b, reverse=False, batch=batch, seq=seq, rows=16)
    hx = _lru_call(feat, cw, cb, wg, b_r[0], b_i[0], lam[0], h0f, h0b, batch=batch, seq=seq)

    out = _merge_call(oa, hx, feat_h, x2, mod3, norm_a_g[0][None, :],
                      p_a[0].astype(BF16), p_b[0].astype(BF16), w_out[0].astype(BF16),
                      ln_g[0][None, :], ln_b[0][None, :], tm=512, tiles_per_batch=seq // 512)
    return out.reshape(batch, seq, D_MODEL)
```

```python
import functools

import jax
import jax.numpy as jnp
from jax import lax
from jax.experimental import pallas as pl
from jax.experimental.pallas import tpu as pltpu

F32 = jnp.float32
BF16 = jnp.bfloat16

D_MODEL = 1024
GRID_W = 64
HEAD_DIM = 128
N_HEADS = D_MODEL // HEAD_DIM
N_BLOCKS = 8
BLOCK_DIM = D_MODEL // N_BLOCKS
N_CONV = 4
RG_C = 8.0
LN_EPS = 1e-5
RMS_EPS = 1e-6
DEEPNORM_ALPHA = 2.0 ** 0.25
Q_SCALE = HEAD_DIM ** -0.5

VMEM_LIMIT = 56 * 1024 * 1024

LATENT_F32 = (0, 1, 2, 5)
LATENT_BF16 = (3, 4, 6, 7, 8)
CTX_F32 = (1, 2, 3, 5)
F_Q, F_F0, F_F1, F_Z5 = 0, 1, 2, 3
H_V, H_G4, H_G6, H_M7, H_M8 = 0, 1, 2, 3, 4


def _sigmoid(x):
    return 0.5 * jnp.tanh(0.5 * x) + 0.5


def _silu(x):
    h = 0.5 * x
    return h * jnp.tanh(h) + h


def _log_forget(z, lb):
    return jnp.log((0.5 + 0.5 * lb) + (0.5 - 0.5 * lb) * jnp.tanh(0.5 * z))


def _dot(a, b):
    return jnp.dot(a, b, preferred_element_type=F32)


def _dot_nt(a, b):
    return lax.dot_general(a, b, (((1,), (1,)), ((), ())), preferred_element_type=F32)


def _dot_tn(a, b):
    return lax.dot_general(a, b, (((0,), (0,)), ((), ())), preferred_element_type=F32)


def _chunk_cumprod(x, reverse):
    n, c = x.shape
    nb = n // 8
    y = x.reshape(nb, 8, c)
    sub = lax.broadcasted_iota(jnp.int32, (nb, 8, c), 1)
    for s in (1, 2, 4):
        if reverse:
            y = y * jnp.where(sub < 8 - s, pltpu.roll(y, 8 - s, axis=1), 1.0)
        else:
            y = y * jnp.where(sub >= s, pltpu.roll(y, s, axis=1), 1.0)
    offs = [None] * nb
    acc = jnp.ones((1, c), F32)
    for b in (reversed(range(nb)) if reverse else range(nb)):
        offs[b] = acc
        acc = acc * (y[b, 0:1, :] if reverse else y[b, 7:8, :])
    return (y * jnp.stack(offs)).reshape(n, c), acc


def _cumsum_rows(mask_bf16, x):
    hi = x.astype(BF16)
    r1 = x - hi.astype(F32)
    mid = r1.astype(BF16)
    lo = (r1 - mid.astype(F32)).astype(BF16)
    return _dot(mask_bf16, hi) + _dot(mask_bf16, mid) + _dot(mask_bf16, lo)


def _mod_kernel(c_ref, cc_ref, w_ref, b_ref, lbl_ref, mod_ref, lb_ref, rows_ref):
    n = c_ref.shape[0]
    rows_ref[...] = jnp.zeros(rows_ref.shape, F32)
    rows_ref[0:n, :] = c_ref[...]
    rows_ref[n:n + 1, :] = cc_ref[...]
    mod_ref[...] = _dot(_silu(rows_ref[...]), w_ref[...]) + b_ref[...]
    l = lbl_ref[...]
    e = jnp.exp(l - jnp.max(l, axis=0, keepdims=True))
    lb_ref[...] = e[0] / jnp.sum(e, axis=0)


def _mod_call(c, c_ctx, w_mod, b_mod, lb_logits):
    return pl.pallas_call(
        _mod_kernel,
        out_shape=(jax.ShapeDtypeStruct((8, 3 * D_MODEL), F32),
                   jax.ShapeDtypeStruct((2, D_MODEL), F32)),
        scratch_shapes=[pltpu.VMEM((8, D_MODEL), F32)],
        compiler_params=pltpu.CompilerParams(vmem_limit_bytes=VMEM_LIMIT),
        name="mod",
    )(c, c_ctx, w_mod, b_mod, lb_logits)


def _inproj_kernel(x_ref, mod_ref, *refs, head_major):
    n_col = (len(refs) - 1) // 2
    w_refs, b_refs, o_ref = refs[:n_col], refs[n_col:2 * n_col], refs[-1]
    u = (x_ref[...] * (1.0 + mod_ref[1:2, :]) + mod_ref[0:1, :]).astype(BF16)
    for j in range(n_col):
        z = (_dot(u, w_refs[j][...]) + b_refs[j][...]).astype(o_ref.dtype)
        if head_major:
            for h in range(N_HEADS):
                o_ref[j, h] = z[:, h * HEAD_DIM:(h + 1) * HEAD_DIM]
        else:
            o_ref[j] = z


def _inproj_call(x2, mod3, w, b, cols, *, out_dtype, tm, tiles_per_mod, head_major=False):
    n_tok = x2.shape[0]
    n_col = len(cols)
    tile = (N_HEADS, tm, HEAD_DIM) if head_major else (tm, D_MODEL)
    col_block = lambda rows: [pl.BlockSpec((rows, D_MODEL), lambda i, c=c: (0, c), pipeline_mode=pl.Buffered(1))
                              for c in cols]
    return pl.pallas_call(
        functools.partial(_inproj_kernel, head_major=head_major),
        grid=(n_tok // tm,),
        in_specs=[
            pl.BlockSpec((tm, D_MODEL), lambda i: (i, 0)),
            pl.BlockSpec((None, 3, D_MODEL), lambda i: (i // tiles_per_mod, 0, 0)),
            *col_block(D_MODEL),
            *col_block(1),
        ],
        out_specs=pl.BlockSpec((None, n_col) + tile, lambda i: (i, 0) + (0,) * len(tile)),
        out_shape=jax.ShapeDtypeStruct((n_tok // tm, n_col) + tile, out_dtype),
        compiler_params=pltpu.CompilerParams(
            dimension_semantics=("arbitrary",), vmem_limit_bytes=VMEM_LIMIT),
        name="inproj",
    )(x2, mod3, *([w] * n_col), *([b] * n_col))


LOG2_E = 1.4426950408889634


def _lru_ab(xc, h_r, h_i, br, bi, sp):
    k = (-0.5 * RG_C * LOG2_E) * sp
    a = jnp.exp2(k * jnp.tanh(h_r + 0.5 * br) + k)
    i = 0.5 * jnp.tanh(h_i + 0.5 * bi) + 0.5
    y = 1.0 - a * a
    mult = jnp.where(y > 0.0, y * lax.rsqrt(y), 0.0)
    return a, mult * (i * xc)


def _softplus(y):
    return jnp.maximum(y, 0.0) + jnp.log(1.0 + jnp.exp(-jnp.abs(y)))


def _ctx_kernel(zf_ref, zb_ref, v_ref, z5_ref, lb_ref, cw_ref, cb_ref, wg_ref, br_ref, bi_ref, lam_ref,
                sf_ref, sb_ref, hf_ref, hb_ref, zp_ref, a_ref, b_ref):
    n = zf_ref.shape[0]
    ri = lax.broadcasted_iota(jnp.int32, (n, n), 0)
    ci = lax.broadcasted_iota(jnp.int32, (n, n), 1)
    tril = (ci <= ri).astype(F32).astype(BF16)
    v = v_ref[...].astype(BF16)

    lf = _log_forget(zf_ref[...], lb_ref[0:1, :])
    g = _cumsum_rows(tril, lf)
    ke = ((1.0 - jnp.exp(lf)) * jnp.exp(g[n - 1:n, :] - g)).astype(BF16)
    for h in range(N_HEADS):
        sl = slice(h * HEAD_DIM, (h + 1) * HEAD_DIM)
        sf_ref[h] = _dot_tn(v[:, sl], ke[:, sl])
    lf = _log_forget(zb_ref[...], lb_ref[1:2, :])
    g = _cumsum_rows(tril, lf)
    ke = ((1.0 - jnp.exp(lf)) * jnp.exp(g - lf)).astype(BF16)
    for h in range(N_HEADS):
        sl = slice(h * HEAD_DIM, (h + 1) * HEAD_DIM)
        sb_ref[h] = _dot_tn(v[:, sl], ke[:, sl])

    zp_ref[0:8, :] = jnp.zeros((8, D_MODEL), F32)
    zp_ref[8 + n:16 + n, :] = jnp.zeros((8, D_MODEL), F32)
    zp_ref[8:8 + n, :] = z5_ref[...]
    xc = cb_ref[...] + zp_ref[7:7 + n, :] * cw_ref[0:1, :]
    for kk in range(1, N_CONV):
        xc = xc + zp_ref[7 + kk:7 + kk + n, :] * cw_ref[kk:kk + 1, :]

    for d, h_ref in ((0, hf_ref), (1, hb_ref)):
        sp = _softplus(-lam_ref[d:d + 1, :])
        for blk in range(N_BLOCKS):
            sl = slice(blk * BLOCK_DIM, (blk + 1) * BLOCK_DIM)
            g = _dot(xc[:, sl].astype(BF16), wg_ref[blk, :, 2 * d * BLOCK_DIM:2 * (d + 1) * BLOCK_DIM])
            a, b = _lru_ab(xc[:, sl], g[:, :BLOCK_DIM], g[:, BLOCK_DIM:],
                           br_ref[d:d + 1, sl], bi_ref[d:d + 1, sl], sp[:, sl])
            a_ref[:, sl] = a
            b_ref[:, sl] = b

        def step(t, h, d=d):
            tt = (n - 1 - t) if d == 1 else t
            return a_ref[pl.ds(tt, 1), :] * h + b_ref[pl.ds(tt, 1), :]

        h_ref[...] = lax.fori_loop(0, n, step, jnp.zeros((1, D_MODEL), F32))


def _ctx_call(feat_c, lb, conv_w, conv_b, wg, br, bi, lam, *, batch, n_ctx):
    feat = lambda j: pl.BlockSpec((None, None, n_ctx, D_MODEL), lambda b, j=j: (b, j, 0, 0))
    full = lambda shape: pl.BlockSpec(shape, lambda b: (0,) * len(shape))
    state = pl.BlockSpec((None, N_HEADS, HEAD_DIM, HEAD_DIM), lambda b: (b, 0, 0, 0))
    hvec = pl.BlockSpec((None, 1, D_MODEL), lambda b: (b, 0, 0))
    return pl.pallas_call(
        _ctx_kernel,
        grid=(batch,),
        in_specs=[feat(0), feat(1), feat(2), feat(3), full((2, D_MODEL)),
                  full((N_CONV, D_MODEL)), full((1, D_MODEL)),
                  full((N_BLOCKS, BLOCK_DIM, 4 * BLOCK_DIM)),
                  full((2, D_MODEL)), full((2, D_MODEL)), full((2, D_MODEL))],
        out_specs=(state, state, hvec, hvec),
        out_shape=(jax.ShapeDtypeStruct((batch, N_HEADS, HEAD_DIM, HEAD_DIM), F32),
                   jax.ShapeDtypeStruct((batch, N_HEADS, HEAD_DIM, HEAD_DIM), F32),
                   jax.ShapeDtypeStruct((batch, 1, D_MODEL), F32),
                   jax.ShapeDtypeStruct((batch, 1, D_MODEL), F32)),
        scratch_shapes=[pltpu.VMEM((n_ctx + 16, D_MODEL), F32),
                        pltpu.VMEM((n_ctx, D_MODEL), F32),
                        pltpu.VMEM((n_ctx, D_MODEL), F32)],
        compiler_params=pltpu.CompilerParams(
            dimension_semantics=("arbitrary",), vmem_limit_bytes=VMEM_LIMIT),
        name="ctx_states",
    )(feat_c, feat_c, feat_c, feat_c, lb, conv_w, conv_b, wg, br, bi, lam)


GLA_ROWS_PER_ITER = 8


def _gla_kernel(*refs, reverse, rows, finish):
    if finish:
        (zq_ref, zf_ref, v_ref, lb_ref, s0_ref, other_ref, o_ref,
         st_ref, sn_ref, qd_ref, kit_ref, ke_ref, dec_ref) = refs
    else:
        zq_ref, zf_ref, v_ref, lb_ref, s0_ref, o_ref, st_ref, sn_ref, qd_ref, kit_ref, ke_ref, dec_ref = refs

    @pl.when(pl.program_id(1) == 0)
    def _():
        st_ref[...] = s0_ref[...]
        for h in range(N_HEADS):
            sn_ref[h] = s0_ref[h].astype(BF16).T

    ri = lax.broadcasted_iota(jnp.int32, (GRID_W, GRID_W), 0)
    ci = lax.broadcasted_iota(jnp.int32, (GRID_W, GRID_W), 1)
    allow = (ci >= ri) if reverse else (ci <= ri)
    lb_dir = 1 if reverse else 0

    def row_offset(n):
        n = jnp.minimum(n, rows - 1)
        return pl.multiple_of(((rows - 1 - n) if reverse else n) * GRID_W, GRID_W)

    def prepare(n, slot):
        off = row_offset(n)
        zq = zq_ref[pl.ds(off, GRID_W), :]
        hq = (0.5 * Q_SCALE) * zq
        q = hq * jnp.tanh(0.5 * zq) + hq
        lb = lb_ref[lb_dir:lb_dir + 1, :]
        f = (0.5 + 0.5 * lb) + (0.5 - 0.5 * lb) * jnp.tanh(0.5 * zf_ref[pl.ds(off, GRID_W), :])
        dg, dec = _chunk_cumprod(f, reverse)
        ki = (1.0 - f) / dg
        qd = (q * dg).astype(BF16)
        ke = (ki * dec).astype(BF16)
        kit_ref[slot] = ki.astype(BF16).T
        dec_ref[slot] = dec
        for h in range(N_HEADS):
            sl = slice(h * HEAD_DIM, (h + 1) * HEAD_DIM)
            qd_ref[slot, h] = qd[:, sl]
            ke_ref[slot, h] = ke[:, sl]

    def contract(n, slot):
        off = row_offset(n)
        for h in range(N_HEADS):
            sl = slice(h * HEAD_DIM, (h + 1) * HEAD_DIM)
            qd = qd_ref[slot, h]
            v = v_ref[h, pl.ds(off, GRID_W), :]
            p = jnp.where(allow, _dot(qd, kit_ref[slot, sl, :]), 0.0).astype(BF16)
            o = _dot(p, v) + _dot(qd, sn_ref[h])
            st = st_ref[h] * dec_ref[slot, :, sl] + _dot_tn(v, ke_ref[slot, h])
            st_ref[h] = st
            sn_ref[h] = st.astype(BF16).T
            if finish:
                o = o + other_ref[h, pl.ds(off, GRID_W), :].astype(F32)
            o_ref[h, pl.ds(off, GRID_W), :] = o.astype(o_ref.dtype)

    prepare(0, 0)

    def body(m, carry):
        n = GLA_ROWS_PER_ITER * m
        for i in range(GLA_ROWS_PER_ITER):
            contract(n + i, i % 2)
            prepare(n + i + 1, (i + 1) % 2)
        return carry

    lax.fori_loop(0, rows // GLA_ROWS_PER_ITER, body, 0)


def _gla_call(feat, feat_h, lb, f_col, s0, other=None, *, reverse, batch, seq, rows):
    tb = rows * GRID_W
    nrb = seq // tb
    finish = other is not None
    per_tile = feat.shape[2] // tb

    def row_block(b, i):
        return b * nrb + ((nrb - 1 - i) if reverse else i)

    def col(c):
        return pl.BlockSpec((None, None, tb, D_MODEL), lambda b, i, c=c: (
            row_block(b, i) // per_tile, c, row_block(b, i) % per_tile, 0))

    tok = pl.BlockSpec((N_HEADS, tb, HEAD_DIM), lambda b, i: (0, row_block(b, i), 0))
    v_spec = pl.BlockSpec((None, None, N_HEADS, tb, HEAD_DIM), lambda b, i: (
        row_block(b, i) // per_tile, H_V, 0, row_block(b, i) % per_tile, 0))
    in_specs = [col(F_Q), col(f_col), v_spec,
                pl.BlockSpec((2, D_MODEL), lambda b, i: (0, 0)),
                pl.BlockSpec((None, N_HEADS, HEAD_DIM, HEAD_DIM), lambda b, i: (b, 0, 0, 0))]
    args = [feat, feat, feat_h, lb, s0]
    if finish:
        in_specs += [tok]
        args += [other]
    return pl.pallas_call(
        functools.partial(_gla_kernel, reverse=reverse, rows=rows, finish=finish),
        grid=(batch, nrb),
        in_specs=in_specs,
        out_specs=tok,
        out_shape=jax.ShapeDtypeStruct((N_HEADS, batch * seq, HEAD_DIM), BF16),
        scratch_shapes=[pltpu.VMEM((N_HEADS, HEAD_DIM, HEAD_DIM), F32)]
        + [pltpu.VMEM((N_HEADS, HEAD_DIM, HEAD_DIM), BF16), pltpu.VMEM((2, N_HEADS, GRID_W, HEAD_DIM), BF16),
           pltpu.VMEM((2, D_MODEL, GRID_W), BF16), pltpu.VMEM((2, N_HEADS, GRID_W, HEAD_DIM), BF16),
           pltpu.VMEM((2, 1, D_MODEL), F32)],
        compiler_params=pltpu.CompilerParams(
            dimension_semantics=("arbitrary", "arbitrary"), vmem_limit_bytes=VMEM_LIMIT),
        name="gla_bwd" if reverse else "gla_fwd",
    )(*args)


def _lru_kernel(z5_ref, cw_ref, cb_ref, wg_ref, br_ref, bi_ref, lam_ref, h0f_ref, h0b_ref, o_ref,
                zp_ref, af_ref, bf_ref, ab_ref, bb_ref, cf_ref, cr_ref, *, n_rows, rows_per_step):
    w = GRID_W
    seq = n_rows * w
    tb = rows_per_step * w
    n_steps = n_rows // rows_per_step

    zp_ref[0:w, :] = jnp.zeros((w, BLOCK_DIM), F32)
    zp_ref[w + seq:, :] = jnp.zeros((2 * w, BLOCK_DIM), F32)
    tile = z5_ref.shape[1]
    for k in range(z5_ref.shape[0]):
        zp_ref[w + k * tile:w + (k + 1) * tile, :] = z5_ref[k]

    sp = [_softplus(-lam_ref[d:d + 1, :]) for d in (0, 1)]

    def gates(s, c):
        off = pl.multiple_of(s * tb, tb)
        xc = cb_ref[...] + zp_ref[pl.ds(off, tb), :] * cw_ref[0:1, :]
        for kk in range(1, N_CONV):
            xc = xc + zp_ref[pl.ds(off + kk * w, tb), :] * cw_ref[kk:kk + 1, :]
        g = _dot(xc.astype(BF16), wg_ref[...])
        for d, a_ref, b_ref in ((0, af_ref, bf_ref), (1, ab_ref, bb_ref)):
            c0 = 2 * d * BLOCK_DIM
            a, b = _lru_ab(xc, g[:, c0:c0 + BLOCK_DIM], g[:, c0 + BLOCK_DIM:c0 + 2 * BLOCK_DIM],
                           br_ref[d:d + 1, :], bi_ref[d:d + 1, :], sp[d])
            a_ref[pl.ds(off, tb), :] = a
            b_ref[pl.ds(off, tb), :] = b
        return c

    lax.fori_loop(0, n_steps, gates, 0)

    def scan(n, carry):
        hf, pf, hb, pb = carry
        off_f = pl.multiple_of(n * w, w)
        off_b = pl.multiple_of((n_rows - 1 - n) * w, w)
        a = af_ref[pl.ds(off_f, w), :]
        hf = a * hf + bf_ref[pl.ds(off_f, w), :]
        pf = pf * a
        bf_ref[pl.ds(off_f, w), :] = hf
        af_ref[pl.ds(off_f, w), :] = pf
        a = ab_ref[pl.ds(off_b, w), :]
        hb = a * hb + bb_ref[pl.ds(off_b, w), :]
        pb = pb * a
        bb_ref[pl.ds(off_b, w), :] = hb
        ab_ref[pl.ds(off_b, w), :] = pb
        return hf, pf, hb, pb

    zeros = jnp.zeros((w, BLOCK_DIM), F32)
    ones = jnp.ones((w, BLOCK_DIM), F32)
    lax.fori_loop(0, n_rows, scan, (zeros, ones, zeros, ones), unroll=4)

    last = (n_rows - 1) * w

    def carry(n, c):
        cf, cb = c
        cf_ref[pl.ds(n, 1), :] = cf
        cf = bf_ref[pl.ds(last + n, 1), :] + af_ref[pl.ds(last + n, 1), :] * cf
        col = w - 1 - n
        cr_ref[pl.ds(col, 1), :] = cb
        cb = bb_ref[pl.ds(col, 1), :] + ab_ref[pl.ds(col, 1), :] * cb
        return cf, cb

    lax.fori_loop(0, w, carry, (h0f_ref[...], h0b_ref[...]), unroll=8)

    def fix(s, c):
        off = pl.multiple_of(s * tb, tb)
        cf = jnp.concatenate([cf_ref[...]] * rows_per_step, axis=0)
        cr = jnp.concatenate([cr_ref[...]] * rows_per_step, axis=0)
        h = (bf_ref[pl.ds(off, tb), :] + af_ref[pl.ds(off, tb), :] * cf
             + bb_ref[pl.ds(off, tb), :] + ab_ref[pl.ds(off, tb), :] * cr)
        o_ref[pl.ds(off, tb), :] = h.astype(o_ref.dtype)
        return c

    lax.fori_loop(0, n_steps, fix, 0)


def _lru_call(feat, conv_w, conv_b, wg, br, bi, lam, h0f, h0b, *, batch, seq):
    n_rows = seq // GRID_W
    vec = lambda n: pl.BlockSpec((n, BLOCK_DIM), lambda b, c: (0, c))
    h0 = pl.BlockSpec((None, 1, BLOCK_DIM), lambda b, c: (b, 0, c))
    buf = pltpu.VMEM((seq, BLOCK_DIM), F32)
    n_i, n_col, tm, _ = feat.shape
    feat5 = feat.reshape(batch, n_i // batch, n_col, tm, D_MODEL)
    return pl.pallas_call(
        functools.partial(_lru_kernel, n_rows=n_rows, rows_per_step=8),
        grid=(batch, N_BLOCKS),
        in_specs=[pl.BlockSpec((None, n_i // batch, None, tm, BLOCK_DIM), lambda b, c: (b, 0, F_Z5, 0, c)),
                  vec(N_CONV), vec(1),
                  pl.BlockSpec((None, BLOCK_DIM, 4 * BLOCK_DIM), lambda b, c: (c, 0, 0)),
                  vec(2), vec(2), vec(2), h0, h0],
        out_specs=pl.BlockSpec((None, seq, BLOCK_DIM), lambda b, c: (c, b, 0)),
        out_shape=jax.ShapeDtypeStruct((N_BLOCKS, batch * seq, BLOCK_DIM), BF16),
        scratch_shapes=[pltpu.VMEM((seq + 3 * GRID_W, BLOCK_DIM), F32), buf, buf, buf, buf,
                        pltpu.VMEM((GRID_W, BLOCK_DIM), F32), pltpu.VMEM((GRID_W, BLOCK_DIM), F32)],
        compiler_params=pltpu.CompilerParams(
            dimension_semantics=("arbitrary", "arbitrary"), vmem_limit_bytes=VMEM_LIMIT),
        name="lru",
    )(feat5, conv_w, conv_b, wg, br, bi, lam, h0f, h0b)


def _merge_kernel(oa_ref, hx_ref, z4_ref, z6_ref, z7_ref, z8_ref, x_ref, mod_ref, ng_ref,
                  pa_ref, pb_ref, wo_ref, lg_ref, lbias_ref, o_ref):
    f32 = lambda ref: jnp.concatenate([ref[h] for h in range(N_HEADS)], axis=-1).astype(F32)
    o_b = (f32(hx_ref) * _silu(f32(z6_ref))).astype(BF16)
    y = _sigmoid(f32(z8_ref)) * _dot(o_b, pb_ref[...])
    y_a = None
    for pair in range(N_HEADS // 2):
        o_h = []
        for h in (2 * pair, 2 * pair + 1):
            t = oa_ref[h].astype(F32)
            ms = jnp.mean(t * t, axis=-1, keepdims=True)
            o_h.append((t * lax.rsqrt(ms + RMS_EPS) * ng_ref[...] * _silu(z4_ref[h].astype(F32))).astype(BF16))
        rows = slice(2 * pair * HEAD_DIM, 2 * (pair + 1) * HEAD_DIM)
        part = _dot(jnp.concatenate(o_h, axis=-1), pa_ref[rows, :])
        y_a = part if y_a is None else y_a + part
    y = y + _sigmoid(f32(z7_ref)) * y_a
    y = _dot(y.astype(BF16), wo_ref[...])
    t = DEEPNORM_ALPHA * x_ref[...] + mod_ref[2:3, :] * y
    mu = jnp.mean(t, axis=-1, keepdims=True)
    tc = t - mu
    var = jnp.mean(tc * tc, axis=-1, keepdims=True)
    o_ref[...] = tc * lax.rsqrt(var + LN_EPS) * lg_ref[...] + lbias_ref[...]


def _merge_call(oa, hx, feat_h, x2, mod3, ng, pa, pb, wo, lg, lbias, *, tm, tiles_per_batch):
    n_tok = x2.shape[0]
    tok = pl.BlockSpec((tm, D_MODEL), lambda i: (i, 0))
    heads = pl.BlockSpec((N_HEADS, tm, HEAD_DIM), lambda i: (0, i, 0))
    per_tile = feat_h.shape[3] // tm
    col = lambda j: pl.BlockSpec((None, None, N_HEADS, tm, HEAD_DIM),
                                 lambda i, j=j: (i // per_tile, j, 0, i % per_tile, 0))
    full = lambda shape: pl.BlockSpec(shape, lambda i: (0,) * len(shape), pipeline_mode=pl.Buffered(1))
    return pl.pallas_call(
        _merge_kernel,
        grid=(n_tok // tm,),
        in_specs=[heads, heads, col(H_G4), col(H_G6), col(H_M7), col(H_M8), tok,
                  pl.BlockSpec((None, 3, D_MODEL), lambda i: (i // tiles_per_batch, 0, 0)),
                  full((1, HEAD_DIM)), full((D_MODEL, D_MODEL)), full((D_MODEL, D_MODEL)),
                  full((D_MODEL, D_MODEL)), full((1, D_MODEL)), full((1, D_MODEL))],
        out_specs=tok,
        out_shape=jax.ShapeDtypeStruct((n_tok, D_MODEL), F32),
        compiler_params=pltpu.CompilerParams(
            dimension_semantics=("arbitrary",), vmem_limit_bytes=VMEM_LIMIT),
        name="merge",
    )(oa, hx, feat_h, feat_h, feat_h, feat_h, x2, mod3, ng, pa, pb, wo, lg, lbias)


def kernel(x, c, ctx, c_ctx, w_mod, b_mod, w_in, b_in, lb_logits, norm_a_g, conv_w, conv_b,
           w_r, b_r, w_i, b_i, lam, p_a, p_b, w_out, ln_g, ln_b):
    batch, seq, d = x.shape
    n_ctx = ctx.shape[1]
    assert d == D_MODEL and seq % GRID_W == 0 and w_in.shape[0] == 1

    w_bf = w_in[0].astype(BF16)
    b2 = b_in[0][None, :]
    wg = (0.5 * jnp.concatenate([w_r[0, 0], w_i[0, 0], w_r[0, 1], w_i[0, 1]], axis=-1)).astype(BF16)
    assert batch < 8
    mod, lb = _mod_call(c, c_ctx[None, :], w_mod[0], b_mod[0][None, :], lb_logits)
    mod3 = mod.reshape(8, 3, D_MODEL)

    x2 = x.reshape(batch * seq, D_MODEL)
    tm = 1024
    feat = _inproj_call(x2, mod3, w_bf, b2, LATENT_F32, out_dtype=F32, tm=tm, tiles_per_mod=seq // tm)
    feat_h = _inproj_call(x2, mod3, w_bf, b2, LATENT_BF16, out_dtype=BF16, tm=tm, tiles_per_mod=seq // tm,
                          head_major=True)
    feat_c = _inproj_call(ctx.reshape(batch * n_ctx, D_MODEL), mod3[batch:batch + 1], w_bf, b2, CTX_F32,
                          out_dtype=F32, tm=n_ctx, tiles_per_mod=batch)

    cw, cb = conv_w[0], conv_b[0][None, :]
    s0f, s0b, h0f, h0b = _ctx_call(feat_c, lb, cw, cb, wg, b_r[0], b_i[0], lam[0], batch=batch, n_ctx=n_ctx)

    o_b = _gla_call(feat, feat_h, lb, F_F1, s0b, reverse=True, batch=batch, seq=seq, rows=16)
    oa = _gla_call(feat, feat_h, lb, F_F0, s0f, o_b, reverse=False, batch=batch, seq=seq, rows=16)
    hx = _lru_call(feat, cw, cb, wg, b_r[0], b_i[0], lam[0], h0f, h0b, batch=batch, seq=seq)

    out = _merge_call(oa, hx, feat_h, x2, mod3, norm_a_g[0][None, :],
                      p_a[0].astype(BF16), p_b[0].astype(BF16), w_out[0].astype(BF16),
                      ln_g[0][None, :], ln_b[0][None, :], tm=512, tiles_per_batch=seq // 512)
    return out.reshape(batch, seq, D_MODEL)
```

```python
import functools

import jax
import jax.numpy as jnp
from jax import lax
from jax.experimental import pallas as pl
from jax.experimental.pallas import tpu as pltpu

F32 = jnp.float32
BF16 = jnp.bfloat16

D_MODEL = 1024
GRID_W = 64
HEAD_DIM = 128
N_HEADS = D_MODEL // HEAD_DIM
N_BLOCKS = 8
BLOCK_DIM = D_MODEL // N_BLOCKS
N_CONV = 4
RG_C = 8.0
LN_EPS = 1e-5
RMS_EPS = 1e-6
DEEPNORM_ALPHA = 2.0 ** 0.25
Q_SCALE = HEAD_DIM ** -0.5

V7X_VMEM_BYTES = 64 * 1024 * 1024
VMEM_LIMIT = V7X_VMEM_BYTES - 8 * 1024 * 1024
SUBLANES = 8

INPROJ_TM = 1024
GLA_ROWS = 16
LRU_ROWS_PER_STEP = 8
MERGE_TM = 512

LATENT_F32 = (0, 1, 2, 5)
LATENT_BF16 = (3, 4, 6, 7, 8)
CTX_F32 = (1, 2, 3, 5)
F_Q, F_F0, F_F1, F_Z5 = 0, 1, 2, 3
H_V, H_G4, H_G6, H_M7, H_M8 = 0, 1, 2, 3, 4


def _sigmoid(x):
    return 0.5 * jnp.tanh(0.5 * x) + 0.5


def _silu(x):
    h = 0.5 * x
    return h * jnp.tanh(h) + h


def _log_forget(z, lb):
    return jnp.log((0.5 + 0.5 * lb) + (0.5 - 0.5 * lb) * jnp.tanh(0.5 * z))


def _dot(a, b):
    return jnp.dot(a, b, preferred_element_type=F32)


def _dot_tn(a, b):
    return lax.dot_general(a, b, (((0,), (0,)), ((), ())), preferred_element_type=F32)


def _chunk_cumprod(x, reverse):
    n, c = x.shape
    nb = n // 8
    y = x.reshape(nb, 8, c)
    sub = lax.broadcasted_iota(jnp.int32, (nb, 8, c), 1)
    for s in (1, 2, 4):
        if reverse:
            y = y * jnp.where(sub < 8 - s, pltpu.roll(y, 8 - s, axis=1), 1.0)
        else:
            y = y * jnp.where(sub >= s, pltpu.roll(y, s, axis=1), 1.0)
    offs = [None] * nb
    acc = jnp.ones((1, c), F32)
    for b in (reversed(range(nb)) if reverse else range(nb)):
        offs[b] = acc
        acc = acc * (y[b, 0:1, :] if reverse else y[b, 7:8, :])
    return (y * jnp.stack(offs)).reshape(n, c), acc


def _cumsum_rows(mask_bf16, x):
    hi = x.astype(BF16)
    r1 = x - hi.astype(F32)
    mid = r1.astype(BF16)
    lo = (r1 - mid.astype(F32)).astype(BF16)
    return _dot(mask_bf16, hi) + _dot(mask_bf16, mid) + _dot(mask_bf16, lo)


def _mod_kernel(c_ref, cc_ref, w_ref, b_ref, lbl_ref, mod_ref, lb_ref, rows_ref):
    n = c_ref.shape[0]
    rows_ref[...] = jnp.zeros(rows_ref.shape, F32)
    rows_ref[0:n, :] = c_ref[...]
    rows_ref[n:n + 1, :] = cc_ref[...]
    mod_ref[...] = _dot(_silu(rows_ref[...]), w_ref[...]) + b_ref[...]
    l = lbl_ref[...]
    e = jnp.exp(l - jnp.max(l, axis=0, keepdims=True))
    lb_ref[...] = e[0] / jnp.sum(e, axis=0)


def _mod_call(c, c_ctx, w_mod, b_mod, lb_logits):
    return pl.pallas_call(
        _mod_kernel,
        out_shape=(jax.ShapeDtypeStruct((SUBLANES, 3 * D_MODEL), F32),
                   jax.ShapeDtypeStruct((2, D_MODEL), F32)),
        scratch_shapes=[pltpu.VMEM((SUBLANES, D_MODEL), F32)],
        compiler_params=pltpu.CompilerParams(vmem_limit_bytes=VMEM_LIMIT),
        name="mod",
    )(c, c_ctx, w_mod, b_mod, lb_logits)


def _inproj_kernel(x_ref, mod_ref, *refs, head_major):
    n_col = (len(refs) - 1) // 2
    w_refs, b_refs, o_ref = refs[:n_col], refs[n_col:2 * n_col], refs[-1]
    u = (x_ref[...] * (1.0 + mod_ref[1:2, :]) + mod_ref[0:1, :]).astype(BF16)
    for j in range(n_col):
        z = (_dot(u, w_refs[j][...]) + b_refs[j][...]).astype(o_ref.dtype)
        if head_major:
            for h in range(N_HEADS):
                o_ref[j, h] = z[:, h * HEAD_DIM:(h + 1) * HEAD_DIM]
        else:
            o_ref[j] = z


def _inproj_call(x2, mod3, w, b, cols, *, out_dtype, tm, tiles_per_mod, head_major=False):
    n_tok = x2.shape[0]
    n_col = len(cols)
    tile = (N_HEADS, tm, HEAD_DIM) if head_major else (tm, D_MODEL)
    col_block = lambda rows: [pl.BlockSpec((rows, D_MODEL), lambda i, c=c: (0, c), pipeline_mode=pl.Buffered(1))
                              for c in cols]
    return pl.pallas_call(
        functools.partial(_inproj_kernel, head_major=head_major),
        grid=(n_tok // tm,),
        in_specs=[
            pl.BlockSpec((tm, D_MODEL), lambda i: (i, 0)),
            pl.BlockSpec((None, 3, D_MODEL), lambda i: (i // tiles_per_mod, 0, 0)),
            *col_block(D_MODEL),
            *col_block(1),
        ],
        out_specs=pl.BlockSpec((None, n_col) + tile, lambda i: (i, 0) + (0,) * len(tile)),
        out_shape=jax.ShapeDtypeStruct((n_tok // tm, n_col) + tile, out_dtype),
        compiler_params=pltpu.CompilerParams(
            dimension_semantics=("arbitrary",), vmem_limit_bytes=VMEM_LIMIT),
        name="inproj",
    )(x2, mod3, *([w] * n_col), *([b] * n_col))


LOG2_E = 1.4426950408889634


def _lru_ab(xc, h_r, h_i, br, bi, sp):
    k = (-0.5 * RG_C * LOG2_E) * sp
    a = jnp.exp2(k * jnp.tanh(h_r + 0.5 * br) + k)
    i = 0.5 * jnp.tanh(h_i + 0.5 * bi) + 0.5
    y = 1.0 - a * a
    mult = jnp.where(y > 0.0, y * lax.rsqrt(y), 0.0)
    return a, mult * (i * xc)


def _softplus(y):
    return jnp.maximum(y, 0.0) + jnp.log(1.0 + jnp.exp(-jnp.abs(y)))


def _ctx_kernel(zf_ref, zb_ref, v_ref, z5_ref, lb_ref, cw_ref, cb_ref, wg_ref, br_ref, bi_ref, lam_ref,
                sf_ref, sb_ref, hf_ref, hb_ref, zp_ref, a_ref, b_ref):
    n = zf_ref.shape[0]
    ri = lax.broadcasted_iota(jnp.int32, (n, n), 0)
    ci = lax.broadcasted_iota(jnp.int32, (n, n), 1)
    tril = (ci <= ri).astype(F32).astype(BF16)
    v = v_ref[...].astype(BF16)

    lf = _log_forget(zf_ref[...], lb_ref[0:1, :])
    g = _cumsum_rows(tril, lf)
    ke = ((1.0 - jnp.exp(lf)) * jnp.exp(g[n - 1:n, :] - g)).astype(BF16)
    for h in range(N_HEADS):
        sl = slice(h * HEAD_DIM, (h + 1) * HEAD_DIM)
        sf_ref[h] = _dot_tn(v[:, sl], ke[:, sl])
    lf = _log_forget(zb_ref[...], lb_ref[1:2, :])
    g = _cumsum_rows(tril, lf)
    ke = ((1.0 - jnp.exp(lf)) * jnp.exp(g - lf)).astype(BF16)
    for h in range(N_HEADS):
        sl = slice(h * HEAD_DIM, (h + 1) * HEAD_DIM)
        sb_ref[h] = _dot_tn(v[:, sl], ke[:, sl])

    pad = SUBLANES
    zp_ref[0:pad, :] = jnp.zeros((pad, D_MODEL), F32)
    zp_ref[pad + n:2 * pad + n, :] = jnp.zeros((pad, D_MODEL), F32)
    zp_ref[pad:pad + n, :] = z5_ref[...]
    first = pad - (N_CONV - 1) // 2
    xc = cb_ref[...] + zp_ref[first:first + n, :] * cw_ref[0:1, :]
    for kk in range(1, N_CONV):
        xc = xc + zp_ref[first + kk:first + kk + n, :] * cw_ref[kk:kk + 1, :]

    for d, h_ref in ((0, hf_ref), (1, hb_ref)):
        sp = _softplus(-lam_ref[d:d + 1, :])
        for blk in range(N_BLOCKS):
            sl = slice(blk * BLOCK_DIM, (blk + 1) * BLOCK_DIM)
            g = _dot(xc[:, sl].astype(BF16), wg_ref[blk, :, 2 * d * BLOCK_DIM:2 * (d + 1) * BLOCK_DIM])
            a, b = _lru_ab(xc[:, sl], g[:, :BLOCK_DIM], g[:, BLOCK_DIM:],
                           br_ref[d:d + 1, sl], bi_ref[d:d + 1, sl], sp[:, sl])
            a_ref[:, sl] = a
            b_ref[:, sl] = b

        def step(t, h, d=d):
            tt = (n - 1 - t) if d == 1 else t
            return a_ref[pl.ds(tt, 1), :] * h + b_ref[pl.ds(tt, 1), :]

        h_ref[...] = lax.fori_loop(0, n, step, jnp.zeros((1, D_MODEL), F32), unroll=8)


def _ctx_call(feat_c, lb, conv_w, conv_b, wg, br, bi, lam, *, batch, n_ctx):
    feat = lambda j: pl.BlockSpec((None, None, n_ctx, D_MODEL), lambda b, j=j: (b, j, 0, 0))
    full = lambda shape: pl.BlockSpec(shape, lambda b: (0,) * len(shape))
    state = pl.BlockSpec((None, N_HEADS, HEAD_DIM, HEAD_DIM), lambda b: (b, 0, 0, 0))
    hvec = pl.BlockSpec((None, 1, D_MODEL), lambda b: (b, 0, 0))
    return pl.pallas_call(
        _ctx_kernel,
        grid=(batch,),
        in_specs=[feat(0), feat(1), feat(2), feat(3), full((2, D_MODEL)),
                  full((N_CONV, D_MODEL)), full((1, D_MODEL)),
                  full((N_BLOCKS, BLOCK_DIM, 4 * BLOCK_DIM)),
                  full((2, D_MODEL)), full((2, D_MODEL)), full((2, D_MODEL))],
        out_specs=(state, state, hvec, hvec),
        out_shape=(jax.ShapeDtypeStruct((batch, N_HEADS, HEAD_DIM, HEAD_DIM), F32),
                   jax.ShapeDtypeStruct((batch, N_HEADS, HEAD_DIM, HEAD_DIM), F32),
                   jax.ShapeDtypeStruct((batch, 1, D_MODEL), F32),
                   jax.ShapeDtypeStruct((batch, 1, D_MODEL), F32)),
        scratch_shapes=[pltpu.VMEM((n_ctx + 2 * SUBLANES, D_MODEL), F32),
                        pltpu.VMEM((n_ctx, D_MODEL), F32),
                        pltpu.VMEM((n_ctx, D_MODEL), F32)],
        compiler_params=pltpu.CompilerParams(
            dimension_semantics=("arbitrary",), vmem_limit_bytes=VMEM_LIMIT),
        name="ctx_states",
    )(feat_c, feat_c, feat_c, feat_c, lb, conv_w, conv_b, wg, br, bi, lam)


GLA_ROWS_PER_ITER = 8


def _gla_kernel(*refs, reverse, rows, finish):
    if finish:
        (zq_ref, zf_ref, v_ref, lb_ref, s0_ref, other_ref, o_ref,
         st_ref, sn_ref, qd_ref, kit_ref, ke_ref, dec_ref) = refs
    else:
        zq_ref, zf_ref, v_ref, lb_ref, s0_ref, o_ref, st_ref, sn_ref, qd_ref, kit_ref, ke_ref, dec_ref = refs

    @pl.when(pl.program_id(1) == 0)
    def _():
        st_ref[...] = s0_ref[...]
        for h in range(N_HEADS):
            sn_ref[h] = s0_ref[h].astype(BF16).T

    ri = lax.broadcasted_iota(jnp.int32, (GRID_W, GRID_W), 0)
    ci = lax.broadcasted_iota(jnp.int32, (GRID_W, GRID_W), 1)
    allow = (ci >= ri) if reverse else (ci <= ri)
    lb_dir = 1 if reverse else 0

    def row_offset(n):
        n = jnp.minimum(n, rows - 1)
        return pl.multiple_of(((rows - 1 - n) if reverse else n) * GRID_W, GRID_W)

    def prepare(n, slot):
        off = row_offset(n)
        zq = zq_ref[pl.ds(off, GRID_W), :]
        hq = (0.5 * Q_SCALE) * zq
        q = hq * jnp.tanh(0.5 * zq) + hq
        lb = lb_ref[lb_dir:lb_dir + 1, :]
        f = (0.5 + 0.5 * lb) + (0.5 - 0.5 * lb) * jnp.tanh(0.5 * zf_ref[pl.ds(off, GRID_W), :])
        dg, dec = _chunk_cumprod(f, reverse)
        ki = (1.0 - f) / dg
        qd = (q * dg).astype(BF16)
        ke = (ki * dec).astype(BF16)
        kit_ref[slot] = ki.astype(BF16).T
        dec_ref[slot] = dec
        for h in range(N_HEADS):
            sl = slice(h * HEAD_DIM, (h + 1) * HEAD_DIM)
            qd_ref[slot, h] = qd[:, sl]
            ke_ref[slot, h] = ke[:, sl]

    def contract(n, slot):
        off = row_offset(n)
        for h in range(N_HEADS):
            sl = slice(h * HEAD_DIM, (h + 1) * HEAD_DIM)
            qd = qd_ref[slot, h]
            v = v_ref[h, pl.ds(off, GRID_W), :]
            p = jnp.where(allow, _dot(qd, kit_ref[slot, sl, :]), 0.0).astype(BF16)
            o = _dot(p, v) + _dot(qd, sn_ref[h])
            st = st_ref[h] * dec_ref[slot, :, sl] + _dot_tn(v, ke_ref[slot, h])
            st_ref[h] = st
            sn_ref[h] = st.astype(BF16).T
            if finish:
                o = o + other_ref[h, pl.ds(off, GRID_W), :].astype(F32)
            o_ref[h, pl.ds(off, GRID_W), :] = o.astype(o_ref.dtype)

    prepare(0, 0)

    def body(m, carry):
        n = GLA_ROWS_PER_ITER * m
        for i in range(GLA_ROWS_PER_ITER):
            contract(n + i, i % 2)
            prepare(n + i + 1, (i + 1) % 2)
        return carry

    lax.fori_loop(0, rows // GLA_ROWS_PER_ITER, body, 0)


def _gla_call(feat, feat_h, lb, f_col, s0, other=None, *, reverse, batch, seq, rows):
    tb = rows * GRID_W
    nrb = seq // tb
    finish = other is not None
    per_tile = feat.shape[2] // tb

    def row_block(b, i):
        return b * nrb + ((nrb - 1 - i) if reverse else i)

    def col(c):
        return pl.BlockSpec((None, None, tb, D_MODEL), lambda b, i, c=c: (
            row_block(b, i) // per_tile, c, row_block(b, i) % per_tile, 0))

    tok = pl.BlockSpec((N_HEADS, tb, HEAD_DIM), lambda b, i: (0, row_block(b, i), 0))
    v_spec = pl.BlockSpec((None, None, N_HEADS, tb, HEAD_DIM), lambda b, i: (
        row_block(b, i) // per_tile, H_V, 0, row_block(b, i) % per_tile, 0))
    in_specs = [col(F_Q), col(f_col), v_spec,
                pl.BlockSpec((2, D_MODEL), lambda b, i: (0, 0)),
                pl.BlockSpec((None, N_HEADS, HEAD_DIM, HEAD_DIM), lambda b, i: (b, 0, 0, 0))]
    args = [feat, feat, feat_h, lb, s0]
    if finish:
        in_specs += [tok]
        args += [other]
    return pl.pallas_call(
        functools.partial(_gla_kernel, reverse=reverse, rows=rows, finish=finish),
        grid=(batch, nrb),
        in_specs=in_specs,
        out_specs=tok,
        out_shape=jax.ShapeDtypeStruct((N_HEADS, batch * seq, HEAD_DIM), BF16),
        scratch_shapes=[pltpu.VMEM((N_HEADS, HEAD_DIM, HEAD_DIM), F32)]
        + [pltpu.VMEM((N_HEADS, HEAD_DIM, HEAD_DIM), BF16), pltpu.VMEM((2, N_HEADS, GRID_W, HEAD_DIM), BF16),
           pltpu.VMEM((2, D_MODEL, GRID_W), BF16), pltpu.VMEM((2, N_HEADS, GRID_W, HEAD_DIM), BF16),
           pltpu.VMEM((2, 1, D_MODEL), F32)],
        compiler_params=pltpu.CompilerParams(
            dimension_semantics=("arbitrary", "arbitrary"), vmem_limit_bytes=VMEM_LIMIT),
        name="gla_bwd" if reverse else "gla_fwd",
    )(*args)


def _lru_kernel(z5_ref, cw_ref, cb_ref, wg_ref, br_ref, bi_ref, lam_ref, h0f_ref, h0b_ref, o_ref,
                zp_ref, af_ref, bf_ref, ab_ref, bb_ref, cf_ref, cr_ref, *, n_rows, rows_per_step):
    w = GRID_W
    seq = n_rows * w
    tb = rows_per_step * w
    n_steps = n_rows // rows_per_step

    zp_ref[0:w, :] = jnp.zeros((w, BLOCK_DIM), F32)
    zp_ref[w + seq:, :] = jnp.zeros((2 * w, BLOCK_DIM), F32)
    tile = z5_ref.shape[1]
    for k in range(z5_ref.shape[0]):
        zp_ref[w + k * tile:w + (k + 1) * tile, :] = z5_ref[k]

    sp = [_softplus(-lam_ref[d:d + 1, :]) for d in (0, 1)]

    def gates(s, c):
        off = pl.multiple_of(s * tb, tb)
        xc = cb_ref[...] + zp_ref[pl.ds(off, tb), :] * cw_ref[0:1, :]
        for kk in range(1, N_CONV):
            xc = xc + zp_ref[pl.ds(off + kk * w, tb), :] * cw_ref[kk:kk + 1, :]
        g = _dot(xc.astype(BF16), wg_ref[...])
        for d, a_ref, b_ref in ((0, af_ref, bf_ref), (1, ab_ref, bb_ref)):
            c0 = 2 * d * BLOCK_DIM
            a, b = _lru_ab(xc, g[:, c0:c0 + BLOCK_DIM], g[:, c0 + BLOCK_DIM:c0 + 2 * BLOCK_DIM],
                           br_ref[d:d + 1, :], bi_ref[d:d + 1, :], sp[d])
            a_ref[pl.ds(off, tb), :] = a
            b_ref[pl.ds(off, tb), :] = b
        return c

    lax.fori_loop(0, n_steps, gates, 0, unroll=2)

    def scan(n, carry):
        hf, pf, hb, pb = carry
        off_f = pl.multiple_of(n * w, w)
        off_b = pl.multiple_of((n_rows - 1 - n) * w, w)
        a = af_ref[pl.ds(off_f, w), :]
        hf = a * hf + bf_ref[pl.ds(off_f, w), :]
        pf = pf * a
        bf_ref[pl.ds(off_f, w), :] = hf
        af_ref[pl.ds(off_f, w), :] = pf
        a = ab_ref[pl.ds(off_b, w), :]
        hb = a * hb + bb_ref[pl.ds(off_b, w), :]
        pb = pb * a
        bb_ref[pl.ds(off_b, w), :] = hb
        ab_ref[pl.ds(off_b, w), :] = pb
        return hf, pf, hb, pb

    zeros = jnp.zeros((w, BLOCK_DIM), F32)
    ones = jnp.ones((w, BLOCK_DIM), F32)
    lax.fori_loop(0, n_rows, scan, (zeros, ones, zeros, ones), unroll=8)

    last = (n_rows - 1) * w

    def carry(n, c):
        cf, cb = c
        cf_ref[pl.ds(n, 1), :] = cf
        cf = bf_ref[pl.ds(last + n, 1), :] + af_ref[pl.ds(last + n, 1), :] * cf
        col = w - 1 - n
        cr_ref[pl.ds(col, 1), :] = cb
        cb = bb_ref[pl.ds(col, 1), :] + ab_ref[pl.ds(col, 1), :] * cb
        return cf, cb

    lax.fori_loop(0, w, carry, (h0f_ref[...], h0b_ref[...]), unroll=8)

    def fix(s, c):
        off = pl.multiple_of(s * tb, tb)
        cf = jnp.concatenate([cf_ref[...]] * rows_per_step, axis=0)
        cr = jnp.concatenate([cr_ref[...]] * rows_per_step, axis=0)
        h = (bf_ref[pl.ds(off, tb), :] + af_ref[pl.ds(off, tb), :] * cf
             + bb_ref[pl.ds(off, tb), :] + ab_ref[pl.ds(off, tb), :] * cr)
        o_ref[pl.ds(off, tb), :] = h.astype(o_ref.dtype)
        return c

    lax.fori_loop(0, n_steps, fix, 0, unroll=2)


def _lru_call(feat, conv_w, conv_b, wg, br, bi, lam, h0f, h0b, *, batch, seq):
    n_rows = seq // GRID_W
    vec = lambda n: pl.BlockSpec((n, BLOCK_DIM), lambda b, c: (0, c))
    h0 = pl.BlockSpec((None, 1, BLOCK_DIM), lambda b, c: (b, 0, c))
    buf = pltpu.VMEM((seq, BLOCK_DIM), F32)
    n_i, n_col, tm, _ = feat.shape
    feat5 = feat.reshape(batch, n_i // batch, n_col, tm, D_MODEL)
    return pl.pallas_call(
        functools.partial(_lru_kernel, n_rows=n_rows, rows_per_step=LRU_ROWS_PER_STEP),
        grid=(batch, N_BLOCKS),
        in_specs=[pl.BlockSpec((None, n_i // batch, None, tm, BLOCK_DIM), lambda b, c: (b, 0, F_Z5, 0, c)),
                  vec(N_CONV), vec(1),
                  pl.BlockSpec((None, BLOCK_DIM, 4 * BLOCK_DIM), lambda b, c: (c, 0, 0)),
                  vec(2), vec(2), vec(2), h0, h0],
        out_specs=pl.BlockSpec((None, seq, BLOCK_DIM), lambda b, c: (c, b, 0)),
        out_shape=jax.ShapeDtypeStruct((N_BLOCKS, batch * seq, BLOCK_DIM), BF16),
        scratch_shapes=[pltpu.VMEM((seq + 3 * GRID_W, BLOCK_DIM), F32), buf, buf, buf, buf,
                        pltpu.VMEM((GRID_W, BLOCK_DIM), F32), pltpu.VMEM((GRID_W, BLOCK_DIM), F32)],
        compiler_params=pltpu.CompilerParams(
            dimension_semantics=("arbitrary", "arbitrary"), vmem_limit_bytes=VMEM_LIMIT),
        name="lru",
    )(feat5, conv_w, conv_b, wg, br, bi, lam, h0f, h0b)


def _merge_kernel(oa_ref, hx_ref, z4_ref, z6_ref, z7_ref, z8_ref, x_ref, mod_ref, ng_ref,
                  pa_ref, pb_ref, wo_ref, lg_ref, lbias_ref, o_ref):
    f32 = lambda ref: jnp.concatenate([ref[h] for h in range(N_HEADS)], axis=-1).astype(F32)
    o_b = (f32(hx_ref) * _silu(f32(z6_ref))).astype(BF16)
    y = _sigmoid(f32(z8_ref)) * _dot(o_b, pb_ref[...])
    y_a = None
    for pair in range(N_HEADS // 2):
        o_h = []
        for h in (2 * pair, 2 * pair + 1):
            t = oa_ref[h].astype(F32)
            ms = jnp.mean(t * t, axis=-1, keepdims=True)
            o_h.append((t * lax.rsqrt(ms + RMS_EPS) * ng_ref[...] * _silu(z4_ref[h].astype(F32))).astype(BF16))
        rows = slice(2 * pair * HEAD_DIM, 2 * (pair + 1) * HEAD_DIM)
        part = _dot(jnp.concatenate(o_h, axis=-1), pa_ref[rows, :])
        y_a = part if y_a is None else y_a + part
    y = y + _sigmoid(f32(z7_ref)) * y_a
    y = _dot(y.astype(BF16), wo_ref[...])
    t = DEEPNORM_ALPHA * x_ref[...] + mod_ref[2:3, :] * y
    mu = jnp.mean(t, axis=-1, keepdims=True)
    tc = t - mu
    var = jnp.mean(tc * tc, axis=-1, keepdims=True)
    o_ref[...] = tc * lax.rsqrt(var + LN_EPS) * lg_ref[...] + lbias_ref[...]


def _merge_call(oa, hx, feat_h, x2, mod3, ng, pa, pb, wo, lg, lbias, *, tm, tiles_per_batch):
    n_tok = x2.shape[0]
    tok = pl.BlockSpec((tm, D_MODEL), lambda i: (i, 0))
    heads = pl.BlockSpec((N_HEADS, tm, HEAD_DIM), lambda i: (0, i, 0))
    per_tile = feat_h.shape[3] // tm
    col = lambda j: pl.BlockSpec((None, None, N_HEADS, tm, HEAD_DIM),
                                 lambda i, j=j: (i // per_tile, j, 0, i % per_tile, 0))
    full = lambda shape: pl.BlockSpec(shape, lambda i: (0,) * len(shape), pipeline_mode=pl.Buffered(1))
    return pl.pallas_call(
        _merge_kernel,
        grid=(n_tok // tm,),
        in_specs=[heads, heads, col(H_G4), col(H_G6), col(H_M7), col(H_M8), tok,
                  pl.BlockSpec((None, 3, D_MODEL), lambda i: (i // tiles_per_batch, 0, 0)),
                  full((1, HEAD_DIM)), full((D_MODEL, D_MODEL)), full((D_MODEL, D_MODEL)),
                  full((D_MODEL, D_MODEL)), full((1, D_MODEL)), full((1, D_MODEL))],
        out_specs=tok,
        out_shape=jax.ShapeDtypeStruct((n_tok, D_MODEL), F32),
        compiler_params=pltpu.CompilerParams(
            dimension_semantics=("arbitrary",), vmem_limit_bytes=VMEM_LIMIT),
        name="merge",
    )(oa, hx, feat_h, feat_h, feat_h, feat_h, x2, mod3, ng, pa, pb, wo, lg, lbias)


def kernel(x, c, ctx, c_ctx, w_mod, b_mod, w_in, b_in, lb_logits, norm_a_g, conv_w, conv_b,
           w_r, b_r, w_i, b_i, lam, p_a, p_b, w_out, ln_g, ln_b):
    batch, seq, d = x.shape
    n_ctx = ctx.shape[1]
    assert d == D_MODEL and seq % GRID_W == 0 and w_in.shape[0] == 1

    w_bf = w_in[0].astype(BF16)
    b2 = b_in[0][None, :]
    wg = (0.5 * jnp.concatenate([w_r[0, 0], w_i[0, 0], w_r[0, 1], w_i[0, 1]], axis=-1)).astype(BF16)
    assert batch < SUBLANES
    mod, lb = _mod_call(c, c_ctx[None, :], w_mod[0], b_mod[0][None, :], lb_logits)
    mod3 = mod.reshape(SUBLANES, 3, D_MODEL)

    x2 = x.reshape(batch * seq, D_MODEL)
    tm = INPROJ_TM
    feat = _inproj_call(x2, mod3, w_bf, b2, LATENT_F32, out_dtype=F32, tm=tm, tiles_per_mod=seq // tm)
    feat_h = _inproj_call(x2, mod3, w_bf, b2, LATENT_BF16, out_dtype=BF16, tm=tm, tiles_per_mod=seq // tm,
                          head_major=True)
    feat_c = _inproj_call(ctx.reshape(batch * n_ctx, D_MODEL), mod3[batch:batch + 1], w_bf, b2, CTX_F32,
                          out_dtype=F32, tm=n_ctx, tiles_per_mod=batch)

    cw, cb = conv_w[0], conv_b[0][None, :]
    s0f, s0b, h0f, h0b = _ctx_call(feat_c, lb, cw, cb, wg, b_r[0], b_i[0], lam[0], batch=batch, n_ctx=n_ctx)

    o_b = _gla_call(feat, feat_h, lb, F_F1, s0b, reverse=True, batch=batch, seq=seq, rows=GLA_ROWS)
    oa = _gla_call(feat, feat_h, lb, F_F0, s0f, o_b, reverse=False, batch=batch, seq=seq, rows=GLA_ROWS)
    hx = _lru_call(feat, cw, cb, wg, b_r[0], b_i[0], lam[0], h0f, h0b, batch=batch, seq=seq)

    out = _merge_call(oa, hx, feat_h, x2, mod3, norm_a_g[0][None, :],
                      p_a[0].astype(BF16), p_b[0].astype(BF16), w_out[0].astype(BF16),
                      ln_g[0][None, :], ln_b[0][None, :], tm=MERGE_TM, tiles_per_batch=seq // MERGE_TM)
    return out.reshape(batch, seq, D_MODEL)
```

```python
import functools

import jax
import jax.numpy as jnp
from jax import lax
from jax.experimental import pallas as pl
from jax.experimental.pallas import tpu as pltpu

F32 = jnp.float32
BF16 = jnp.bfloat16

D_MODEL = 1024
GRID_W = 64
HEAD_DIM = 128
N_HEADS = D_MODEL // HEAD_DIM
N_BLOCKS = 8
BLOCK_DIM = D_MODEL // N_BLOCKS
N_CONV = 4
RG_C = 8.0
LN_EPS = 1e-5
RMS_EPS = 1e-6
DEEPNORM_ALPHA = 2.0 ** 0.25
Q_SCALE = HEAD_DIM ** -0.5

V7X_VMEM_BYTES = 64 * 1024 * 1024
VMEM_LIMIT = V7X_VMEM_BYTES - 8 * 1024 * 1024
SUBLANES = 8

INPROJ_TM = 1024
GLA_ROWS = 16
LRU_ROWS_PER_STEP = 8
MERGE_TM = 512

LATENT_F32 = (0, 1, 2, 5)
LATENT_BF16 = (3, 4, 6, 7, 8)
CTX_F32 = (1, 2, 3, 5)
F_Q, F_F0, F_F1, F_Z5 = 0, 1, 2, 3
H_V, H_G4, H_G6, H_M7, H_M8 = 0, 1, 2, 3, 4


def _sigmoid(x):
    return 0.5 * jnp.tanh(0.5 * x) + 0.5


def _silu(x):
    h = 0.5 * x
    return h * jnp.tanh(h) + h


def _log_forget(z, lb):
    return jnp.log((0.5 + 0.5 * lb) + (0.5 - 0.5 * lb) * jnp.tanh(0.5 * z))


def _dot(a, b):
    return jnp.dot(a, b, preferred_element_type=F32)


def _dot_tn(a, b):
    return lax.dot_general(a, b, (((0,), (0,)), ((), ())), preferred_element_type=F32)


def _chunk_cumprod(x, reverse):
    n, c = x.shape
    nb = n // 8
    y = x.reshape(nb, 8, c)
    sub = lax.broadcasted_iota(jnp.int32, (nb, 8, c), 1)
    for s in (1, 2, 4):
        if reverse:
            y = y * jnp.where(sub < 8 - s, pltpu.roll(y, 8 - s, axis=1), 1.0)
        else:
            y = y * jnp.where(sub >= s, pltpu.roll(y, s, axis=1), 1.0)
    offs = [None] * nb
    acc = jnp.ones((1, c), F32)
    for b in (reversed(range(nb)) if reverse else range(nb)):
        offs[b] = acc
        acc = acc * (y[b, 0:1, :] if reverse else y[b, 7:8, :])
    return (y * jnp.stack(offs)).reshape(n, c), acc


def _cumsum_rows(mask_bf16, x):
    hi = x.astype(BF16)
    r1 = x - hi.astype(F32)
    mid = r1.astype(BF16)
    lo = (r1 - mid.astype(F32)).astype(BF16)
    return _dot(mask_bf16, hi) + _dot(mask_bf16, mid) + _dot(mask_bf16, lo)


def _mod_kernel(c_ref, cc_ref, w_ref, b_ref, lbl_ref, mod_ref, lb_ref, rows_ref):
    n = c_ref.shape[0]
    rows_ref[...] = jnp.zeros(rows_ref.shape, F32)
    rows_ref[0:n, :] = c_ref[...]
    rows_ref[n:n + 1, :] = cc_ref[...]
    mod_ref[...] = _dot(_silu(rows_ref[...]), w_ref[...]) + b_ref[...]
    l = lbl_ref[...]
    e = jnp.exp(l - jnp.max(l, axis=0, keepdims=True))
    lb_ref[...] = e[0] / jnp.sum(e, axis=0)


def _mod_call(c, c_ctx, w_mod, b_mod, lb_logits):
    return pl.pallas_call(
        _mod_kernel,
        out_shape=(jax.ShapeDtypeStruct((SUBLANES, 3 * D_MODEL), F32),
                   jax.ShapeDtypeStruct((2, D_MODEL), F32)),
        scratch_shapes=[pltpu.VMEM((SUBLANES, D_MODEL), F32)],
        compiler_params=pltpu.CompilerParams(vmem_limit_bytes=VMEM_LIMIT),
        name="mod",
    )(c, c_ctx, w_mod, b_mod, lb_logits)


def _inproj_kernel(x_ref, mod_ref, *refs, head_major):
    n_col = (len(refs) - 1) // 2
    w_refs, b_refs, o_ref = refs[:n_col], refs[n_col:2 * n_col], refs[-1]
    u = (x_ref[...] * (1.0 + mod_ref[1:2, :]) + mod_ref[0:1, :]).astype(BF16)
    for j in range(n_col):
        z = (_dot(u, w_refs[j][...]) + b_refs[j][...]).astype(o_ref.dtype)
        if head_major:
            for h in range(N_HEADS):
                o_ref[j, h] = z[:, h * HEAD_DIM:(h + 1) * HEAD_DIM]
        else:
            o_ref[j] = z


def _inproj_call(x2, mod3, w, b, cols, *, out_dtype, tm, tiles_per_mod, head_major=False):
    n_tok = x2.shape[0]
    n_col = len(cols)
    tile = (N_HEADS, tm, HEAD_DIM) if head_major else (tm, D_MODEL)
    col_block = lambda rows: [pl.BlockSpec((rows, D_MODEL), lambda i, c=c: (0, c), pipeline_mode=pl.Buffered(1))
                              for c in cols]
    return pl.pallas_call(
        functools.partial(_inproj_kernel, head_major=head_major),
        grid=(n_tok // tm,),
        in_specs=[
            pl.BlockSpec((tm, D_MODEL), lambda i: (i, 0)),
            pl.BlockSpec((None, 3, D_MODEL), lambda i: (i // tiles_per_mod, 0, 0)),
            *col_block(D_MODEL),
            *col_block(1),
        ],
        out_specs=pl.BlockSpec((None, n_col) + tile, lambda i: (i, 0) + (0,) * len(tile)),
        out_shape=jax.ShapeDtypeStruct((n_tok // tm, n_col) + tile, out_dtype),
        compiler_params=pltpu.CompilerParams(
            dimension_semantics=("arbitrary",), vmem_limit_bytes=VMEM_LIMIT),
        name="inproj",
    )(x2, mod3, *([w] * n_col), *([b] * n_col))


LOG2_E = 1.4426950408889634


def _lru_ab(xc, h_r, h_i, br, bi, sp):
    k = (-0.5 * RG_C * LOG2_E) * sp
    a = jnp.exp2(k * jnp.tanh(h_r + 0.5 * br) + k)
    i = 0.5 * jnp.tanh(h_i + 0.5 * bi) + 0.5
    y = 1.0 - a * a
    mult = jnp.where(y > 0.0, y * lax.rsqrt(y), 0.0)
    return a, mult * (i * xc)


def _softplus(y):
    return jnp.maximum(y, 0.0) + jnp.log(1.0 + jnp.exp(-jnp.abs(y)))


def _ctx_kernel(zf_ref, zb_ref, v_ref, z5_ref, lb_ref, cw_ref, cb_ref, wg_ref, br_ref, bi_ref, lam_ref,
                sf_ref, sb_ref, hf_ref, hb_ref, zp_ref, a_ref, b_ref):
    n = zf_ref.shape[0]
    ri = lax.broadcasted_iota(jnp.int32, (n, n), 0)
    ci = lax.broadcasted_iota(jnp.int32, (n, n), 1)
    tril = (ci <= ri).astype(F32).astype(BF16)
    v = v_ref[...].astype(BF16)

    lf = _log_forget(zf_ref[...], lb_ref[0:1, :])
    g = _cumsum_rows(tril, lf)
    ke = ((1.0 - jnp.exp(lf)) * jnp.exp(g[n - 1:n, :] - g)).astype(BF16)
    for h in range(N_HEADS):
        sl = slice(h * HEAD_DIM, (h + 1) * HEAD_DIM)
        sf_ref[h] = _dot_tn(v[:, sl], ke[:, sl])
    lf = _log_forget(zb_ref[...], lb_ref[1:2, :])
    g = _cumsum_rows(tril, lf)
    ke = ((1.0 - jnp.exp(lf)) * jnp.exp(g - lf)).astype(BF16)
    for h in range(N_HEADS):
        sl = slice(h * HEAD_DIM, (h + 1) * HEAD_DIM)
        sb_ref[h] = _dot_tn(v[:, sl], ke[:, sl])

    pad = SUBLANES
    zp_ref[0:pad, :] = jnp.zeros((pad, D_MODEL), F32)
    zp_ref[pad + n:2 * pad + n, :] = jnp.zeros((pad, D_MODEL), F32)
    zp_ref[pad:pad + n, :] = z5_ref[...]
    first = pad - (N_CONV - 1) // 2
    xc = cb_ref[...] + zp_ref[first:first + n, :] * cw_ref[0:1, :]
    for kk in range(1, N_CONV):
        xc = xc + zp_ref[first + kk:first + kk + n, :] * cw_ref[kk:kk + 1, :]

    for d, h_ref in ((0, hf_ref), (1, hb_ref)):
        sp = _softplus(-lam_ref[d:d + 1, :])
        for blk in range(N_BLOCKS):
            sl = slice(blk * BLOCK_DIM, (blk + 1) * BLOCK_DIM)
            g = _dot(xc[:, sl].astype(BF16), wg_ref[blk, :, 2 * d * BLOCK_DIM:2 * (d + 1) * BLOCK_DIM])
            a, b = _lru_ab(xc[:, sl], g[:, :BLOCK_DIM], g[:, BLOCK_DIM:],
                           br_ref[d:d + 1, sl], bi_ref[d:d + 1, sl], sp[:, sl])
            a_ref[:, sl] = a
            b_ref[:, sl] = b

        def step(t, h, d=d):
            tt = (n - 1 - t) if d == 1 else t
            return a_ref[pl.ds(tt, 1), :] * h + b_ref[pl.ds(tt, 1), :]

        h_ref[...] = lax.fori_loop(0, n, step, jnp.zeros((1, D_MODEL), F32), unroll=8)


def _ctx_call(feat_c, lb, conv_w, conv_b, wg, br, bi, lam, *, batch, n_ctx):
    feat = lambda j: pl.BlockSpec((None, None, n_ctx, D_MODEL), lambda b, j=j: (b, j, 0, 0))
    full = lambda shape: pl.BlockSpec(shape, lambda b: (0,) * len(shape))
    state = pl.BlockSpec((None, N_HEADS, HEAD_DIM, HEAD_DIM), lambda b: (b, 0, 0, 0))
    hvec = pl.BlockSpec((None, 1, D_MODEL), lambda b: (b, 0, 0))
    return pl.pallas_call(
        _ctx_kernel,
        grid=(batch,),
        in_specs=[feat(0), feat(1), feat(2), feat(3), full((2, D_MODEL)),
                  full((N_CONV, D_MODEL)), full((1, D_MODEL)),
                  full((N_BLOCKS, BLOCK_DIM, 4 * BLOCK_DIM)),
                  full((2, D_MODEL)), full((2, D_MODEL)), full((2, D_MODEL))],
        out_specs=(state, state, hvec, hvec),
        out_shape=(jax.ShapeDtypeStruct((batch, N_HEADS, HEAD_DIM, HEAD_DIM), F32),
                   jax.ShapeDtypeStruct((batch, N_HEADS, HEAD_DIM, HEAD_DIM), F32),
                   jax.ShapeDtypeStruct((batch, 1, D_MODEL), F32),
                   jax.ShapeDtypeStruct((batch, 1, D_MODEL), F32)),
        scratch_shapes=[pltpu.VMEM((n_ctx + 2 * SUBLANES, D_MODEL), F32),
                        pltpu.VMEM((n_ctx, D_MODEL), F32),
                        pltpu.VMEM((n_ctx, D_MODEL), F32)],
        compiler_params=pltpu.CompilerParams(
            dimension_semantics=("arbitrary",), vmem_limit_bytes=VMEM_LIMIT),
        name="ctx_states",
    )(feat_c, feat_c, feat_c, feat_c, lb, conv_w, conv_b, wg, br, bi, lam)


GLA_ROWS_PER_ITER = 16


def _gla_kernel(*refs, reverse, rows, finish):
    if finish:
        (zq_ref, zf_ref, v_ref, lb_ref, s0_ref, other_ref, o_ref,
         st_ref, sn_ref, qd_ref, kit_ref, ke_ref, dec_ref) = refs
    else:
        zq_ref, zf_ref, v_ref, lb_ref, s0_ref, o_ref, st_ref, sn_ref, qd_ref, kit_ref, ke_ref, dec_ref = refs

    @pl.when(pl.program_id(1) == 0)
    def _():
        st_ref[...] = s0_ref[...]
        for h in range(N_HEADS):
            sn_ref[h] = s0_ref[h].astype(BF16).T

    ri = lax.broadcasted_iota(jnp.int32, (GRID_W, GRID_W), 0)
    ci = lax.broadcasted_iota(jnp.int32, (GRID_W, GRID_W), 1)
    allow = (ci >= ri) if reverse else (ci <= ri)
    lb_dir = 1 if reverse else 0

    def row_offset(n):
        n = jnp.minimum(n, rows - 1)
        return pl.multiple_of(((rows - 1 - n) if reverse else n) * GRID_W, GRID_W)

    def prepare(n, slot):
        off = row_offset(n)
        zq = zq_ref[pl.ds(off, GRID_W), :]
        hq = (0.5 * Q_SCALE) * zq
        q = hq * jnp.tanh(0.5 * zq) + hq
        lb = lb_ref[lb_dir:lb_dir + 1, :]
        f = (0.5 + 0.5 * lb) + (0.5 - 0.5 * lb) * jnp.tanh(0.5 * zf_ref[pl.ds(off, GRID_W), :])
        dg, dec = _chunk_cumprod(f, reverse)
        ki = (1.0 - f) / dg
        qd = (q * dg).astype(BF16)
        ke = (ki * dec).astype(BF16)
        kit_ref[slot] = ki.astype(BF16).T
        dec_ref[slot] = dec
        for h in range(N_HEADS):
            sl = slice(h * HEAD_DIM, (h + 1) * HEAD_DIM)
            qd_ref[slot, h] = qd[:, sl]
            ke_ref[slot, h] = ke[:, sl]

    def contract(n, slot):
        off = row_offset(n)
        for h in range(N_HEADS):
            sl = slice(h * HEAD_DIM, (h + 1) * HEAD_DIM)
            qd = qd_ref[slot, h]
            v = v_ref[h, pl.ds(off, GRID_W), :]
            p = jnp.where(allow, _dot(qd, kit_ref[slot, sl, :]), 0.0).astype(BF16)
            o = _dot(p, v) + _dot(qd, sn_ref[h])
            st = st_ref[h] * dec_ref[slot, :, sl] + _dot_tn(v, ke_ref[slot, h])
            st_ref[h] = st
            sn_ref[h] = st.astype(BF16).T
            if finish:
                o = o + other_ref[h, pl.ds(off, GRID_W), :].astype(F32)
            o_ref[h, pl.ds(off, GRID_W), :] = o.astype(o_ref.dtype)

    prepare(0, 0)

    def body(m, carry):
        n = GLA_ROWS_PER_ITER * m
        for i in range(GLA_ROWS_PER_ITER):
            contract(n + i, i % 2)
            prepare(n + i + 1, (i + 1) % 2)
        return carry

    lax.fori_loop(0, rows // GLA_ROWS_PER_ITER, body, 0)


def _gla_call(feat, feat_h, lb, f_col, s0, other=None, *, reverse, batch, seq, rows):
    tb = rows * GRID_W
    nrb = seq // tb
    finish = other is not None
    per_tile = feat.shape[2] // tb

    def row_block(b, i):
        return b * nrb + ((nrb - 1 - i) if reverse else i)

    def col(c):
        return pl.BlockSpec((None, None, tb, D_MODEL), lambda b, i, c=c: (
            row_block(b, i) // per_tile, c, row_block(b, i) % per_tile, 0))

    tok = pl.BlockSpec((N_HEADS, tb, HEAD_DIM), lambda b, i: (0, row_block(b, i), 0))
    v_spec = pl.BlockSpec((None, None, N_HEADS, tb, HEAD_DIM), lambda b, i: (
        row_block(b, i) // per_tile, H_V, 0, row_block(b, i) % per_tile, 0))
    in_specs = [col(F_Q), col(f_col), v_spec,
                pl.BlockSpec((2, D_MODEL), lambda b, i: (0, 0)),
                pl.BlockSpec((None, N_HEADS, HEAD_DIM, HEAD_DIM), lambda b, i: (b, 0, 0, 0))]
    args = [feat, feat, feat_h, lb, s0]
    if finish:
        in_specs += [tok]
        args += [other]
    return pl.pallas_call(
        functools.partial(_gla_kernel, reverse=reverse, rows=rows, finish=finish),
        grid=(batch, nrb),
        in_specs=in_specs,
        out_specs=tok,
        out_shape=jax.ShapeDtypeStruct((N_HEADS, batch * seq, HEAD_DIM), BF16),
        scratch_shapes=[pltpu.VMEM((N_HEADS, HEAD_DIM, HEAD_DIM), F32)]
        + [pltpu.VMEM((N_HEADS, HEAD_DIM, HEAD_DIM), BF16), pltpu.VMEM((2, N_HEADS, GRID_W, HEAD_DIM), BF16),
           pltpu.VMEM((2, D_MODEL, GRID_W), BF16), pltpu.VMEM((2, N_HEADS, GRID_W, HEAD_DIM), BF16),
           pltpu.VMEM((2, 1, D_MODEL), F32)],
        compiler_params=pltpu.CompilerParams(
            dimension_semantics=("arbitrary", "arbitrary"), vmem_limit_bytes=VMEM_LIMIT),
        name="gla_bwd" if reverse else "gla_fwd",
    )(*args)


def _lru_kernel(z5_ref, cw_ref, cb_ref, wg_ref, br_ref, bi_ref, lam_ref, h0f_ref, h0b_ref, o_ref,
                zp_ref, af_ref, bf_ref, ab_ref, bb_ref, cf_ref, cr_ref, *, n_rows, rows_per_step):
    w = GRID_W
    seq = n_rows * w
    tb = rows_per_step * w
    n_steps = n_rows // rows_per_step

    zp_ref[0:w, :] = jnp.zeros((w, BLOCK_DIM), F32)
    zp_ref[w + seq:, :] = jnp.zeros((2 * w, BLOCK_DIM), F32)
    tile = z5_ref.shape[1]
    for k in range(z5_ref.shape[0]):
        zp_ref[w + k * tile:w + (k + 1) * tile, :] = z5_ref[k]

    sp = [_softplus(-lam_ref[d:d + 1, :]) for d in (0, 1)]

    def gates(s, c):
        off = pl.multiple_of(s * tb, tb)
        xc = cb_ref[...] + zp_ref[pl.ds(off, tb), :] * cw_ref[0:1, :]
        for kk in range(1, N_CONV):
            xc = xc + zp_ref[pl.ds(off + kk * w, tb), :] * cw_ref[kk:kk + 1, :]
        g = _dot(xc.astype(BF16), wg_ref[...])
        for d, a_ref, b_ref in ((0, af_ref, bf_ref), (1, ab_ref, bb_ref)):
            c0 = 2 * d * BLOCK_DIM
            a, b = _lru_ab(xc, g[:, c0:c0 + BLOCK_DIM], g[:, c0 + BLOCK_DIM:c0 + 2 * BLOCK_DIM],
                           br_ref[d:d + 1, :], bi_ref[d:d + 1, :], sp[d])
            a_ref[pl.ds(off, tb), :] = a
            b_ref[pl.ds(off, tb), :] = b
        return c

    lax.fori_loop(0, n_steps, gates, 0, unroll=4)

    def scan(n, carry):
        hf, pf, hb, pb = carry
        off_f = pl.multiple_of(n * w, w)
        off_b = pl.multiple_of((n_rows - 1 - n) * w, w)
        a = af_ref[pl.ds(off_f, w), :]
        hf = a * hf + bf_ref[pl.ds(off_f, w), :]
        pf = pf * a
        bf_ref[pl.ds(off_f, w), :] = hf
        af_ref[pl.ds(off_f, w), :] = pf
        a = ab_ref[pl.ds(off_b, w), :]
        hb = a * hb + bb_ref[pl.ds(off_b, w), :]
        pb = pb * a
        bb_ref[pl.ds(off_b, w), :] = hb
        ab_ref[pl.ds(off_b, w), :] = pb
        return hf, pf, hb, pb

    zeros = jnp.zeros((w, BLOCK_DIM), F32)
    ones = jnp.ones((w, BLOCK_DIM), F32)
    lax.fori_loop(0, n_rows, scan, (zeros, ones, zeros, ones), unroll=16)

    last = (n_rows - 1) * w

    def carry(n, c):
        cf, cb = c
        cf_ref[pl.ds(n, 1), :] = cf
        cf = bf_ref[pl.ds(last + n, 1), :] + af_ref[pl.ds(last + n, 1), :] * cf
        col = w - 1 - n
        cr_ref[pl.ds(col, 1), :] = cb
        cb = bb_ref[pl.ds(col, 1), :] + ab_ref[pl.ds(col, 1), :] * cb
        return cf, cb

    lax.fori_loop(0, w, carry, (h0f_ref[...], h0b_ref[...]), unroll=8)

    def fix(s, c):
        off = pl.multiple_of(s * tb, tb)
        cf = jnp.concatenate([cf_ref[...]] * rows_per_step, axis=0)
        cr = jnp.concatenate([cr_ref[...]] * rows_per_step, axis=0)
        h = (bf_ref[pl.ds(off, tb), :] + af_ref[pl.ds(off, tb), :] * cf
             + bb_ref[pl.ds(off, tb), :] + ab_ref[pl.ds(off, tb), :] * cr)
        o_ref[pl.ds(off, tb), :] = h.astype(o_ref.dtype)
        return c

    lax.fori_loop(0, n_steps, fix, 0, unroll=4)


def _lru_call(feat, conv_w, conv_b, wg, br, bi, lam, h0f, h0b, *, batch, seq):
    n_rows = seq // GRID_W
    vec = lambda n: pl.BlockSpec((n, BLOCK_DIM), lambda b, c: (0, c))
    h0 = pl.BlockSpec((None, 1, BLOCK_DIM), lambda b, c: (b, 0, c))
    buf = pltpu.VMEM((seq, BLOCK_DIM), F32)
    n_i, n_col, tm, _ = feat.shape
    feat5 = feat.reshape(batch, n_i // batch, n_col, tm, D_MODEL)
    return pl.pallas_call(
        functools.partial(_lru_kernel, n_rows=n_rows, rows_per_step=LRU_ROWS_PER_STEP),
        grid=(batch, N_BLOCKS),
        in_specs=[pl.BlockSpec((None, n_i // batch, None, tm, BLOCK_DIM), lambda b, c: (b, 0, F_Z5, 0, c)),
                  vec(N_CONV), vec(1),
                  pl.BlockSpec((None, BLOCK_DIM, 4 * BLOCK_DIM), lambda b, c: (c, 0, 0)),
                  vec(2), vec(2), vec(2), h0, h0],
        out_specs=pl.BlockSpec((None, seq, BLOCK_DIM), lambda b, c: (c, b, 0)),
        out_shape=jax.ShapeDtypeStruct((N_BLOCKS, batch * seq, BLOCK_DIM), BF16),
        scratch_shapes=[pltpu.VMEM((seq + 3 * GRID_W, BLOCK_DIM), F32), buf, buf, buf, buf,
                        pltpu.VMEM((GRID_W, BLOCK_DIM), F32), pltpu.VMEM((GRID_W, BLOCK_DIM), F32)],
        compiler_params=pltpu.CompilerParams(
            dimension_semantics=("arbitrary", "arbitrary"), vmem_limit_bytes=VMEM_LIMIT),
        name="lru",
    )(feat5, conv_w, conv_b, wg, br, bi, lam, h0f, h0b)


def _merge_kernel(oa_ref, hx_ref, z4_ref, z6_ref, z7_ref, z8_ref, x_ref, mod_ref, ng_ref,
                  pa_ref, pb_ref, wo_ref, lg_ref, lbias_ref, o_ref):
    f32 = lambda ref: jnp.concatenate([ref[h] for h in range(N_HEADS)], axis=-1).astype(F32)
    o_b = (f32(hx_ref) * _silu(f32(z6_ref))).astype(BF16)
    y = _sigmoid(f32(z8_ref)) * _dot(o_b, pb_ref[...])
    y_a = None
    for pair in range(N_HEADS // 2):
        o_h = []
        for h in (2 * pair, 2 * pair + 1):
            t = oa_ref[h].astype(F32)
            ms = jnp.mean(t * t, axis=-1, keepdims=True)
            o_h.append((t * lax.rsqrt(ms + RMS_EPS) * ng_ref[...] * _silu(z4_ref[h].astype(F32))).astype(BF16))
        rows = slice(2 * pair * HEAD_DIM, 2 * (pair + 1) * HEAD_DIM)
        part = _dot(jnp.concatenate(o_h, axis=-1), pa_ref[rows, :])
        y_a = part if y_a is None else y_a + part
    y = y + _sigmoid(f32(z7_ref)) * y_a
    y = _dot(y.astype(BF16), wo_ref[...])
    t = DEEPNORM_ALPHA * x_ref[...] + mod_ref[2:3, :] * y
    mu = jnp.mean(t, axis=-1, keepdims=True)
    tc = t - mu
    var = jnp.mean(tc * tc, axis=-1, keepdims=True)
    o_ref[...] = tc * lax.rsqrt(var + LN_EPS) * lg_ref[...] + lbias_ref[...]


def _merge_call(oa, hx, feat_h, x2, mod3, ng, pa, pb, wo, lg, lbias, *, tm, tiles_per_batch):
    n_tok = x2.shape[0]
    tok = pl.BlockSpec((tm, D_MODEL), lambda i: (i, 0))
    heads = pl.BlockSpec((N_HEADS, tm, HEAD_DIM), lambda i: (0, i, 0))
    per_tile = feat_h.shape[3] // tm
    col = lambda j: pl.BlockSpec((None, None, N_HEADS, tm, HEAD_DIM),
                                 lambda i, j=j: (i // per_tile, j, 0, i % per_tile, 0))
    full = lambda shape: pl.BlockSpec(shape, lambda i: (0,) * len(shape), pipeline_mode=pl.Buffered(1))
    return pl.pallas_call(
        _merge_kernel,
        grid=(n_tok // tm,),
        in_specs=[heads, heads, col(H_G4), col(H_G6), col(H_M7), col(H_M8), tok,
                  pl.BlockSpec((None, 3, D_MODEL), lambda i: (i // tiles_per_batch, 0, 0)),
                  full((1, HEAD_DIM)), full((D_MODEL, D_MODEL)), full((D_MODEL, D_MODEL)),
                  full((D_MODEL, D_MODEL)), full((1, D_MODEL)), full((1, D_MODEL))],
        out_specs=tok,
        out_shape=jax.ShapeDtypeStruct((n_tok, D_MODEL), F32),
        compiler_params=pltpu.CompilerParams(
            dimension_semantics=("arbitrary",), vmem_limit_bytes=VMEM_LIMIT),
        name="merge",
    )(oa, hx, feat_h, feat_h, feat_h, feat_h, x2, mod3, ng, pa, pb, wo, lg, lbias)


def kernel(x, c, ctx, c_ctx, w_mod, b_mod, w_in, b_in, lb_logits, norm_a_g, conv_w, conv_b,
           w_r, b_r, w_i, b_i, lam, p_a, p_b, w_out, ln_g, ln_b):
    batch, seq, d = x.shape
    n_ctx = ctx.shape[1]
    assert d == D_MODEL and seq % GRID_W == 0 and w_in.shape[0] == 1

    w_bf = w_in[0].astype(BF16)
    b2 = b_in[0][None, :]
    wg = (0.5 * jnp.concatenate([w_r[0, 0], w_i[0, 0], w_r[0, 1], w_i[0, 1]], axis=-1)).astype(BF16)
    assert batch < SUBLANES
    mod, lb = _mod_call(c, c_ctx[None, :], w_mod[0], b_mod[0][None, :], lb_logits)
    mod3 = mod.reshape(SUBLANES, 3, D_MODEL)

    x2 = x.reshape(batch * seq, D_MODEL)
    tm = INPROJ_TM
    feat = _inproj_call(x2, mod3, w_bf, b2, LATENT_F32, out_dtype=F32, tm=tm, tiles_per_mod=seq // tm)
    feat_h = _inproj_call(x2, mod3, w_bf, b2, LATENT_BF16, out_dtype=BF16, tm=tm, tiles_per_mod=seq // tm,
                          head_major=True)
    feat_c = _inproj_call(ctx.reshape(batch * n_ctx, D_MODEL), mod3[batch:batch + 1], w_bf, b2, CTX_F32,
                          out_dtype=F32, tm=n_ctx, tiles_per_mod=batch)

    cw, cb = conv_w[0], conv_b[0][None, :]
    s0f, s0b, h0f, h0b = _ctx_call(feat_c, lb, cw, cb, wg, b_r[0], b_i[0], lam[0], batch=batch, n_ctx=n_ctx)

    o_b = _gla_call(feat, feat_h, lb, F_F1, s0b, reverse=True, batch=batch, seq=seq, rows=GLA_ROWS)
    oa = _gla_call(feat, feat_h, lb, F_F0, s0f, o_b, reverse=False, batch=batch, seq=seq, rows=GLA_ROWS)
    hx = _lru_call(feat, cw, cb, wg, b_r[0], b_i[0], lam[0], h0f, h0b, batch=batch, seq=seq)

    out = _merge_call(oa, hx, feat_h, x2, mod3, norm_a_g[0][None, :],
                      p_a[0].astype(BF16), p_b[0].astype(BF16), w_out[0].astype(BF16),
                      ln_g[0][None, :], ln_b[0][None, :], tm=MERGE_TM, tiles_per_batch=seq // MERGE_TM)
    return out.reshape(batch, seq, D_MODEL)
```

```python
import functools

import jax
import jax.numpy as jnp
from jax import lax
from jax.experimental import pallas as pl
from jax.experimental.pallas import tpu as pltpu

F32 = jnp.float32
BF16 = jnp.bfloat16

D_MODEL = 1024
GRID_W = 64
HEAD_DIM = 128
N_HEADS = D_MODEL // HEAD_DIM
N_BLOCKS = 8
BLOCK_DIM = D_MODEL // N_BLOCKS
N_CONV = 4
RG_C = 8.0
LN_EPS = 1e-5
RMS_EPS = 1e-6
DEEPNORM_ALPHA = 2.0 ** 0.25
Q_SCALE = HEAD_DIM ** -0.5

V7X_VMEM_BYTES = 64 * 1024 * 1024
VMEM_LIMIT = V7X_VMEM_BYTES - 8 * 1024 * 1024
SUBLANES = 8

INPROJ_TM = 1024
GLA_ROWS = 16
LRU_ROWS_PER_STEP = 8
MERGE_TM = 512

LATENT_F32 = (0, 1, 2, 5)
LATENT_BF16 = (3, 4, 6, 7, 8)
CTX_F32 = (1, 2, 3, 5)
F_Q, F_F0, F_F1, F_Z5 = 0, 1, 2, 3
H_V, H_G4, H_G6, H_M7, H_M8 = 0, 1, 2, 3, 4


def _sigmoid(x):
    return 0.5 * jnp.tanh(0.5 * x) + 0.5


def _silu(x):
    h = 0.5 * x
    return h * jnp.tanh(h) + h


def _scaled_silu(z):
    hq = (0.5 * Q_SCALE) * z
    return hq * jnp.tanh(0.5 * z) + hq


def _forget(z, lb):
    return (0.5 + 0.5 * lb) + (0.5 - 0.5 * lb) * jnp.tanh(0.5 * z)


def _dot(a, b):
    return jnp.dot(a, b, preferred_element_type=F32)


def _dot_tn(a, b):
    return lax.dot_general(a, b, (((0,), (0,)), ((), ())), preferred_element_type=F32)


def _chunk_cumprod(x, reverse):
    n, c = x.shape
    nb = n // 8
    y = x.reshape(nb, 8, c)
    sub = lax.broadcasted_iota(jnp.int32, (nb, 8, c), 1)
    for s in (1, 2, 4):
        if reverse:
            y = y * jnp.where(sub < 8 - s, pltpu.roll(y, 8 - s, axis=1), 1.0)
        else:
            y = y * jnp.where(sub >= s, pltpu.roll(y, s, axis=1), 1.0)
    offs = [None] * nb
    acc = jnp.ones((1, c), F32)
    for b in (reversed(range(nb)) if reverse else range(nb)):
        offs[b] = acc
        acc = acc * (y[b, 0:1, :] if reverse else y[b, 7:8, :])
    return (y * jnp.stack(offs)).reshape(n, c), acc


def _cumsum_rows(mask_bf16, x):
    hi = x.astype(BF16)
    r1 = x - hi.astype(F32)
    mid = r1.astype(BF16)
    lo = (r1 - mid.astype(F32)).astype(BF16)
    return _dot(mask_bf16, hi) + _dot(mask_bf16, mid) + _dot(mask_bf16, lo)


def _mod_kernel(c_ref, cc_ref, w_ref, b_ref, lbl_ref, mod_ref, lb_ref, rows_ref):
    n = c_ref.shape[0]
    rows_ref[...] = jnp.zeros(rows_ref.shape, F32)
    rows_ref[0:n, :] = c_ref[...]
    rows_ref[n:n + 1, :] = cc_ref[...]
    mod_ref[...] = _dot(_silu(rows_ref[...]), w_ref[...]) + b_ref[...]
    l = lbl_ref[...]
    e = jnp.exp(l - jnp.max(l, axis=0, keepdims=True))
    lb_ref[...] = e[0] / jnp.sum(e, axis=0)


def _mod_call(c, c_ctx, w_mod, b_mod, lb_logits):
    return pl.pallas_call(
        _mod_kernel,
        out_shape=(jax.ShapeDtypeStruct((SUBLANES, 3 * D_MODEL), F32),
                   jax.ShapeDtypeStruct((2, D_MODEL), F32)),
        scratch_shapes=[pltpu.VMEM((SUBLANES, D_MODEL), F32)],
        compiler_params=pltpu.CompilerParams(vmem_limit_bytes=VMEM_LIMIT),
        name="mod",
    )(c, c_ctx, w_mod, b_mod, lb_logits)


def _inproj_kernel(x_ref, mod_ref, *refs, head_major, acts):
    n_col = len(acts)
    if any(a and a.startswith("forget") for a in acts):
        lb_ref, refs = refs[0], refs[1:]
    w_refs, b_refs, o_ref = refs[:n_col], refs[n_col:2 * n_col], refs[-1]
    u = (x_ref[...] * (1.0 + mod_ref[1:2, :]) + mod_ref[0:1, :]).astype(BF16)
    for j in range(n_col):
        z = _dot(u, w_refs[j][...]) + b_refs[j][...]
        if acts[j] == "q":
            z = _scaled_silu(z)
        elif acts[j] == "silu":
            z = _silu(z)
        elif acts[j] == "sig":
            z = _sigmoid(z)
        elif acts[j] is not None:
            d = int(acts[j][-1])
            z = _forget(z, lb_ref[d:d + 1, :])
        z = z.astype(o_ref.dtype)
        if head_major:
            for h in range(N_HEADS):
                o_ref[j, h] = z[:, h * HEAD_DIM:(h + 1) * HEAD_DIM]
        else:
            o_ref[j] = z


def _inproj_call(x2, mod3, w, b, cols, acts, lb=None, *, out_dtype, tm, tiles_per_mod, head_major=False):
    n_tok = x2.shape[0]
    n_col = len(cols)
    tile = (N_HEADS, tm, HEAD_DIM) if head_major else (tm, D_MODEL)
    col_block = lambda rows: [pl.BlockSpec((rows, D_MODEL), lambda i, c=c: (0, c), pipeline_mode=pl.Buffered(1))
                              for c in cols]
    return pl.pallas_call(
        functools.partial(_inproj_kernel, head_major=head_major, acts=acts),
        grid=(n_tok // tm,),
        in_specs=[
            pl.BlockSpec((tm, D_MODEL), lambda i: (i, 0)),
            pl.BlockSpec((None, 3, D_MODEL), lambda i: (i // tiles_per_mod, 0, 0)),
            *([] if lb is None else [pl.BlockSpec((2, D_MODEL), lambda i: (0, 0))]),
            *col_block(D_MODEL),
            *col_block(1),
        ],
        out_specs=pl.BlockSpec((None, n_col) + tile, lambda i: (i, 0) + (0,) * len(tile)),
        out_shape=jax.ShapeDtypeStruct((n_tok // tm, n_col) + tile, out_dtype),
        compiler_params=pltpu.CompilerParams(
            dimension_semantics=("arbitrary",), vmem_limit_bytes=VMEM_LIMIT),
        name="inproj",
    )(x2, mod3, *([] if lb is None else [lb]), *([w] * n_col), *([b] * n_col))


LOG2_E = 1.4426950408889634


def _lru_ab(xc, h_r, h_i, br, bi, sp):
    k = (-0.5 * RG_C * LOG2_E) * sp
    a = jnp.exp2(k * jnp.tanh(h_r + 0.5 * br) + k)
    i = 0.5 * jnp.tanh(h_i + 0.5 * bi) + 0.5
    y = 1.0 - a * a
    mult = jnp.where(y > 0.0, y * lax.rsqrt(y), 0.0)
    return a, mult * (i * xc)


def _softplus(y):
    return jnp.maximum(y, 0.0) + jnp.log(1.0 + jnp.exp(-jnp.abs(y)))


def _ctx_kernel(ff_ref, fb_ref, v_ref, z5_ref, cw_ref, cb_ref, wg_ref, br_ref, bi_ref, lam_ref,
                sf_ref, sb_ref, hf_ref, hb_ref, zp_ref, a_ref, b_ref):
    n = ff_ref.shape[0]
    ri = lax.broadcasted_iota(jnp.int32, (n, n), 0)
    ci = lax.broadcasted_iota(jnp.int32, (n, n), 1)
    tril = (ci <= ri).astype(F32).astype(BF16)
    v = v_ref[...].astype(BF16)

    f = ff_ref[...]
    g = _cumsum_rows(tril, jnp.log(f))
    ke = ((1.0 - f) * jnp.exp(g[n - 1:n, :] - g)).astype(BF16)
    for h in range(N_HEADS):
        sl = slice(h * HEAD_DIM, (h + 1) * HEAD_DIM)
        sf_ref[h] = _dot_tn(v[:, sl], ke[:, sl])
    f = fb_ref[...]
    lf = jnp.log(f)
    ke = ((1.0 - f) * jnp.exp(_cumsum_rows(tril, lf) - lf)).astype(BF16)
    for h in range(N_HEADS):
        sl = slice(h * HEAD_DIM, (h + 1) * HEAD_DIM)
        sb_ref[h] = _dot_tn(v[:, sl], ke[:, sl])

    pad = SUBLANES
    zp_ref[0:pad, :] = jnp.zeros((pad, D_MODEL), F32)
    zp_ref[pad + n:2 * pad + n, :] = jnp.zeros((pad, D_MODEL), F32)
    zp_ref[pad:pad + n, :] = z5_ref[...]
    first = pad - (N_CONV - 1) // 2
    xc = cb_ref[...] + zp_ref[first:first + n, :] * cw_ref[0:1, :]
    for kk in range(1, N_CONV):
        xc = xc + zp_ref[first + kk:first + kk + n, :] * cw_ref[kk:kk + 1, :]

    for d, h_ref in ((0, hf_ref), (1, hb_ref)):
        sp = _softplus(-lam_ref[d:d + 1, :])
        for blk in range(N_BLOCKS):
            sl = slice(blk * BLOCK_DIM, (blk + 1) * BLOCK_DIM)
            g = _dot(xc[:, sl].astype(BF16), wg_ref[blk, :, 2 * d * BLOCK_DIM:2 * (d + 1) * BLOCK_DIM])
            a, b = _lru_ab(xc[:, sl], g[:, :BLOCK_DIM], g[:, BLOCK_DIM:],
                           br_ref[d:d + 1, sl], bi_ref[d:d + 1, sl], sp[:, sl])
            a_ref[:, sl] = a
            b_ref[:, sl] = b

        def step(t, h, d=d):
            tt = (n - 1 - t) if d == 1 else t
            return a_ref[pl.ds(tt, 1), :] * h + b_ref[pl.ds(tt, 1), :]

        h_ref[...] = lax.fori_loop(0, n, step, jnp.zeros((1, D_MODEL), F32), unroll=8)


def _ctx_call(feat_c, conv_w, conv_b, wg, br, bi, lam, *, batch, n_ctx):
    feat = lambda j: pl.BlockSpec((None, None, n_ctx, D_MODEL), lambda b, j=j: (b, j, 0, 0))
    full = lambda shape: pl.BlockSpec(shape, lambda b: (0,) * len(shape))
    state = pl.BlockSpec((None, N_HEADS, HEAD_DIM, HEAD_DIM), lambda b: (b, 0, 0, 0))
    hvec = pl.BlockSpec((None, 1, D_MODEL), lambda b: (b, 0, 0))
    return pl.pallas_call(
        _ctx_kernel,
        grid=(batch,),
        in_specs=[feat(0), feat(1), feat(2), feat(3),
                  full((N_CONV, D_MODEL)), full((1, D_MODEL)),
                  full((N_BLOCKS, BLOCK_DIM, 4 * BLOCK_DIM)),
                  full((2, D_MODEL)), full((2, D_MODEL)), full((2, D_MODEL))],
        out_specs=(state, state, hvec, hvec),
        out_shape=(jax.ShapeDtypeStruct((batch, N_HEADS, HEAD_DIM, HEAD_DIM), F32),
                   jax.ShapeDtypeStruct((batch, N_HEADS, HEAD_DIM, HEAD_DIM), F32),
                   jax.ShapeDtypeStruct((batch, 1, D_MODEL), F32),
                   jax.ShapeDtypeStruct((batch, 1, D_MODEL), F32)),
        scratch_shapes=[pltpu.VMEM((n_ctx + 2 * SUBLANES, D_MODEL), F32),
                        pltpu.VMEM((n_ctx, D_MODEL), F32),
                        pltpu.VMEM((n_ctx, D_MODEL), F32)],
        compiler_params=pltpu.CompilerParams(
            dimension_semantics=("arbitrary",), vmem_limit_bytes=VMEM_LIMIT),
        name="ctx_states",
    )(feat_c, feat_c, feat_c, feat_c, conv_w, conv_b, wg, br, bi, lam)


GLA_ROWS_PER_ITER = 16


def _gla_kernel(*refs, reverse, rows, finish):
    if finish:
        (q_ref, f_ref, v_ref, s0_ref, other_ref, o_ref,
         st_ref, sn_ref, qd_ref, kit_ref, ke_ref, dec_ref) = refs
    else:
        q_ref, f_ref, v_ref, s0_ref, o_ref, st_ref, sn_ref, qd_ref, kit_ref, ke_ref, dec_ref = refs

    @pl.when(pl.program_id(1) == 0)
    def _():
        st_ref[...] = s0_ref[...]
        for h in range(N_HEADS):
            sn_ref[h] = s0_ref[h].astype(BF16).T

    ri = lax.broadcasted_iota(jnp.int32, (GRID_W, GRID_W), 0)
    ci = lax.broadcasted_iota(jnp.int32, (GRID_W, GRID_W), 1)
    allow = (ci >= ri) if reverse else (ci <= ri)

    def row_offset(n):
        n = jnp.minimum(n, rows - 1)
        return pl.multiple_of(((rows - 1 - n) if reverse else n) * GRID_W, GRID_W)

    def prepare(n, slot):
        off = row_offset(n)
        q = q_ref[pl.ds(off, GRID_W), :]
        f = f_ref[pl.ds(off, GRID_W), :]
        dg, dec = _chunk_cumprod(f, reverse)
        ki = (1.0 - f) / dg
        qd = (q * dg).astype(BF16)
        ke = (ki * dec).astype(BF16)
        kit_ref[slot] = ki.astype(BF16).T
        dec_ref[slot] = dec
        for h in range(N_HEADS):
            sl = slice(h * HEAD_DIM, (h + 1) * HEAD_DIM)
            qd_ref[slot, h] = qd[:, sl]
            ke_ref[slot, h] = ke[:, sl]

    def contract(n, slot):
        off = row_offset(n)
        for h in range(N_HEADS):
            sl = slice(h * HEAD_DIM, (h + 1) * HEAD_DIM)
            qd = qd_ref[slot, h]
            v = v_ref[h, pl.ds(off, GRID_W), :]
            p = jnp.where(allow, _dot(qd, kit_ref[slot, sl, :]), 0.0).astype(BF16)
            o = _dot(p, v) + _dot(qd, sn_ref[h])
            st = st_ref[h] * dec_ref[slot, :, sl] + _dot_tn(v, ke_ref[slot, h])
            st_ref[h] = st
            sn_ref[h] = st.astype(BF16).T
            if finish:
                o = o + other_ref[h, pl.ds(off, GRID_W), :].astype(F32)
            o_ref[h, pl.ds(off, GRID_W), :] = o.astype(o_ref.dtype)

    prepare(0, 0)

    def body(m, carry):
        n = GLA_ROWS_PER_ITER * m
        for i in range(GLA_ROWS_PER_ITER):
            contract(n + i, i % 2)
            prepare(n + i + 1, (i + 1) % 2)
        return carry

    lax.fori_loop(0, rows // GLA_ROWS_PER_ITER, body, 0)


def _gla_call(feat, feat_h, f_col, s0, other=None, *, reverse, batch, seq, rows):
    tb = rows * GRID_W
    nrb = seq // tb
    finish = other is not None
    per_tile = feat.shape[2] // tb

    def row_block(b, i):
        return b * nrb + ((nrb - 1 - i) if reverse else i)

    def col(c):
        return pl.BlockSpec((None, None, tb, D_MODEL), lambda b, i, c=c: (
            row_block(b, i) // per_tile, c, row_block(b, i) % per_tile, 0))

    tok = pl.BlockSpec((N_HEADS, tb, HEAD_DIM), lambda b, i: (0, row_block(b, i), 0))
    v_spec = pl.BlockSpec((None, None, N_HEADS, tb, HEAD_DIM), lambda b, i: (
        row_block(b, i) // per_tile, H_V, 0, row_block(b, i) % per_tile, 0))
    in_specs = [col(F_Q), col(f_col), v_spec,
                pl.BlockSpec((None, N_HEADS, HEAD_DIM, HEAD_DIM), lambda b, i: (b, 0, 0, 0))]
    args = [feat, feat, feat_h, s0]
    if finish:
        in_specs += [tok]
        args += [other]
    return pl.pallas_call(
        functools.partial(_gla_kernel, reverse=reverse, rows=rows, finish=finish),
        grid=(batch, nrb),
        in_specs=in_specs,
        out_specs=tok,
        out_shape=jax.ShapeDtypeStruct((N_HEADS, batch * seq, HEAD_DIM), BF16),
        scratch_shapes=[pltpu.VMEM((N_HEADS, HEAD_DIM, HEAD_DIM), F32)]
        + [pltpu.VMEM((N_HEADS, HEAD_DIM, HEAD_DIM), BF16), pltpu.VMEM((2, N_HEADS, GRID_W, HEAD_DIM), BF16),
           pltpu.VMEM((2, D_MODEL, GRID_W), BF16), pltpu.VMEM((2, N_HEADS, GRID_W, HEAD_DIM), BF16),
           pltpu.VMEM((2, 1, D_MODEL), F32)],
        compiler_params=pltpu.CompilerParams(
            dimension_semantics=("arbitrary", "arbitrary"), vmem_limit_bytes=VMEM_LIMIT),
        name="gla_bwd" if reverse else "gla_fwd",
    )(*args)


def _lru_kernel(z5_ref, cw_ref, cb_ref, wg_ref, br_ref, bi_ref, lam_ref, h0f_ref, h0b_ref, o_ref,
                zp_ref, af_ref, bf_ref, ab_ref, bb_ref, cf_ref, cr_ref, *, n_rows, rows_per_step):
    w = GRID_W
    seq = n_rows * w
    tb = rows_per_step * w
    n_steps = n_rows // rows_per_step

    zp_ref[0:w, :] = jnp.zeros((w, BLOCK_DIM), F32)
    zp_ref[w + seq:, :] = jnp.zeros((2 * w, BLOCK_DIM), F32)
    tile = z5_ref.shape[1]
    for k in range(z5_ref.shape[0]):
        zp_ref[w + k * tile:w + (k + 1) * tile, :] = z5_ref[k]

    sp = [_softplus(-lam_ref[d:d + 1, :]) for d in (0, 1)]

    def gates(s, c):
        off = pl.multiple_of(s * tb, tb)
        xc = cb_ref[...] + zp_ref[pl.ds(off, tb), :] * cw_ref[0:1, :]
        for kk in range(1, N_CONV):
            xc = xc + zp_ref[pl.ds(off + kk * w, tb), :] * cw_ref[kk:kk + 1, :]
        g = _dot(xc.astype(BF16), wg_ref[...])
        for d, a_ref, b_ref in ((0, af_ref, bf_ref), (1, ab_ref, bb_ref)):
            c0 = 2 * d * BLOCK_DIM
            a, b = _lru_ab(xc, g[:, c0:c0 + BLOCK_DIM], g[:, c0 + BLOCK_DIM:c0 + 2 * BLOCK_DIM],
                           br_ref[d:d + 1, :], bi_ref[d:d + 1, :], sp[d])
            a_ref[pl.ds(off, tb), :] = a
            b_ref[pl.ds(off, tb), :] = b
        return c

    lax.fori_loop(0, n_steps, gates, 0, unroll=4)

    def scan(n, carry):
        hf, pf, hb, pb = carry
        off_f = pl.multiple_of(n * w, w)
        off_b = pl.multiple_of((n_rows - 1 - n) * w, w)
        a = af_ref[pl.ds(off_f, w), :]
        hf = a * hf + bf_ref[pl.ds(off_f, w), :]
        pf = pf * a
        bf_ref[pl.ds(off_f, w), :] = hf
        af_ref[pl.ds(off_f, w), :] = pf
        a = ab_ref[pl.ds(off_b, w), :]
        hb = a * hb + bb_ref[pl.ds(off_b, w), :]
        pb = pb * a
        bb_ref[pl.ds(off_b, w), :] = hb
        ab_ref[pl.ds(off_b, w), :] = pb
        return hf, pf, hb, pb

    zeros = jnp.zeros((w, BLOCK_DIM), F32)
    ones = jnp.ones((w, BLOCK_DIM), F32)
    lax.fori_loop(0, n_rows, scan, (zeros, ones, zeros, ones), unroll=16)

    last = (n_rows - 1) * w

    def carry(n, c):
        cf, cb = c
        cf_ref[pl.ds(n, 1), :] = cf
        cf = bf_ref[pl.ds(last + n, 1), :] + af_ref[pl.ds(last + n, 1), :] * cf
        col = w - 1 - n
        cr_ref[pl.ds(col, 1), :] = cb
        cb = bb_ref[pl.ds(col, 1), :] + ab_ref[pl.ds(col, 1), :] * cb
        return cf, cb

    lax.fori_loop(0, w, carry, (h0f_ref[...], h0b_ref[...]), unroll=8)

    def fix(s, c):
        off = pl.multiple_of(s * tb, tb)
        cf = jnp.concatenate([cf_ref[...]] * rows_per_step, axis=0)
        cr = jnp.concatenate([cr_ref[...]] * rows_per_step, axis=0)
        h = (bf_ref[pl.ds(off, tb), :] + af_ref[pl.ds(off, tb), :] * cf
             + bb_ref[pl.ds(off, tb), :] + ab_ref[pl.ds(off, tb), :] * cr)
        o_ref[pl.ds(off, tb), :] = h.astype(o_ref.dtype)
        return c

    lax.fori_loop(0, n_steps, fix, 0, unroll=4)


def _lru_call(feat, conv_w, conv_b, wg, br, bi, lam, h0f, h0b, *, batch, seq):
    n_rows = seq // GRID_W
    vec = lambda n: pl.BlockSpec((n, BLOCK_DIM), lambda b, c: (0, c))
    h0 = pl.BlockSpec((None, 1, BLOCK_DIM), lambda b, c: (b, 0, c))
    buf = pltpu.VMEM((seq, BLOCK_DIM), F32)
    n_i, n_col, tm, _ = feat.shape
    feat5 = feat.reshape(batch, n_i // batch, n_col, tm, D_MODEL)
    return pl.pallas_call(
        functools.partial(_lru_kernel, n_rows=n_rows, rows_per_step=LRU_ROWS_PER_STEP),
        grid=(batch, N_BLOCKS),
        in_specs=[pl.BlockSpec((None, n_i // batch, None, tm, BLOCK_DIM), lambda b, c: (b, 0, F_Z5, 0, c)),
                  vec(N_CONV), vec(1),
                  pl.BlockSpec((None, BLOCK_DIM, 4 * BLOCK_DIM), lambda b, c: (c, 0, 0)),
                  vec(2), vec(2), vec(2), h0, h0],
        out_specs=pl.BlockSpec((None, seq, BLOCK_DIM), lambda b, c: (c, b, 0)),
        out_shape=jax.ShapeDtypeStruct((N_BLOCKS, batch * seq, BLOCK_DIM), BF16),
        scratch_shapes=[pltpu.VMEM((seq + 3 * GRID_W, BLOCK_DIM), F32), buf, buf, buf, buf,
                        pltpu.VMEM((GRID_W, BLOCK_DIM), F32), pltpu.VMEM((GRID_W, BLOCK_DIM), F32)],
        compiler_params=pltpu.CompilerParams(
            dimension_semantics=("arbitrary", "arbitrary"), vmem_limit_bytes=VMEM_LIMIT),
        name="lru",
    )(feat5, conv_w, conv_b, wg, br, bi, lam, h0f, h0b)


def _merge_kernel(oa_ref, hx_ref, g4_ref, g6_ref, m7_ref, m8_ref, x_ref, mod_ref, ng_ref,
                  pa_ref, pb_ref, wo_ref, lg_ref, lbias_ref, o_ref):
    f32 = lambda ref: jnp.concatenate([ref[h] for h in range(N_HEADS)], axis=-1).astype(F32)
    o_b = (f32(hx_ref) * f32(g6_ref)).astype(BF16)
    y = f32(m8_ref) * _dot(o_b, pb_ref[...])
    y_a = None
    for pair in range(N_HEADS // 2):
        o_h = []
        for h in (2 * pair, 2 * pair + 1):
            t = oa_ref[h].astype(F32)
            ms = jnp.mean(t * t, axis=-1, keepdims=True)
            o_h.append((t * lax.rsqrt(ms + RMS_EPS) * ng_ref[...] * g4_ref[h].astype(F32)).astype(BF16))
        rows = slice(2 * pair * HEAD_DIM, 2 * (pair + 1) * HEAD_DIM)
        part = _dot(jnp.concatenate(o_h, axis=-1), pa_ref[rows, :])
        y_a = part if y_a is None else y_a + part
    y = y + f32(m7_ref) * y_a
    y = _dot(y.astype(BF16), wo_ref[...])
    t = DEEPNORM_ALPHA * x_ref[...] + mod_ref[2:3, :] * y
    mu = jnp.mean(t, axis=-1, keepdims=True)
    tc = t - mu
    var = jnp.mean(tc * tc, axis=-1, keepdims=True)
    o_ref[...] = tc * lax.rsqrt(var + LN_EPS) * lg_ref[...] + lbias_ref[...]


def _merge_call(oa, hx, feat_h, x2, mod3, ng, pa, pb, wo, lg, lbias, *, tm, tiles_per_batch):
    n_tok = x2.shape[0]
    tok = pl.BlockSpec((tm, D_MODEL), lambda i: (i, 0))
    heads = pl.BlockSpec((N_HEADS, tm, HEAD_DIM), lambda i: (0, i, 0))
    per_tile = feat_h.shape[3] // tm
    col = lambda j: pl.BlockSpec((None, None, N_HEADS, tm, HEAD_DIM),
                                 lambda i, j=j: (i // per_tile, j, 0, i % per_tile, 0))
    full = lambda shape: pl.BlockSpec(shape, lambda i: (0,) * len(shape), pipeline_mode=pl.Buffered(1))
    return pl.pallas_call(
        _merge_kernel,
        grid=(n_tok // tm,),
        in_specs=[heads, heads, col(H_G4), col(H_G6), col(H_M7), col(H_M8), tok,
                  pl.BlockSpec((None, 3, D_MODEL), lambda i: (i // tiles_per_batch, 0, 0)),
                  full((1, HEAD_DIM)), full((D_MODEL, D_MODEL)), full((D_MODEL, D_MODEL)),
                  full((D_MODEL, D_MODEL)), full((1, D_MODEL)), full((1, D_MODEL))],
        out_specs=tok,
        out_shape=jax.ShapeDtypeStruct((n_tok, D_MODEL), F32),
        compiler_params=pltpu.CompilerParams(
            dimension_semantics=("arbitrary",), vmem_limit_bytes=VMEM_LIMIT),
        name="merge",
    )(oa, hx, feat_h, feat_h, feat_h, feat_h, x2, mod3, ng, pa, pb, wo, lg, lbias)


def kernel(x, c, ctx, c_ctx, w_mod, b_mod, w_in, b_in, lb_logits, norm_a_g, conv_w, conv_b,
           w_r, b_r, w_i, b_i, lam, p_a, p_b, w_out, ln_g, ln_b):
    batch, seq, d = x.shape
    n_ctx = ctx.shape[1]
    assert d == D_MODEL and seq % GRID_W == 0 and w_in.shape[0] == 1

    w_bf = w_in[0].astype(BF16)
    b2 = b_in[0][None, :]
    wg = (0.5 * jnp.concatenate([w_r[0, 0], w_i[0, 0], w_r[0, 1], w_i[0, 1]], axis=-1)).astype(BF16)
    assert batch < SUBLANES
    mod, lb = _mod_call(c, c_ctx[None, :], w_mod[0], b_mod[0][None, :], lb_logits)
    mod3 = mod.reshape(SUBLANES, 3, D_MODEL)

    x2 = x.reshape(batch * seq, D_MODEL)
    tm = INPROJ_TM
    feat = _inproj_call(x2, mod3, w_bf, b2, LATENT_F32, ("q", "forget0", "forget1", None), lb,
                        out_dtype=F32, tm=tm, tiles_per_mod=seq // tm)
    feat_h = _inproj_call(x2, mod3, w_bf, b2, LATENT_BF16, (None, "silu", "silu", "sig", "sig"),
                          out_dtype=BF16, tm=tm, tiles_per_mod=seq // tm, head_major=True)
    feat_c = _inproj_call(ctx.reshape(batch * n_ctx, D_MODEL), mod3[batch:batch + 1], w_bf, b2, CTX_F32,
                          ("forget0", "forget1", None, None), lb,
                          out_dtype=F32, tm=n_ctx, tiles_per_mod=batch)

    cw, cb = conv_w[0], conv_b[0][None, :]
    s0f, s0b, h0f, h0b = _ctx_call(feat_c, cw, cb, wg, b_r[0], b_i[0], lam[0], batch=batch, n_ctx=n_ctx)

    o_b = _gla_call(feat, feat_h, F_F1, s0b, reverse=True, batch=batch, seq=seq, rows=GLA_ROWS)
    oa = _gla_call(feat, feat_h, F_F0, s0f, o_b, reverse=False, batch=batch, seq=seq, rows=GLA_ROWS)
    hx = _lru_call(feat, cw, cb, wg, b_r[0], b_i[0], lam[0], h0f, h0b, batch=batch, seq=seq)

    out = _merge_call(oa, hx, feat_h, x2, mod3, norm_a_g[0][None, :],
                      p_a[0].astype(BF16), p_b[0].astype(BF16), w_out[0].astype(BF16),
                      ln_g[0][None, :], ln_b[0][None, :], tm=MERGE_TM, tiles_per_batch=seq // MERGE_TM)
    return out.reshape(batch, seq, D_MODEL)
```

```python
import functools

import jax
import jax.numpy as jnp
from jax import lax
from jax.experimental import pallas as pl
from jax.experimental.pallas import tpu as pltpu

F32 = jnp.float32
BF16 = jnp.bfloat16

D_MODEL = 1024
GRID_W = 64
HEAD_DIM = 128
N_HEADS = D_MODEL // HEAD_DIM
N_BLOCKS = 8
BLOCK_DIM = D_MODEL // N_BLOCKS
N_CONV = 4
RG_C = 8.0
LN_EPS = 1e-5
RMS_EPS = 1e-6
DEEPNORM_ALPHA = 2.0 ** 0.25
Q_SCALE = HEAD_DIM ** -0.5

V7X_VMEM_BYTES = 64 * 1024 * 1024
VMEM_LIMIT = V7X_VMEM_BYTES - 8 * 1024 * 1024
SUBLANES = 8

INPROJ_TM = 1024
GLA_ROWS = 16
LRU_ROWS_PER_STEP = 8
MERGE_TM = 512

LATENT_F32 = (0, 1, 2, 5)
LATENT_BF16 = (4, 6, 7, 8, 3)
CTX_F32 = (1, 2, 3, 5)
F_Q, F_F0, F_F1, F_Z5 = 0, 1, 2, 3
H_G4, H_G6, H_M7, H_M8, H_V = 0, 1, 2, 3, 4
HALVED_BLOCKS = (0, 1, 2, 4, 6, 7, 8)


def _silu(x):
    return _half_silu(0.5 * x)


def _half_silu(h):
    return h * jnp.tanh(h) + h


def _half_sigmoid(h):
    return 0.5 * jnp.tanh(h) + 0.5


def _half_forget(h, lb):
    return (0.5 + 0.5 * lb) + (0.5 - 0.5 * lb) * jnp.tanh(h)


def _dot(a, b):
    return jnp.dot(a, b, preferred_element_type=F32)


def _dot_tn(a, b):
    return lax.dot_general(a, b, (((0,), (0,)), ((), ())), preferred_element_type=F32)


def _chunk_cumprod(x, reverse):
    n, c = x.shape
    nb = n // 8
    y = x.reshape(nb, 8, c)
    sub = lax.broadcasted_iota(jnp.int32, (nb, 8, c), 1)
    for s in (1, 2, 4):
        if reverse:
            y = y * jnp.where(sub < 8 - s, pltpu.roll(y, 8 - s, axis=1), 1.0)
        else:
            y = y * jnp.where(sub >= s, pltpu.roll(y, s, axis=1), 1.0)
    offs = [None] * nb
    acc = jnp.ones((1, c), F32)
    for b in (reversed(range(nb)) if reverse else range(nb)):
        offs[b] = acc
        acc = acc * (y[b, 0:1, :] if reverse else y[b, 7:8, :])
    return (y * jnp.stack(offs)).reshape(n, c), acc


def _cumsum_rows(mask_bf16, x):
    hi = x.astype(BF16)
    r1 = x - hi.astype(F32)
    mid = r1.astype(BF16)
    lo = (r1 - mid.astype(F32)).astype(BF16)
    return _dot(mask_bf16, hi) + _dot(mask_bf16, mid) + _dot(mask_bf16, lo)


def _mod_kernel(c_ref, cc_ref, w_ref, b_ref, lbl_ref, mod_ref, lb_ref, rows_ref):
    n = c_ref.shape[0]
    rows_ref[...] = jnp.zeros(rows_ref.shape, F32)
    rows_ref[0:n, :] = c_ref[...]
    rows_ref[n:n + 1, :] = cc_ref[...]
    mod_ref[...] = _dot(_silu(rows_ref[...]), w_ref[...]) + b_ref[...]
    l = lbl_ref[...]
    e = jnp.exp(l - jnp.max(l, axis=0, keepdims=True))
    lb_ref[...] = e[0] / jnp.sum(e, axis=0)


def _mod_call(c, c_ctx, w_mod, b_mod, lb_logits):
    return pl.pallas_call(
        _mod_kernel,
        out_shape=(jax.ShapeDtypeStruct((SUBLANES, 3 * D_MODEL), F32),
                   jax.ShapeDtypeStruct((2, D_MODEL), F32)),
        scratch_shapes=[pltpu.VMEM((SUBLANES, D_MODEL), F32)],
        compiler_params=pltpu.CompilerParams(vmem_limit_bytes=VMEM_LIMIT),
        name="mod",
    )(c, c_ctx, w_mod, b_mod, lb_logits)


def _inproj_kernel(x_ref, mod_ref, *refs, head_major, acts):
    n_col = len(acts)
    if any(a and a.startswith("forget") for a in acts):
        lb_ref, refs = refs[0], refs[1:]
    w_refs, b_refs, o_ref = refs[:n_col], refs[n_col:2 * n_col], refs[-1]
    u = (x_ref[...] * (1.0 + mod_ref[1:2, :]) + mod_ref[0:1, :]).astype(BF16)
    for j in range(n_col):
        z = _dot(u, w_refs[j][...]) + b_refs[j][...]
        if acts[j] == "q":
            z = _half_silu(z) * Q_SCALE
        elif acts[j] == "silu":
            z = _half_silu(z)
        elif acts[j] == "sig":
            z = _half_sigmoid(z)
        elif acts[j] is not None:
            d = int(acts[j][-1])
            z = _half_forget(z, lb_ref[d:d + 1, :])
        z = z.astype(o_ref.dtype)
        if head_major:
            for h in range(N_HEADS):
                o_ref[j, h] = z[:, h * HEAD_DIM:(h + 1) * HEAD_DIM]
        else:
            o_ref[j] = z


def _inproj_call(x2, mod3, w, b, cols, acts, lb=None, *, out_dtype, tm, tiles_per_mod, head_major=False):
    n_tok = x2.shape[0]
    n_col = len(cols)
    tile = (N_HEADS, tm, HEAD_DIM) if head_major else (tm, D_MODEL)
    col_block = lambda rows: [pl.BlockSpec((rows, D_MODEL), lambda i, c=c: (0, c), pipeline_mode=pl.Buffered(1))
                              for c in cols]
    return pl.pallas_call(
        functools.partial(_inproj_kernel, head_major=head_major, acts=acts),
        grid=(n_tok // tm,),
        in_specs=[
            pl.BlockSpec((tm, D_MODEL), lambda i: (i, 0)),
            pl.BlockSpec((None, 3, D_MODEL), lambda i: (i // tiles_per_mod, 0, 0)),
            *([] if lb is None else [pl.BlockSpec((2, D_MODEL), lambda i: (0, 0))]),
            *col_block(D_MODEL),
            *col_block(1),
        ],
        out_specs=pl.BlockSpec((None, n_col) + tile, lambda i: (i, 0) + (0,) * len(tile)),
        out_shape=jax.ShapeDtypeStruct((n_tok // tm, n_col) + tile, out_dtype),
        compiler_params=pltpu.CompilerParams(
            dimension_semantics=("arbitrary",), vmem_limit_bytes=VMEM_LIMIT),
        name="inproj",
    )(x2, mod3, *([] if lb is None else [lb]), *([w] * n_col), *([b] * n_col))


LOG2_E = 1.4426950408889634


def _lru_ab(xc, h_r, h_i, br, bi, sp):
    k = (-0.5 * RG_C * LOG2_E) * sp
    a = jnp.exp2(k * jnp.tanh(h_r + 0.5 * br) + k)
    i = 0.5 * jnp.tanh(h_i + 0.5 * bi) + 0.5
    y = 1.0 - a * a
    mult = jnp.where(y > 0.0, y * lax.rsqrt(y), 0.0)
    return a, mult * (i * xc)


def _softplus(y):
    return jnp.maximum(y, 0.0) + jnp.log(1.0 + jnp.exp(-jnp.abs(y)))


def _ctx_kernel(ff_ref, fb_ref, v_ref, z5_ref, cw_ref, cb_ref, wg_ref, br_ref, bi_ref, lam_ref,
                sf_ref, sb_ref, hf_ref, hb_ref, zp_ref, a_ref, b_ref):
    n = ff_ref.shape[0]
    ri = lax.broadcasted_iota(jnp.int32, (n, n), 0)
    ci = lax.broadcasted_iota(jnp.int32, (n, n), 1)
    tril = (ci <= ri).astype(F32).astype(BF16)
    v = v_ref[...].astype(BF16)

    f = ff_ref[...]
    g = _cumsum_rows(tril, jnp.log(f))
    ke = ((1.0 - f) * jnp.exp(g[n - 1:n, :] - g)).astype(BF16)
    for h in range(N_HEADS):
        sl = slice(h * HEAD_DIM, (h + 1) * HEAD_DIM)
        sf_ref[h] = _dot_tn(v[:, sl], ke[:, sl])
    f = fb_ref[...]
    lf = jnp.log(f)
    ke = ((1.0 - f) * jnp.exp(_cumsum_rows(tril, lf) - lf)).astype(BF16)
    for h in range(N_HEADS):
        sl = slice(h * HEAD_DIM, (h + 1) * HEAD_DIM)
        sb_ref[h] = _dot_tn(v[:, sl], ke[:, sl])

    pad = SUBLANES
    zp_ref[0:pad, :] = jnp.zeros((pad, D_MODEL), F32)
    zp_ref[pad + n:2 * pad + n, :] = jnp.zeros((pad, D_MODEL), F32)
    zp_ref[pad:pad + n, :] = z5_ref[...]
    first = pad - (N_CONV - 1) // 2
    xc = cb_ref[...] + zp_ref[first:first + n, :] * cw_ref[0:1, :]
    for kk in range(1, N_CONV):
        xc = xc + zp_ref[first + kk:first + kk + n, :] * cw_ref[kk:kk + 1, :]

    for d, h_ref in ((0, hf_ref), (1, hb_ref)):
        sp = _softplus(-lam_ref[d:d + 1, :])
        for blk in range(N_BLOCKS):
            sl = slice(blk * BLOCK_DIM, (blk + 1) * BLOCK_DIM)
            g = _dot(xc[:, sl].astype(BF16), wg_ref[blk, :, 2 * d * BLOCK_DIM:2 * (d + 1) * BLOCK_DIM])
            a, b = _lru_ab(xc[:, sl], g[:, :BLOCK_DIM], g[:, BLOCK_DIM:],
                           br_ref[d:d + 1, sl], bi_ref[d:d + 1, sl], sp[:, sl])
            a_ref[:, sl] = a
            b_ref[:, sl] = b

        def step(t, h, d=d):
            tt = (n - 1 - t) if d == 1 else t
            return a_ref[pl.ds(tt, 1), :] * h + b_ref[pl.ds(tt, 1), :]

        h_ref[...] = lax.fori_loop(0, n, step, jnp.zeros((1, D_MODEL), F32), unroll=8)


def _ctx_call(feat_c, conv_w, conv_b, wg, br, bi, lam, *, batch, n_ctx):
    feat = lambda j: pl.BlockSpec((None, None, n_ctx, D_MODEL), lambda b, j=j: (b, j, 0, 0))
    full = lambda shape: pl.BlockSpec(shape, lambda b: (0,) * len(shape))
    state = pl.BlockSpec((None, N_HEADS, HEAD_DIM, HEAD_DIM), lambda b: (b, 0, 0, 0))
    hvec = pl.BlockSpec((None, 1, D_MODEL), lambda b: (b, 0, 0))
    return pl.pallas_call(
        _ctx_kernel,
        grid=(batch,),
        in_specs=[feat(0), feat(1), feat(2), feat(3),
                  full((N_CONV, D_MODEL)), full((1, D_MODEL)),
                  full((N_BLOCKS, BLOCK_DIM, 4 * BLOCK_DIM)),
                  full((2, D_MODEL)), full((2, D_MODEL)), full((2, D_MODEL))],
        out_specs=(state, state, hvec, hvec),
        out_shape=(jax.ShapeDtypeStruct((batch, N_HEADS, HEAD_DIM, HEAD_DIM), F32),
                   jax.ShapeDtypeStruct((batch, N_HEADS, HEAD_DIM, HEAD_DIM), F32),
                   jax.ShapeDtypeStruct((batch, 1, D_MODEL), F32),
                   jax.ShapeDtypeStruct((batch, 1, D_MODEL), F32)),
        scratch_shapes=[pltpu.VMEM((n_ctx + 2 * SUBLANES, D_MODEL), F32),
                        pltpu.VMEM((n_ctx, D_MODEL), F32),
                        pltpu.VMEM((n_ctx, D_MODEL), F32)],
        compiler_params=pltpu.CompilerParams(
            dimension_semantics=("arbitrary",), vmem_limit_bytes=VMEM_LIMIT),
        name="ctx_states",
    )(feat_c, feat_c, feat_c, feat_c, conv_w, conv_b, wg, br, bi, lam)


GLA_ROWS_PER_ITER = 16


def _gla_kernel(*refs, reverse, rows, finish):
    if finish:
        (q_ref, f_ref, v_ref, s0_ref, other_ref, o_ref,
         st_ref, sn_ref, qd_ref, kit_ref, ke_ref, dec_ref) = refs
    else:
        q_ref, f_ref, v_ref, s0_ref, o_ref, st_ref, sn_ref, qd_ref, kit_ref, ke_ref, dec_ref = refs

    @pl.when(pl.program_id(1) == 0)
    def _():
        st_ref[...] = s0_ref[...]
        for h in range(N_HEADS):
            sn_ref[h] = s0_ref[h].astype(BF16).T

    ri = lax.broadcasted_iota(jnp.int32, (GRID_W, GRID_W), 0)
    ci = lax.broadcasted_iota(jnp.int32, (GRID_W, GRID_W), 1)
    allow = (ci >= ri) if reverse else (ci <= ri)

    def row_offset(n):
        n = jnp.minimum(n, rows - 1)
        return pl.multiple_of(((rows - 1 - n) if reverse else n) * GRID_W, GRID_W)

    def prepare(n, slot):
        off = row_offset(n)
        q = q_ref[pl.ds(off, GRID_W), :]
        f = f_ref[pl.ds(off, GRID_W), :]
        dg, dec = _chunk_cumprod(f, reverse)
        ki = (1.0 - f) / dg
        qd = (q * dg).astype(BF16)
        ke = (ki * dec).astype(BF16)
        kit_ref[slot] = ki.astype(BF16).T
        dec_ref[slot] = dec
        for h in range(N_HEADS):
            sl = slice(h * HEAD_DIM, (h + 1) * HEAD_DIM)
            qd_ref[slot, h] = qd[:, sl]
            ke_ref[slot, h] = ke[:, sl]

    def contract(n, slot):
        off = row_offset(n)
        for h in range(N_HEADS):
            sl = slice(h * HEAD_DIM, (h + 1) * HEAD_DIM)
            qd = qd_ref[slot, h]
            v = v_ref[h, pl.ds(off, GRID_W), :]
            p = jnp.where(allow, _dot(qd, kit_ref[slot, sl, :]), 0.0).astype(BF16)
            o = _dot(p, v) + _dot(qd, sn_ref[h])
            st = st_ref[h] * dec_ref[slot, :, sl] + _dot_tn(v, ke_ref[slot, h])
            st_ref[h] = st
            sn_ref[h] = st.astype(BF16).T
            if finish:
                o = o + other_ref[h, pl.ds(off, GRID_W), :].astype(F32)
            o_ref[h, pl.ds(off, GRID_W), :] = o.astype(o_ref.dtype)

    prepare(0, 0)

    def body(m, carry):
        n = GLA_ROWS_PER_ITER * m
        for i in range(GLA_ROWS_PER_ITER):
            contract(n + i, i % 2)
            prepare(n + i + 1, (i + 1) % 2)
        return carry

    lax.fori_loop(0, rows // GLA_ROWS_PER_ITER, body, 0)


def _gla_call(feat, feat_h, f_col, s0, other=None, *, reverse, batch, seq, rows):
    tb = rows * GRID_W
    nrb = seq // tb
    finish = other is not None
    per_tile = feat.shape[2] // tb

    def row_block(b, i):
        return b * nrb + ((nrb - 1 - i) if reverse else i)

    def col(c):
        return pl.BlockSpec((None, None, tb, D_MODEL), lambda b, i, c=c: (
            row_block(b, i) // per_tile, c, row_block(b, i) % per_tile, 0))

    tok = pl.BlockSpec((N_HEADS, tb, HEAD_DIM), lambda b, i: (0, row_block(b, i), 0))
    v_spec = pl.BlockSpec((None, None, N_HEADS, tb, HEAD_DIM), lambda b, i: (
        row_block(b, i) // per_tile, H_V, 0, row_block(b, i) % per_tile, 0))
    in_specs = [col(F_Q), col(f_col), v_spec,
                pl.BlockSpec((None, N_HEADS, HEAD_DIM, HEAD_DIM), lambda b, i: (b, 0, 0, 0))]
    args = [feat, feat, feat_h, s0]
    if finish:
        in_specs += [tok]
        args += [other]
    return pl.pallas_call(
        functools.partial(_gla_kernel, reverse=reverse, rows=rows, finish=finish),
        grid=(batch, nrb),
        in_specs=in_specs,
        out_specs=tok,
        out_shape=jax.ShapeDtypeStruct((N_HEADS, batch * seq, HEAD_DIM), BF16),
        scratch_shapes=[pltpu.VMEM((N_HEADS, HEAD_DIM, HEAD_DIM), F32)]
        + [pltpu.VMEM((N_HEADS, HEAD_DIM, HEAD_DIM), BF16), pltpu.VMEM((2, N_HEADS, GRID_W, HEAD_DIM), BF16),
           pltpu.VMEM((2, D_MODEL, GRID_W), BF16), pltpu.VMEM((2, N_HEADS, GRID_W, HEAD_DIM), BF16),
           pltpu.VMEM((2, 1, D_MODEL), F32)],
        compiler_params=pltpu.CompilerParams(
            dimension_semantics=("arbitrary", "arbitrary"), vmem_limit_bytes=VMEM_LIMIT),
        name="gla_bwd" if reverse else "gla_fwd",
    )(*args)


def _lru_kernel(z5_ref, cw_ref, cb_ref, wg_ref, br_ref, bi_ref, lam_ref, h0f_ref, h0b_ref, o_ref,
                zp_ref, af_ref, bf_ref, ab_ref, bb_ref, cf_ref, cr_ref, *, n_rows, rows_per_step):
    w = GRID_W
    seq = n_rows * w
    tb = rows_per_step * w
    n_steps = n_rows // rows_per_step

    zp_ref[0:w, :] = jnp.zeros((w, BLOCK_DIM), F32)
    zp_ref[w + seq:, :] = jnp.zeros((2 * w, BLOCK_DIM), F32)
    tile = z5_ref.shape[1]
    for k in range(z5_ref.shape[0]):
        zp_ref[w + k * tile:w + (k + 1) * tile, :] = z5_ref[k]

    sp = [_softplus(-lam_ref[d:d + 1, :]) for d in (0, 1)]

    def gates(s, c):
        off = pl.multiple_of(s * tb, tb)
        xc = cb_ref[...] + zp_ref[pl.ds(off, tb), :] * cw_ref[0:1, :]
        for kk in range(1, N_CONV):
            xc = xc + zp_ref[pl.ds(off + kk * w, tb), :] * cw_ref[kk:kk + 1, :]
        g = _dot(xc.astype(BF16), wg_ref[...])
        for d, a_ref, b_ref in ((0, af_ref, bf_ref), (1, ab_ref, bb_ref)):
            c0 = 2 * d * BLOCK_DIM
            a, b = _lru_ab(xc, g[:, c0:c0 + BLOCK_DIM], g[:, c0 + BLOCK_DIM:c0 + 2 * BLOCK_DIM],
                           br_ref[d:d + 1, :], bi_ref[d:d + 1, :], sp[d])
            a_ref[pl.ds(off, tb), :] = a
            b_ref[pl.ds(off, tb), :] = b
        return c

    lax.fori_loop(0, n_steps, gates, 0, unroll=4)

    def scan(n, carry):
        hf, pf, hb, pb = carry
        off_f = pl.multiple_of(n * w, w)
        off_b = pl.multiple_of((n_rows - 1 - n) * w, w)
        a = af_ref[pl.ds(off_f, w), :]
        hf = a * hf + bf_ref[pl.ds(off_f, w), :]
        pf = pf * a
        bf_ref[pl.ds(off_f, w), :] = hf
        af_ref[pl.ds(off_f, w), :] = pf
        a = ab_ref[pl.ds(off_b, w), :]
        hb = a * hb + bb_ref[pl.ds(off_b, w), :]
        pb = pb * a
        bb_ref[pl.ds(off_b, w), :] = hb
        ab_ref[pl.ds(off_b, w), :] = pb
        return hf, pf, hb, pb

    zeros = jnp.zeros((w, BLOCK_DIM), F32)
    ones = jnp.ones((w, BLOCK_DIM), F32)
    lax.fori_loop(0, n_rows, scan, (zeros, ones, zeros, ones), unroll=16)

    last = (n_rows - 1) * w

    def carry(n, c):
        cf, cb = c
        cf_ref[pl.ds(n, 1), :] = cf
        cf = bf_ref[pl.ds(last + n, 1), :] + af_ref[pl.ds(last + n, 1), :] * cf
        col = w - 1 - n
        cr_ref[pl.ds(col, 1), :] = cb
        cb = bb_ref[pl.ds(col, 1), :] + ab_ref[pl.ds(col, 1), :] * cb
        return cf, cb

    lax.fori_loop(0, w, carry, (h0f_ref[...], h0b_ref[...]), unroll=8)

    def fix(s, c):
        off = pl.multiple_of(s * tb, tb)
        cf = jnp.concatenate([cf_ref[...]] * rows_per_step, axis=0)
        cr = jnp.concatenate([cr_ref[...]] * rows_per_step, axis=0)
        h = (bf_ref[pl.ds(off, tb), :] + af_ref[pl.ds(off, tb), :] * cf
             + bb_ref[pl.ds(off, tb), :] + ab_ref[pl.ds(off, tb), :] * cr)
        o_ref[pl.ds(off, tb), :] = h.astype(o_ref.dtype)
        return c

    lax.fori_loop(0, n_steps, fix, 0, unroll=4)


def _lru_call(feat, conv_w, conv_b, wg, br, bi, lam, h0f, h0b, *, batch, seq):
    n_rows = seq // GRID_W
    vec = lambda n: pl.BlockSpec((n, BLOCK_DIM), lambda b, c: (0, c))
    h0 = pl.BlockSpec((None, 1, BLOCK_DIM), lambda b, c: (b, 0, c))
    buf = pltpu.VMEM((seq, BLOCK_DIM), F32)
    n_i, n_col, tm, _ = feat.shape
    feat5 = feat.reshape(batch, n_i // batch, n_col, tm, D_MODEL)
    return pl.pallas_call(
        functools.partial(_lru_kernel, n_rows=n_rows, rows_per_step=LRU_ROWS_PER_STEP),
        grid=(batch, N_BLOCKS),
        in_specs=[pl.BlockSpec((None, n_i // batch, None, tm, BLOCK_DIM), lambda b, c: (b, 0, F_Z5, 0, c)),
                  vec(N_CONV), vec(1),
                  pl.BlockSpec((None, BLOCK_DIM, 4 * BLOCK_DIM), lambda b, c: (c, 0, 0)),
                  vec(2), vec(2), vec(2), h0, h0],
        out_specs=pl.BlockSpec((None, seq, BLOCK_DIM), lambda b, c: (c, b, 0)),
        out_shape=jax.ShapeDtypeStruct((N_BLOCKS, batch * seq, BLOCK_DIM), BF16),
        scratch_shapes=[pltpu.VMEM((seq + 3 * GRID_W, BLOCK_DIM), F32), buf, buf, buf, buf,
                        pltpu.VMEM((GRID_W, BLOCK_DIM), F32), pltpu.VMEM((GRID_W, BLOCK_DIM), F32)],
        compiler_params=pltpu.CompilerParams(
            dimension_semantics=("arbitrary", "arbitrary"), vmem_limit_bytes=VMEM_LIMIT),
        name="lru",
    )(feat5, conv_w, conv_b, wg, br, bi, lam, h0f, h0b)


def _merge_kernel(oa_ref, hx_ref, g4_ref, g6_ref, m7_ref, m8_ref, x_ref, mod_ref, ng_ref,
                  pa_ref, pb_ref, wo_ref, lg_ref, lbias_ref, o_ref):
    f32 = lambda ref: jnp.concatenate([ref[h] for h in range(N_HEADS)], axis=-1).astype(F32)
    o_b = (f32(hx_ref) * f32(g6_ref)).astype(BF16)
    y = f32(m8_ref) * _dot(o_b, pb_ref[...])
    y_a = None
    for pair in range(N_HEADS // 2):
        o_h = []
        for h in (2 * pair, 2 * pair + 1):
            t = oa_ref[h].astype(F32)
            ms = jnp.mean(t * t, axis=-1, keepdims=True)
            o_h.append((t * lax.rsqrt(ms + RMS_EPS) * ng_ref[...] * g4_ref[h].astype(F32)).astype(BF16))
        rows = slice(2 * pair * HEAD_DIM, 2 * (pair + 1) * HEAD_DIM)
        part = _dot(jnp.concatenate(o_h, axis=-1), pa_ref[rows, :])
        y_a = part if y_a is None else y_a + part
    y = y + f32(m7_ref) * y_a
    y = _dot(y.astype(BF16), wo_ref[...])
    t = DEEPNORM_ALPHA * x_ref[...] + mod_ref[2:3, :] * y
    mu = jnp.mean(t, axis=-1, keepdims=True)
    tc = t - mu
    var = jnp.mean(tc * tc, axis=-1, keepdims=True)
    o_ref[...] = tc * lax.rsqrt(var + LN_EPS) * lg_ref[...] + lbias_ref[...]


def _merge_call(oa, hx, feat_h, x2, mod3, ng, pa, pb, wo, lg, lbias, *, tm, tiles_per_batch):
    n_tok = x2.shape[0]
    tok = pl.BlockSpec((tm, D_MODEL), lambda i: (i, 0))
    heads = pl.BlockSpec((N_HEADS, tm, HEAD_DIM), lambda i: (0, i, 0))
    per_tile = feat_h.shape[3] // tm
    col = lambda j: pl.BlockSpec((None, None, N_HEADS, tm, HEAD_DIM),
                                 lambda i, j=j: (i // per_tile, j, 0, i % per_tile, 0))
    full = lambda shape: pl.BlockSpec(shape, lambda i: (0,) * len(shape), pipeline_mode=pl.Buffered(1))
    return pl.pallas_call(
        _merge_kernel,
        grid=(n_tok // tm,),
        in_specs=[heads, heads, col(H_G4), col(H_G6), col(H_M7), col(H_M8), tok,
                  pl.BlockSpec((None, 3, D_MODEL), lambda i: (i // tiles_per_batch, 0, 0)),
                  full((1, HEAD_DIM)), full((D_MODEL, D_MODEL)), full((D_MODEL, D_MODEL)),
                  full((D_MODEL, D_MODEL)), full((1, D_MODEL)), full((1, D_MODEL))],
        out_specs=tok,
        out_shape=jax.ShapeDtypeStruct((n_tok, D_MODEL), F32),
        compiler_params=pltpu.CompilerParams(
            dimension_semantics=("arbitrary",), vmem_limit_bytes=VMEM_LIMIT),
        name="merge",
    )(oa, hx, feat_h, feat_h, feat_h, feat_h, x2, mod3, ng, pa, pb, wo, lg, lbias)


def kernel(x, c, ctx, c_ctx, w_mod, b_mod, w_in, b_in, lb_logits, norm_a_g, conv_w, conv_b,
           w_r, b_r, w_i, b_i, lam, p_a, p_b, w_out, ln_g, ln_b):
    batch, seq, d = x.shape
    n_ctx = ctx.shape[1]
    assert d == D_MODEL and seq % GRID_W == 0 and w_in.shape[0] == 1

    n_blocks = w_in.shape[-1] // D_MODEL
    col_scale = jnp.repeat(jnp.array([0.5 if j in HALVED_BLOCKS else 1.0 for j in range(n_blocks)], F32), D_MODEL)
    w_bf = (w_in[0] * col_scale).astype(BF16)
    b2 = (b_in[0] * col_scale)[None, :]
    wg = (0.5 * jnp.concatenate([w_r[0, 0], w_i[0, 0], w_r[0, 1], w_i[0, 1]], axis=-1)).astype(BF16)
    assert batch < SUBLANES
    mod, lb = _mod_call(c, c_ctx[None, :], w_mod[0], b_mod[0][None, :], lb_logits)
    mod3 = mod.reshape(SUBLANES, 3, D_MODEL)

    x2 = x.reshape(batch * seq, D_MODEL)
    tm = INPROJ_TM
    feat = _inproj_call(x2, mod3, w_bf, b2, LATENT_F32, ("q", "forget0", "forget1", None), lb,
                        out_dtype=F32, tm=tm, tiles_per_mod=seq // tm)
    feat_h = _inproj_call(x2, mod3, w_bf, b2, LATENT_BF16, ("silu", "silu", "sig", "sig", None),
                          out_dtype=BF16, tm=tm, tiles_per_mod=seq // tm, head_major=True)
    feat_c = _inproj_call(ctx.reshape(batch * n_ctx, D_MODEL), mod3[batch:batch + 1], w_bf, b2, CTX_F32,
                          ("forget0", "forget1", None, None), lb,
                          out_dtype=F32, tm=n_ctx, tiles_per_mod=batch)

    cw, cb = conv_w[0], conv_b[0][None, :]
    s0f, s0b, h0f, h0b = _ctx_call(feat_c, cw, cb, wg, b_r[0], b_i[0], lam[0], batch=batch, n_ctx=n_ctx)

    o_b = _gla_call(feat, feat_h, F_F1, s0b, reverse=True, batch=batch, seq=seq, rows=GLA_ROWS)
    oa = _gla_call(feat, feat_h, F_F0, s0f, o_b, reverse=False, batch=batch, seq=seq, rows=GLA_ROWS)
    hx = _lru_call(feat, cw, cb, wg, b_r[0], b_i[0], lam[0], h0f, h0b, batch=batch, seq=seq)

    out = _merge_call(oa, hx, feat_h, x2, mod3, norm_a_g[0][None, :],
                      p_a[0].astype(BF16), p_b[0].astype(BF16), w_out[0].astype(BF16),
                      ln_g[0][None, :], ln_b[0][None, :], tm=MERGE_TM, tiles_per_batch=seq // MERGE_TM)
    return out.reshape(batch, seq, D_MODEL)
```

```python
import functools

import jax
import jax.numpy as jnp
from jax import lax
from jax.experimental import pallas as pl
from jax.experimental.pallas import tpu as pltpu

F32 = jnp.float32
BF16 = jnp.bfloat16

D_MODEL = 1024
GRID_W = 64
HEAD_DIM = 128
N_HEADS = D_MODEL // HEAD_DIM
N_BLOCKS = 8
BLOCK_DIM = D_MODEL // N_BLOCKS
N_CONV = 4
RG_C = 8.0
LN_EPS = 1e-5
RMS_EPS = 1e-6
DEEPNORM_ALPHA = 2.0 ** 0.25
Q_SCALE = HEAD_DIM ** -0.5

V7X_VMEM_BYTES = 64 * 1024 * 1024
VMEM_LIMIT = V7X_VMEM_BYTES - 8 * 1024 * 1024
SUBLANES = 8

INPROJ_TM = 1024
GLA_ROWS = 16
LRU_ROWS_PER_STEP = 8
MERGE_TM = 512

LATENT_F32 = (0, 1, 2, 5)
LATENT_BF16 = (4, 6, 7, 8, 3)
CTX_F32 = (1, 2, 3, 5)
F_Q, F_F0, F_F1, F_Z5 = 0, 1, 2, 3
H_G4, H_G6, H_M7, H_M8, H_V = 0, 1, 2, 3, 4
HALVED_BLOCKS = (0, 1, 2, 4, 6, 7, 8)


def _silu(x):
    return _half_silu(0.5 * x)


def _half_silu(h):
    return h * jnp.tanh(h) + h


def _half_sigmoid(h):
    return 0.5 * jnp.tanh(h) + 0.5


def _half_forget(h, lb):
    return (0.5 + 0.5 * lb) + (0.5 - 0.5 * lb) * jnp.tanh(h)


def _dot(a, b):
    return jnp.dot(a, b, preferred_element_type=F32)


def _dot_tn(a, b):
    return lax.dot_general(a, b, (((0,), (0,)), ((), ())), preferred_element_type=F32)


def _chunk_cumprod(x, reverse):
    n, c = x.shape
    nb = n // 8
    y = x.reshape(nb, 8, c)
    sub = lax.broadcasted_iota(jnp.int32, (nb, 8, c), 1)
    for s in (1, 2, 4):
        if reverse:
            y = y * jnp.where(sub < 8 - s, pltpu.roll(y, 8 - s, axis=1), 1.0)
        else:
            y = y * jnp.where(sub >= s, pltpu.roll(y, s, axis=1), 1.0)
    offs = [None] * nb
    acc = jnp.ones((1, c), F32)
    for b in (reversed(range(nb)) if reverse else range(nb)):
        offs[b] = acc
        acc = acc * (y[b, 0:1, :] if reverse else y[b, 7:8, :])
    return (y * jnp.stack(offs)).reshape(n, c), acc


def _cumsum_rows(mask_bf16, x):
    hi = x.astype(BF16)
    r1 = x - hi.astype(F32)
    mid = r1.astype(BF16)
    lo = (r1 - mid.astype(F32)).astype(BF16)
    return _dot(mask_bf16, hi) + _dot(mask_bf16, mid) + _dot(mask_bf16, lo)


def _mod_kernel(c_ref, cc_ref, w_ref, b_ref, lbl_ref, mod_ref, lb_ref, rows_ref):
    n = c_ref.shape[0]
    rows_ref[...] = jnp.zeros(rows_ref.shape, F32)
    rows_ref[0:n, :] = c_ref[...]
    rows_ref[n:n + 1, :] = cc_ref[...]
    mod_ref[...] = _dot(_silu(rows_ref[...]), w_ref[...]) + b_ref[...]
    l = lbl_ref[...]
    e = jnp.exp(l - jnp.max(l, axis=0, keepdims=True))
    lb_ref[...] = e[0] / jnp.sum(e, axis=0)


def _mod_call(c, c_ctx, w_mod, b_mod, lb_logits):
    return pl.pallas_call(
        _mod_kernel,
        out_shape=(jax.ShapeDtypeStruct((SUBLANES, 3 * D_MODEL), F32),
                   jax.ShapeDtypeStruct((2, D_MODEL), F32)),
        scratch_shapes=[pltpu.VMEM((SUBLANES, D_MODEL), F32)],
        compiler_params=pltpu.CompilerParams(vmem_limit_bytes=VMEM_LIMIT),
        name="mod",
    )(c, c_ctx, w_mod, b_mod, lb_logits)


def _inproj_kernel(x_ref, mod_ref, *refs, head_major, acts):
    n_col = len(acts)
    if any(a and a.startswith("forget") for a in acts):
        lb_ref, refs = refs[0], refs[1:]
    w_refs, b_refs, o_ref = refs[:n_col], refs[n_col:2 * n_col], refs[-1]
    u = (x_ref[...] * (1.0 + mod_ref[1:2, :]) + mod_ref[0:1, :]).astype(BF16)
    for j in range(n_col):
        z = _dot(u, w_refs[j][...]) + b_refs[j][...]
        if acts[j] == "q":
            z = _half_silu(z) * Q_SCALE
        elif acts[j] == "silu":
            z = _half_silu(z)
        elif acts[j] == "sig":
            z = _half_sigmoid(z)
        elif acts[j] is not None:
            d = int(acts[j][-1])
            z = _half_forget(z, lb_ref[d:d + 1, :])
        z = z.astype(o_ref.dtype)
        if head_major:
            for h in range(N_HEADS):
                o_ref[j, h] = z[:, h * HEAD_DIM:(h + 1) * HEAD_DIM]
        else:
            o_ref[j] = z


def _inproj_call(x2, mod3, w, b, cols, acts, lb=None, *, out_dtype, tm, tiles_per_mod, head_major=False):
    n_tok = x2.shape[0]
    n_col = len(cols)
    tile = (N_HEADS, tm, HEAD_DIM) if head_major else (tm, D_MODEL)
    col_block = lambda rows: [pl.BlockSpec((rows, D_MODEL), lambda i, c=c: (0, c), pipeline_mode=pl.Buffered(1))
                              for c in cols]
    return pl.pallas_call(
        functools.partial(_inproj_kernel, head_major=head_major, acts=acts),
        grid=(n_tok // tm,),
        in_specs=[
            pl.BlockSpec((tm, D_MODEL), lambda i: (i, 0)),
            pl.BlockSpec((None, 3, D_MODEL), lambda i: (i // tiles_per_mod, 0, 0)),
            *([] if lb is None else [pl.BlockSpec((2, D_MODEL), lambda i: (0, 0))]),
            *col_block(D_MODEL),
            *col_block(1),
        ],
        out_specs=pl.BlockSpec((None, n_col) + tile, lambda i: (i, 0) + (0,) * len(tile)),
        out_shape=jax.ShapeDtypeStruct((n_tok // tm, n_col) + tile, out_dtype),
        compiler_params=pltpu.CompilerParams(
            dimension_semantics=("arbitrary",), vmem_limit_bytes=VMEM_LIMIT),
        name="inproj",
    )(x2, mod3, *([] if lb is None else [lb]), *([w] * n_col), *([b] * n_col))


LOG2_E = 1.4426950408889634


def _lru_ab(hx, h_r, h_i, br, bi, sp):
    k = (-0.5 * RG_C * LOG2_E) * sp
    a = jnp.exp2(k * jnp.tanh(h_r + 0.5 * br) + k)
    y = 1.0 - a * a
    mult = jnp.where(y > 0.0, y * lax.rsqrt(y), 0.0)
    return a, mult * (hx * jnp.tanh(h_i + 0.5 * bi) + hx)


def _softplus(y):
    return jnp.maximum(y, 0.0) + jnp.log(1.0 + jnp.exp(-jnp.abs(y)))


def _ctx_kernel(ff_ref, fb_ref, v_ref, z5_ref, cw_ref, cb_ref, wg_ref, br_ref, bi_ref, lam_ref,
                sf_ref, sb_ref, hf_ref, hb_ref, zp_ref, a_ref, b_ref):
    n = ff_ref.shape[0]
    ri = lax.broadcasted_iota(jnp.int32, (n, n), 0)
    ci = lax.broadcasted_iota(jnp.int32, (n, n), 1)
    tril = (ci <= ri).astype(F32).astype(BF16)
    v = v_ref[...].astype(BF16)

    f = ff_ref[...]
    g = _cumsum_rows(tril, jnp.log(f))
    ke = ((1.0 - f) * jnp.exp(g[n - 1:n, :] - g)).astype(BF16)
    for h in range(N_HEADS):
        sl = slice(h * HEAD_DIM, (h + 1) * HEAD_DIM)
        sf_ref[h] = _dot_tn(v[:, sl], ke[:, sl])
    f = fb_ref[...]
    lf = jnp.log(f)
    ke = ((1.0 - f) * jnp.exp(_cumsum_rows(tril, lf) - lf)).astype(BF16)
    for h in range(N_HEADS):
        sl = slice(h * HEAD_DIM, (h + 1) * HEAD_DIM)
        sb_ref[h] = _dot_tn(v[:, sl], ke[:, sl])

    pad = SUBLANES
    zp_ref[0:pad, :] = jnp.zeros((pad, D_MODEL), F32)
    zp_ref[pad + n:2 * pad + n, :] = jnp.zeros((pad, D_MODEL), F32)
    zp_ref[pad:pad + n, :] = z5_ref[...]
    first = pad - (N_CONV - 1) // 2
    xc = cb_ref[...] + zp_ref[first:first + n, :] * cw_ref[0:1, :]
    for kk in range(1, N_CONV):
        xc = xc + zp_ref[first + kk:first + kk + n, :] * cw_ref[kk:kk + 1, :]

    for d, h_ref in ((0, hf_ref), (1, hb_ref)):
        sp = _softplus(-lam_ref[d:d + 1, :])
        for blk in range(N_BLOCKS):
            sl = slice(blk * BLOCK_DIM, (blk + 1) * BLOCK_DIM)
            g = _dot(xc[:, sl].astype(BF16), wg_ref[blk, :, 2 * d * BLOCK_DIM:2 * (d + 1) * BLOCK_DIM])
            a, b = _lru_ab(0.5 * xc[:, sl], g[:, :BLOCK_DIM], g[:, BLOCK_DIM:],
                           br_ref[d:d + 1, sl], bi_ref[d:d + 1, sl], sp[:, sl])
            a_ref[:, sl] = a
            b_ref[:, sl] = b

        def step(t, h, d=d):
            tt = (n - 1 - t) if d == 1 else t
            return a_ref[pl.ds(tt, 1), :] * h + b_ref[pl.ds(tt, 1), :]

        h_ref[...] = lax.fori_loop(0, n, step, jnp.zeros((1, D_MODEL), F32), unroll=8)


def _ctx_call(feat_c, conv_w, conv_b, wg, br, bi, lam, *, batch, n_ctx):
    feat = lambda j: pl.BlockSpec((None, None, n_ctx, D_MODEL), lambda b, j=j: (b, j, 0, 0))
    full = lambda shape: pl.BlockSpec(shape, lambda b: (0,) * len(shape))
    state = pl.BlockSpec((None, N_HEADS, HEAD_DIM, HEAD_DIM), lambda b: (b, 0, 0, 0))
    hvec = pl.BlockSpec((None, 1, D_MODEL), lambda b: (b, 0, 0))
    return pl.pallas_call(
        _ctx_kernel,
        grid=(batch,),
        in_specs=[feat(0), feat(1), feat(2), feat(3),
                  full((N_CONV, D_MODEL)), full((1, D_MODEL)),
                  full((N_BLOCKS, BLOCK_DIM, 4 * BLOCK_DIM)),
                  full((2, D_MODEL)), full((2, D_MODEL)), full((2, D_MODEL))],
        out_specs=(state, state, hvec, hvec),
        out_shape=(jax.ShapeDtypeStruct((batch, N_HEADS, HEAD_DIM, HEAD_DIM), F32),
                   jax.ShapeDtypeStruct((batch, N_HEADS, HEAD_DIM, HEAD_DIM), F32),
                   jax.ShapeDtypeStruct((batch, 1, D_MODEL), F32),
                   jax.ShapeDtypeStruct((batch, 1, D_MODEL), F32)),
        scratch_shapes=[pltpu.VMEM((n_ctx + 2 * SUBLANES, D_MODEL), F32),
                        pltpu.VMEM((n_ctx, D_MODEL), F32),
                        pltpu.VMEM((n_ctx, D_MODEL), F32)],
        compiler_params=pltpu.CompilerParams(
            dimension_semantics=("arbitrary",), vmem_limit_bytes=VMEM_LIMIT),
        name="ctx_states",
    )(feat_c, feat_c, feat_c, feat_c, conv_w, conv_b, wg, br, bi, lam)


GLA_ROWS_PER_ITER = 16


def _gla_kernel(*refs, reverse, rows, finish):
    if finish:
        (q_ref, f_ref, v_ref, s0_ref, other_ref, o_ref,
         st_ref, sn_ref, qd_ref, kit_ref, ke_ref, dec_ref) = refs
    else:
        q_ref, f_ref, v_ref, s0_ref, o_ref, st_ref, sn_ref, qd_ref, kit_ref, ke_ref, dec_ref = refs

    @pl.when(pl.program_id(1) == 0)
    def _():
        st_ref[...] = s0_ref[...]
        for h in range(N_HEADS):
            sn_ref[h] = s0_ref[h].astype(BF16).T

    ri = lax.broadcasted_iota(jnp.int32, (GRID_W, GRID_W), 0)
    ci = lax.broadcasted_iota(jnp.int32, (GRID_W, GRID_W), 1)
    allow = (ci >= ri) if reverse else (ci <= ri)

    def row_offset(n):
        n = jnp.minimum(n, rows - 1)
        return pl.multiple_of(((rows - 1 - n) if reverse else n) * GRID_W, GRID_W)

    def prepare(n, slot):
        off = row_offset(n)
        q = q_ref[pl.ds(off, GRID_W), :]
        f = f_ref[pl.ds(off, GRID_W), :]
        dg, dec = _chunk_cumprod(f, reverse)
        ki = (1.0 - f) / dg
        qd = (q * dg).astype(BF16)
        ke = (ki * dec).astype(BF16)
        kit_ref[slot] = ki.astype(BF16).T
        dec_ref[slot] = dec
        for h in range(N_HEADS):
            sl = slice(h * HEAD_DIM, (h + 1) * HEAD_DIM)
            qd_ref[slot, h] = qd[:, sl]
            ke_ref[slot, h] = ke[:, sl]

    def contract(n, slot):
        off = row_offset(n)
        for h in range(N_HEADS):
            sl = slice(h * HEAD_DIM, (h + 1) * HEAD_DIM)
            qd = qd_ref[slot, h]
            v = v_ref[h, pl.ds(off, GRID_W), :]
            p = jnp.where(allow, _dot(qd, kit_ref[slot, sl, :]), 0.0).astype(BF16)
            o = _dot(p, v) + _dot(qd, sn_ref[h])
            st = st_ref[h] * dec_ref[slot, :, sl] + _dot_tn(v, ke_ref[slot, h])
            st_ref[h] = st
            sn_ref[h] = st.astype(BF16).T
            if finish:
                o = o + other_ref[h, pl.ds(off, GRID_W), :].astype(F32)
            o_ref[h, pl.ds(off, GRID_W), :] = o.astype(o_ref.dtype)

    prepare(0, 0)

    def body(m, carry):
        n = GLA_ROWS_PER_ITER * m
        for i in range(GLA_ROWS_PER_ITER):
            contract(n + i, i % 2)
            prepare(n + i + 1, (i + 1) % 2)
        return carry

    lax.fori_loop(0, rows // GLA_ROWS_PER_ITER, body, 0)


def _gla_call(feat, feat_h, f_col, s0, other=None, *, reverse, batch, seq, rows):
    tb = rows * GRID_W
    nrb = seq // tb
    finish = other is not None
    per_tile = feat.shape[2] // tb

    def row_block(b, i):
        return b * nrb + ((nrb - 1 - i) if reverse else i)

    def col(c):
        return pl.BlockSpec((None, None, tb, D_MODEL), lambda b, i, c=c: (
            row_block(b, i) // per_tile, c, row_block(b, i) % per_tile, 0))

    tok = pl.BlockSpec((N_HEADS, tb, HEAD_DIM), lambda b, i: (0, row_block(b, i), 0))
    v_spec = pl.BlockSpec((None, None, N_HEADS, tb, HEAD_DIM), lambda b, i: (
        row_block(b, i) // per_tile, H_V, 0, row_block(b, i) % per_tile, 0))
    in_specs = [col(F_Q), col(f_col), v_spec,
                pl.BlockSpec((None, N_HEADS, HEAD_DIM, HEAD_DIM), lambda b, i: (b, 0, 0, 0))]
    args = [feat, feat, feat_h, s0]
    if finish:
        in_specs += [tok]
        args += [other]
    return pl.pallas_call(
        functools.partial(_gla_kernel, reverse=reverse, rows=rows, finish=finish),
        grid=(batch, nrb),
        in_specs=in_specs,
        out_specs=tok,
        out_shape=jax.ShapeDtypeStruct((N_HEADS, batch * seq, HEAD_DIM), BF16),
        scratch_shapes=[pltpu.VMEM((N_HEADS, HEAD_DIM, HEAD_DIM), F32)]
        + [pltpu.VMEM((N_HEADS, HEAD_DIM, HEAD_DIM), BF16), pltpu.VMEM((2, N_HEADS, GRID_W, HEAD_DIM), BF16),
           pltpu.VMEM((2, D_MODEL, GRID_W), BF16), pltpu.VMEM((2, N_HEADS, GRID_W, HEAD_DIM), BF16),
           pltpu.VMEM((2, 1, D_MODEL), F32)],
        compiler_params=pltpu.CompilerParams(
            dimension_semantics=("arbitrary", "arbitrary"), vmem_limit_bytes=VMEM_LIMIT),
        name="gla_bwd" if reverse else "gla_fwd",
    )(*args)


def _lru_kernel(z5_ref, cw_ref, cb_ref, wg_ref, br_ref, bi_ref, lam_ref, h0f_ref, h0b_ref, o_ref,
                zp_ref, af_ref, bf_ref, ab_ref, bb_ref, cf_ref, cr_ref, *, n_rows, rows_per_step):
    w = GRID_W
    seq = n_rows * w
    tb = rows_per_step * w
    n_steps = n_rows // rows_per_step

    zp_ref[0:w, :] = jnp.zeros((w, BLOCK_DIM), F32)
    zp_ref[w + seq:, :] = jnp.zeros((2 * w, BLOCK_DIM), F32)
    tile = z5_ref.shape[1]
    for k in range(z5_ref.shape[0]):
        zp_ref[w + k * tile:w + (k + 1) * tile, :] = z5_ref[k]

    sp = [_softplus(-lam_ref[d:d + 1, :]) for d in (0, 1)]

    def gates(s, c):
        off = pl.multiple_of(s * tb, tb)
        xc = cb_ref[...] + zp_ref[pl.ds(off, tb), :] * cw_ref[0:1, :]
        for kk in range(1, N_CONV):
            xc = xc + zp_ref[pl.ds(off + kk * w, tb), :] * cw_ref[kk:kk + 1, :]
        g = _dot(xc.astype(BF16), wg_ref[...])
        hx = 0.5 * xc
        for d, a_ref, b_ref in ((0, af_ref, bf_ref), (1, ab_ref, bb_ref)):
            c0 = 2 * d * BLOCK_DIM
            a, b = _lru_ab(hx, g[:, c0:c0 + BLOCK_DIM], g[:, c0 + BLOCK_DIM:c0 + 2 * BLOCK_DIM],
                           br_ref[d:d + 1, :], bi_ref[d:d + 1, :], sp[d])
            a_ref[pl.ds(off, tb), :] = a
            b_ref[pl.ds(off, tb), :] = b
        return c

    lax.fori_loop(0, n_steps, gates, 0, unroll=4)

    def scan(n, carry):
        hf, pf, hb, pb = carry
        off_f = pl.multiple_of(n * w, w)
        off_b = pl.multiple_of((n_rows - 1 - n) * w, w)
        a = af_ref[pl.ds(off_f, w), :]
        hf = a * hf + bf_ref[pl.ds(off_f, w), :]
        pf = pf * a
        bf_ref[pl.ds(off_f, w), :] = hf
        af_ref[pl.ds(off_f, w), :] = pf
        a = ab_ref[pl.ds(off_b, w), :]
        hb = a * hb + bb_ref[pl.ds(off_b, w), :]
        pb = pb * a
        bb_ref[pl.ds(off_b, w), :] = hb
        ab_ref[pl.ds(off_b, w), :] = pb
        return hf, pf, hb, pb

    zeros = jnp.zeros((w, BLOCK_DIM), F32)
    ones = jnp.ones((w, BLOCK_DIM), F32)
    lax.fori_loop(0, n_rows, scan, (zeros, ones, zeros, ones), unroll=16)

    last = (n_rows - 1) * w

    def carry(n, c):
        cf, cb = c
        cf_ref[pl.ds(n, 1), :] = cf
        cf = bf_ref[pl.ds(last + n, 1), :] + af_ref[pl.ds(last + n, 1), :] * cf
        col = w - 1 - n
        cr_ref[pl.ds(col, 1), :] = cb
        cb = bb_ref[pl.ds(col, 1), :] + ab_ref[pl.ds(col, 1), :] * cb
        return cf, cb

    lax.fori_loop(0, w, carry, (h0f_ref[...], h0b_ref[...]), unroll=8)

    def fix(s, c):
        off = pl.multiple_of(s * tb, tb)
        cf = jnp.concatenate([cf_ref[...]] * rows_per_step, axis=0)
        cr = jnp.concatenate([cr_ref[...]] * rows_per_step, axis=0)
        h = (bf_ref[pl.ds(off, tb), :] + af_ref[pl.ds(off, tb), :] * cf
             + bb_ref[pl.ds(off, tb), :] + ab_ref[pl.ds(off, tb), :] * cr)
        o_ref[pl.ds(off, tb), :] = h.astype(o_ref.dtype)
        return c

    lax.fori_loop(0, n_steps, fix, 0, unroll=4)


def _lru_call(feat, conv_w, conv_b, wg, br, bi, lam, h0f, h0b, *, batch, seq):
    n_rows = seq // GRID_W
    vec = lambda n: pl.BlockSpec((n, BLOCK_DIM), lambda b, c: (0, c))
    h0 = pl.BlockSpec((None, 1, BLOCK_DIM), lambda b, c: (b, 0, c))
    buf = pltpu.VMEM((seq, BLOCK_DIM), F32)
    n_i, n_col, tm, _ = feat.shape
    feat5 = feat.reshape(batch, n_i // batch, n_col, tm, D_MODEL)
    return pl.pallas_call(
        functools.partial(_lru_kernel, n_rows=n_rows, rows_per_step=LRU_ROWS_PER_STEP),
        grid=(batch, N_BLOCKS),
        in_specs=[pl.BlockSpec((None, n_i // batch, None, tm, BLOCK_DIM), lambda b, c: (b, 0, F_Z5, 0, c)),
                  vec(N_CONV), vec(1),
                  pl.BlockSpec((None, BLOCK_DIM, 4 * BLOCK_DIM), lambda b, c: (c, 0, 0)),
                  vec(2), vec(2), vec(2), h0, h0],
        out_specs=pl.BlockSpec((None, seq, BLOCK_DIM), lambda b, c: (c, b, 0)),
        out_shape=jax.ShapeDtypeStruct((N_BLOCKS, batch * seq, BLOCK_DIM), BF16),
        scratch_shapes=[pltpu.VMEM((seq + 3 * GRID_W, BLOCK_DIM), F32), buf, buf, buf, buf,
                        pltpu.VMEM((GRID_W, BLOCK_DIM), F32), pltpu.VMEM((GRID_W, BLOCK_DIM), F32)],
        compiler_params=pltpu.CompilerParams(
            dimension_semantics=("arbitrary", "arbitrary"), vmem_limit_bytes=VMEM_LIMIT),
        name="lru",
    )(feat5, conv_w, conv_b, wg, br, bi, lam, h0f, h0b)


def _merge_kernel(oa_ref, hx_ref, g4_ref, g6_ref, m7_ref, m8_ref, x_ref, mod_ref, ng_ref,
                  pa_ref, pb_ref, wo_ref, lg_ref, lbias_ref, o_ref):
    f32 = lambda ref: jnp.concatenate([ref[h] for h in range(N_HEADS)], axis=-1).astype(F32)
    o_b = (f32(hx_ref) * f32(g6_ref)).astype(BF16)
    y = f32(m8_ref) * _dot(o_b, pb_ref[...])
    y_a = None
    for pair in range(N_HEADS // 2):
        o_h = []
        for h in (2 * pair, 2 * pair + 1):
            t = oa_ref[h].astype(F32)
            ms = jnp.mean(t * t, axis=-1, keepdims=True)
            o_h.append((t * lax.rsqrt(ms + RMS_EPS) * ng_ref[...] * g4_ref[h].astype(F32)).astype(BF16))
        rows = slice(2 * pair * HEAD_DIM, 2 * (pair + 1) * HEAD_DIM)
        part = _dot(jnp.concatenate(o_h, axis=-1), pa_ref[rows, :])
        y_a = part if y_a is None else y_a + part
    y = y + f32(m7_ref) * y_a
    y = _dot(y.astype(BF16), wo_ref[...])
    t = DEEPNORM_ALPHA * x_ref[...] + mod_ref[2:3, :] * y
    mu = jnp.mean(t, axis=-1, keepdims=True)
    tc = t - mu
    var = jnp.mean(tc * tc, axis=-1, keepdims=True)
    o_ref[...] = tc * lax.rsqrt(var + LN_EPS) * lg_ref[...] + lbias_ref[...]


def _merge_call(oa, hx, feat_h, x2, mod3, ng, pa, pb, wo, lg, lbias, *, tm, tiles_per_batch):
    n_tok = x2.shape[0]
    tok = pl.BlockSpec((tm, D_MODEL), lambda i: (i, 0))
    heads = pl.BlockSpec((N_HEADS, tm, HEAD_DIM), lambda i: (0, i, 0))
    per_tile = feat_h.shape[3] // tm
    col = lambda j: pl.BlockSpec((None, None, N_HEADS, tm, HEAD_DIM),
                                 lambda i, j=j: (i // per_tile, j, 0, i % per_tile, 0))
    full = lambda shape: pl.BlockSpec(shape, lambda i: (0,) * len(shape), pipeline_mode=pl.Buffered(1))
    return pl.pallas_call(
        _merge_kernel,
        grid=(n_tok // tm,),
        in_specs=[heads, heads, col(H_G4), col(H_G6), col(H_M7), col(H_M8), tok,
                  pl.BlockSpec((None, 3, D_MODEL), lambda i: (i // tiles_per_batch, 0, 0)),
                  full((1, HEAD_DIM)), full((D_MODEL, D_MODEL)), full((D_MODEL, D_MODEL)),
                  full((D_MODEL, D_MODEL)), full((1, D_MODEL)), full((1, D_MODEL))],
        out_specs=tok,
        out_shape=jax.ShapeDtypeStruct((n_tok, D_MODEL), F32),
        compiler_params=pltpu.CompilerParams(
            dimension_semantics=("arbitrary",), vmem_limit_bytes=VMEM_LIMIT),
        name="merge",
    )(oa, hx, feat_h, feat_h, feat_h, feat_h, x2, mod3, ng, pa, pb, wo, lg, lbias)


def kernel(x, c, ctx, c_ctx, w_mod, b_mod, w_in, b_in, lb_logits, norm_a_g, conv_w, conv_b,
           w_r, b_r, w_i, b_i, lam, p_a, p_b, w_out, ln_g, ln_b):
    batch, seq, d = x.shape
    n_ctx = ctx.shape[1]
    assert d == D_MODEL and seq % GRID_W == 0 and w_in.shape[0] == 1

    n_blocks = w_in.shape[-1] // D_MODEL
    col_scale = jnp.repeat(jnp.array([0.5 if j in HALVED_BLOCKS else 1.0 for j in range(n_blocks)], F32), D_MODEL)
    w_bf = (w_in[0] * col_scale).astype(BF16)
    b2 = (b_in[0] * col_scale)[None, :]
    wg = (0.5 * jnp.concatenate([w_r[0, 0], w_i[0, 0], w_r[0, 1], w_i[0, 1]], axis=-1)).astype(BF16)
    assert batch < SUBLANES
    mod, lb = _mod_call(c, c_ctx[None, :], w_mod[0], b_mod[0][None, :], lb_logits)
    mod3 = mod.reshape(SUBLANES, 3, D_MODEL)

    x2 = x.reshape(batch * seq, D_MODEL)
    tm = INPROJ_TM
    feat = _inproj_call(x2, mod3, w_bf, b2, LATENT_F32, ("q", "forget0", "forget1", None), lb,
                        out_dtype=F32, tm=tm, tiles_per_mod=seq // tm)
    feat_h = _inproj_call(x2, mod3, w_bf, b2, LATENT_BF16, ("silu", "silu", "sig", "sig", None),
                          out_dtype=BF16, tm=tm, tiles_per_mod=seq // tm, head_major=True)
    feat_c = _inproj_call(ctx.reshape(batch * n_ctx, D_MODEL), mod3[batch:batch + 1], w_bf, b2, CTX_F32,
                          ("forget0", "forget1", None, None), lb,
                          out_dtype=F32, tm=n_ctx, tiles_per_mod=batch)

    cw, cb = conv_w[0], conv_b[0][None, :]
    s0f, s0b, h0f, h0b = _ctx_call(feat_c, cw, cb, wg, b_r[0], b_i[0], lam[0], batch=batch, n_ctx=n_ctx)

    o_b = _gla_call(feat, feat_h, F_F1, s0b, reverse=True, batch=batch, seq=seq, rows=GLA_ROWS)
    oa = _gla_call(feat, feat_h, F_F0, s0f, o_b, reverse=False, batch=batch, seq=seq, rows=GLA_ROWS)
    hx = _lru_call(feat, cw, cb, wg, b_r[0], b_i[0], lam[0], h0f, h0b, batch=batch, seq=seq)

    out = _merge_call(oa, hx, feat_h, x2, mod3, norm_a_g[0][None, :],
                      p_a[0].astype(BF16), p_b[0].astype(BF16), w_out[0].astype(BF16),
                      ln_g[0][None, :], ln_b[0][None, :], tm=MERGE_TM, tiles_per_batch=seq // MERGE_TM)
    return out.reshape(batch, seq, D_MODEL)
```

```python
import functools

import jax
import jax.numpy as jnp
from jax import lax
from jax.experimental import pallas as pl
from jax.experimental.pallas import tpu as pltpu

F32 = jnp.float32
BF16 = jnp.bfloat16

D_MODEL = 1024
GRID_W = 64
HEAD_DIM = 128
N_HEADS = D_MODEL // HEAD_DIM
N_BLOCKS = 8
BLOCK_DIM = D_MODEL // N_BLOCKS
N_CONV = 4
RG_C = 8.0
LN_EPS = 1e-5
RMS_EPS = 1e-6
DEEPNORM_ALPHA = 2.0 ** 0.25
Q_SCALE = HEAD_DIM ** -0.5

V7X_VMEM_BYTES = 64 * 1024 * 1024
VMEM_LIMIT = V7X_VMEM_BYTES - 8 * 1024 * 1024
SUBLANES = 8

INPROJ_TM = 1024
GLA_ROWS = 16
LRU_ROWS_PER_STEP = 16
MERGE_TM = 512

LATENT_F32 = (0, 1, 2, 5)
LATENT_BF16 = (4, 6, 7, 8, 3)
CTX_F32 = (1, 2, 3, 5)
F_Q, F_F0, F_F1, F_Z5 = 0, 1, 2, 3
H_G4, H_G6, H_M7, H_M8, H_V = 0, 1, 2, 3, 4
HALVED_BLOCKS = (0, 1, 2, 4, 6, 7, 8)


def _silu(x):
    return _half_silu(0.5 * x)


def _half_silu(h):
    return h * jnp.tanh(h) + h


def _half_sigmoid(h):
    return 0.5 * jnp.tanh(h) + 0.5


def _half_forget(h, lb):
    return (0.5 + 0.5 * lb) + (0.5 - 0.5 * lb) * jnp.tanh(h)


def _dot(a, b):
    return jnp.dot(a, b, preferred_element_type=F32)


def _dot_tn(a, b):
    return lax.dot_general(a, b, (((0,), (0,)), ((), ())), preferred_element_type=F32)


def _chunk_cumprod(x, reverse):
    n, c = x.shape
    nb = n // 8
    y = x.reshape(nb, 8, c)
    sub = lax.broadcasted_iota(jnp.int32, (nb, 8, c), 1)
    for s in (1, 2, 4):
        if reverse:
            y = y * jnp.where(sub < 8 - s, pltpu.roll(y, 8 - s, axis=1), 1.0)
        else:
            y = y * jnp.where(sub >= s, pltpu.roll(y, s, axis=1), 1.0)
    offs = [None] * nb
    acc = jnp.ones((1, c), F32)
    for b in (reversed(range(nb)) if reverse else range(nb)):
        offs[b] = acc
        acc = acc * (y[b, 0:1, :] if reverse else y[b, 7:8, :])
    return (y * jnp.stack(offs)).reshape(n, c), acc


def _cumsum_rows(mask_bf16, x):
    hi = x.astype(BF16)
    r1 = x - hi.astype(F32)
    mid = r1.astype(BF16)
    lo = (r1 - mid.astype(F32)).astype(BF16)
    return _dot(mask_bf16, hi) + _dot(mask_bf16, mid) + _dot(mask_bf16, lo)


def _mod_kernel(c_ref, cc_ref, w_ref, b_ref, lbl_ref, mod_ref, lb_ref, rows_ref):
    n = c_ref.shape[0]
    rows_ref[...] = jnp.zeros(rows_ref.shape, F32)
    rows_ref[0:n, :] = c_ref[...]
    rows_ref[n:n + 1, :] = cc_ref[...]
    mod_ref[...] = _dot(_silu(rows_ref[...]), w_ref[...]) + b_ref[...]
    l = lbl_ref[...]
    e = jnp.exp(l - jnp.max(l, axis=0, keepdims=True))
    lb_ref[...] = e[0] / jnp.sum(e, axis=0)


def _mod_call(c, c_ctx, w_mod, b_mod, lb_logits):
    return pl.pallas_call(
        _mod_kernel,
        out_shape=(jax.ShapeDtypeStruct((SUBLANES, 3 * D_MODEL), F32),
                   jax.ShapeDtypeStruct((2, D_MODEL), F32)),
        scratch_shapes=[pltpu.VMEM((SUBLANES, D_MODEL), F32)],
        compiler_params=pltpu.CompilerParams(vmem_limit_bytes=VMEM_LIMIT),
        name="mod",
    )(c, c_ctx, w_mod, b_mod, lb_logits)


def _inproj_kernel(x_ref, mod_ref, *refs, head_major, acts):
    n_col = len(acts)
    if any(a and a.startswith("forget") for a in acts):
        lb_ref, refs = refs[0], refs[1:]
    w_refs, b_refs, o_ref = refs[:n_col], refs[n_col:2 * n_col], refs[-1]
    u = (x_ref[...] * (1.0 + mod_ref[1:2, :]) + mod_ref[0:1, :]).astype(BF16)
    for j in range(n_col):
        z = _dot(u, w_refs[j][...]) + b_refs[j][...]
        if acts[j] == "q":
            z = _half_silu(z) * Q_SCALE
        elif acts[j] == "silu":
            z = _half_silu(z)
        elif acts[j] == "sig":
            z = _half_sigmoid(z)
        elif acts[j] is not None:
            d = int(acts[j][-1])
            z = _half_forget(z, lb_ref[d:d + 1, :])
        z = z.astype(o_ref.dtype)
        if head_major:
            for h in range(N_HEADS):
                o_ref[j, h] = z[:, h * HEAD_DIM:(h + 1) * HEAD_DIM]
        else:
            o_ref[j] = z


def _inproj_call(x2, mod3, w, b, cols, acts, lb=None, *, out_dtype, tm, tiles_per_mod, head_major=False):
    n_tok = x2.shape[0]
    n_col = len(cols)
    tile = (N_HEADS, tm, HEAD_DIM) if head_major else (tm, D_MODEL)
    col_block = lambda rows: [pl.BlockSpec((rows, D_MODEL), lambda i, c=c: (0, c), pipeline_mode=pl.Buffered(1))
                              for c in cols]
    return pl.pallas_call(
        functools.partial(_inproj_kernel, head_major=head_major, acts=acts),
        grid=(n_tok // tm,),
        in_specs=[
            pl.BlockSpec((tm, D_MODEL), lambda i: (i, 0)),
            pl.BlockSpec((None, 3, D_MODEL), lambda i: (i // tiles_per_mod, 0, 0)),
            *([] if lb is None else [pl.BlockSpec((2, D_MODEL), lambda i: (0, 0))]),
            *col_block(D_MODEL),
            *col_block(1),
        ],
        out_specs=pl.BlockSpec((None, n_col) + tile, lambda i: (i, 0) + (0,) * len(tile)),
        out_shape=jax.ShapeDtypeStruct((n_tok // tm, n_col) + tile, out_dtype),
        compiler_params=pltpu.CompilerParams(
            dimension_semantics=("arbitrary",), vmem_limit_bytes=VMEM_LIMIT),
        name="inproj",
    )(x2, mod3, *([] if lb is None else [lb]), *([w] * n_col), *([b] * n_col))


LOG2_E = 1.4426950408889634


def _lru_ab(hx, h_r, h_i, br, bi, sp):
    k = (-0.5 * RG_C * LOG2_E) * sp
    a = jnp.exp2(k * jnp.tanh(h_r + 0.5 * br) + k)
    y = 1.0 - a * a
    mult = jnp.where(y > 0.0, y * lax.rsqrt(y), 0.0)
    return a, mult * (hx * jnp.tanh(h_i + 0.5 * bi) + hx)


def _softplus(y):
    return jnp.maximum(y, 0.0) + jnp.log(1.0 + jnp.exp(-jnp.abs(y)))


def _ctx_kernel(ff_ref, fb_ref, v_ref, z5_ref, cw_ref, cb_ref, wg_ref, br_ref, bi_ref, lam_ref,
                sf_ref, sb_ref, hf_ref, hb_ref, zp_ref, a_ref, b_ref):
    n = ff_ref.shape[0]
    ri = lax.broadcasted_iota(jnp.int32, (n, n), 0)
    ci = lax.broadcasted_iota(jnp.int32, (n, n), 1)
    tril = (ci <= ri).astype(F32).astype(BF16)
    v = v_ref[...].astype(BF16)

    f = ff_ref[...]
    g = _cumsum_rows(tril, jnp.log(f))
    ke = ((1.0 - f) * jnp.exp(g[n - 1:n, :] - g)).astype(BF16)
    for h in range(N_HEADS):
        sl = slice(h * HEAD_DIM, (h + 1) * HEAD_DIM)
        sf_ref[h] = _dot_tn(v[:, sl], ke[:, sl])
    f = fb_ref[...]
    lf = jnp.log(f)
    ke = ((1.0 - f) * jnp.exp(_cumsum_rows(tril, lf) - lf)).astype(BF16)
    for h in range(N_HEADS):
        sl = slice(h * HEAD_DIM, (h + 1) * HEAD_DIM)
        sb_ref[h] = _dot_tn(v[:, sl], ke[:, sl])

    pad = SUBLANES
    zp_ref[0:pad, :] = jnp.zeros((pad, D_MODEL), F32)
    zp_ref[pad + n:2 * pad + n, :] = jnp.zeros((pad, D_MODEL), F32)
    zp_ref[pad:pad + n, :] = z5_ref[...]
    first = pad - (N_CONV - 1) // 2
    xc = cb_ref[...] + zp_ref[first:first + n, :] * cw_ref[0:1, :]
    for kk in range(1, N_CONV):
        xc = xc + zp_ref[first + kk:first + kk + n, :] * cw_ref[kk:kk + 1, :]

    for d, h_ref in ((0, hf_ref), (1, hb_ref)):
        sp = _softplus(-lam_ref[d:d + 1, :])
        for blk in range(N_BLOCKS):
            sl = slice(blk * BLOCK_DIM, (blk + 1) * BLOCK_DIM)
            g = _dot(xc[:, sl].astype(BF16), wg_ref[blk, :, 2 * d * BLOCK_DIM:2 * (d + 1) * BLOCK_DIM])
            a, b = _lru_ab(0.5 * xc[:, sl], g[:, :BLOCK_DIM], g[:, BLOCK_DIM:],
                           br_ref[d:d + 1, sl], bi_ref[d:d + 1, sl], sp[:, sl])
            a_ref[:, sl] = a
            b_ref[:, sl] = b

        def step(t, h, d=d):
            tt = (n - 1 - t) if d == 1 else t
            return a_ref[pl.ds(tt, 1), :] * h + b_ref[pl.ds(tt, 1), :]

        h_ref[...] = lax.fori_loop(0, n, step, jnp.zeros((1, D_MODEL), F32), unroll=8)


def _ctx_call(feat_c, conv_w, conv_b, wg, br, bi, lam, *, batch, n_ctx):
    feat = lambda j: pl.BlockSpec((None, None, n_ctx, D_MODEL), lambda b, j=j: (b, j, 0, 0))
    full = lambda shape: pl.BlockSpec(shape, lambda b: (0,) * len(shape))
    state = pl.BlockSpec((None, N_HEADS, HEAD_DIM, HEAD_DIM), lambda b: (b, 0, 0, 0))
    hvec = pl.BlockSpec((None, 1, D_MODEL), lambda b: (b, 0, 0))
    return pl.pallas_call(
        _ctx_kernel,
        grid=(batch,),
        in_specs=[feat(0), feat(1), feat(2), feat(3),
                  full((N_CONV, D_MODEL)), full((1, D_MODEL)),
                  full((N_BLOCKS, BLOCK_DIM, 4 * BLOCK_DIM)),
                  full((2, D_MODEL)), full((2, D_MODEL)), full((2, D_MODEL))],
        out_specs=(state, state, hvec, hvec),
        out_shape=(jax.ShapeDtypeStruct((batch, N_HEADS, HEAD_DIM, HEAD_DIM), F32),
                   jax.ShapeDtypeStruct((batch, N_HEADS, HEAD_DIM, HEAD_DIM), F32),
                   jax.ShapeDtypeStruct((batch, 1, D_MODEL), F32),
                   jax.ShapeDtypeStruct((batch, 1, D_MODEL), F32)),
        scratch_shapes=[pltpu.VMEM((n_ctx + 2 * SUBLANES, D_MODEL), F32),
                        pltpu.VMEM((n_ctx, D_MODEL), F32),
                        pltpu.VMEM((n_ctx, D_MODEL), F32)],
        compiler_params=pltpu.CompilerParams(
            dimension_semantics=("arbitrary",), vmem_limit_bytes=VMEM_LIMIT),
        name="ctx_states",
    )(feat_c, feat_c, feat_c, feat_c, conv_w, conv_b, wg, br, bi, lam)


GLA_ROWS_PER_ITER = 16


def _gla_kernel(*refs, reverse, rows, finish):
    if finish:
        (q_ref, f_ref, v_ref, s0_ref, other_ref, o_ref,
         st_ref, sn_ref, qd_ref, kit_ref, ke_ref, dec_ref) = refs
    else:
        q_ref, f_ref, v_ref, s0_ref, o_ref, st_ref, sn_ref, qd_ref, kit_ref, ke_ref, dec_ref = refs

    @pl.when(pl.program_id(1) == 0)
    def _():
        st_ref[...] = s0_ref[...]
        for h in range(N_HEADS):
            sn_ref[h] = s0_ref[h].astype(BF16).T

    ri = lax.broadcasted_iota(jnp.int32, (GRID_W, GRID_W), 0)
    ci = lax.broadcasted_iota(jnp.int32, (GRID_W, GRID_W), 1)
    allow = (ci >= ri) if reverse else (ci <= ri)

    def row_offset(n):
        n = jnp.minimum(n, rows - 1)
        return pl.multiple_of(((rows - 1 - n) if reverse else n) * GRID_W, GRID_W)

    def prepare(n, slot):
        off = row_offset(n)
        q = q_ref[pl.ds(off, GRID_W), :]
        f = f_ref[pl.ds(off, GRID_W), :]
        dg, dec = _chunk_cumprod(f, reverse)
        ki = (1.0 - f) / dg
        qd = (q * dg).astype(BF16)
        ke = (ki * dec).astype(BF16)
        kit_ref[slot] = ki.astype(BF16).T
        dec_ref[slot] = dec
        for h in range(N_HEADS):
            sl = slice(h * HEAD_DIM, (h + 1) * HEAD_DIM)
            qd_ref[slot, h] = qd[:, sl]
            ke_ref[slot, h] = ke[:, sl]

    def contract(n, slot):
        off = row_offset(n)
        for h in range(N_HEADS):
            sl = slice(h * HEAD_DIM, (h + 1) * HEAD_DIM)
            qd = qd_ref[slot, h]
            v = v_ref[h, pl.ds(off, GRID_W), :]
            p = jnp.where(allow, _dot(qd, kit_ref[slot, sl, :]), 0.0).astype(BF16)
            o = _dot(p, v) + _dot(qd, sn_ref[h])
            st = st_ref[h] * dec_ref[slot, :, sl] + _dot_tn(v, ke_ref[slot, h])
            st_ref[h] = st
            sn_ref[h] = st.astype(BF16).T
            if finish:
                o = o + other_ref[h, pl.ds(off, GRID_W), :].astype(F32)
            o_ref[h, pl.ds(off, GRID_W), :] = o.astype(o_ref.dtype)

    prepare(0, 0)

    def body(m, carry):
        n = GLA_ROWS_PER_ITER * m
        for i in range(GLA_ROWS_PER_ITER):
            contract(n + i, i % 2)
            prepare(n + i + 1, (i + 1) % 2)
        return carry

    lax.fori_loop(0, rows // GLA_ROWS_PER_ITER, body, 0)


def _gla_call(feat, feat_h, f_col, s0, other=None, *, reverse, batch, seq, rows):
    tb = rows * GRID_W
    nrb = seq // tb
    finish = other is not None
    per_tile = feat.shape[2] // tb

    def row_block(b, i):
        return b * nrb + ((nrb - 1 - i) if reverse else i)

    def col(c):
        return pl.BlockSpec((None, None, tb, D_MODEL), lambda b, i, c=c: (
            row_block(b, i) // per_tile, c, row_block(b, i) % per_tile, 0))

    tok = pl.BlockSpec((N_HEADS, tb, HEAD_DIM), lambda b, i: (0, row_block(b, i), 0))
    v_spec = pl.BlockSpec((None, None, N_HEADS, tb, HEAD_DIM), lambda b, i: (
        row_block(b, i) // per_tile, H_V, 0, row_block(b, i) % per_tile, 0))
    in_specs = [col(F_Q), col(f_col), v_spec,
                pl.BlockSpec((None, N_HEADS, HEAD_DIM, HEAD_DIM), lambda b, i: (b, 0, 0, 0))]
    args = [feat, feat, feat_h, s0]
    if finish:
        in_specs += [tok]
        args += [other]
    return pl.pallas_call(
        functools.partial(_gla_kernel, reverse=reverse, rows=rows, finish=finish),
        grid=(batch, nrb),
        in_specs=in_specs,
        out_specs=tok,
        out_shape=jax.ShapeDtypeStruct((N_HEADS, batch * seq, HEAD_DIM), BF16),
        scratch_shapes=[pltpu.VMEM((N_HEADS, HEAD_DIM, HEAD_DIM), F32)]
        + [pltpu.VMEM((N_HEADS, HEAD_DIM, HEAD_DIM), BF16), pltpu.VMEM((2, N_HEADS, GRID_W, HEAD_DIM), BF16),
           pltpu.VMEM((2, D_MODEL, GRID_W), BF16), pltpu.VMEM((2, N_HEADS, GRID_W, HEAD_DIM), BF16),
           pltpu.VMEM((2, 1, D_MODEL), F32)],
        compiler_params=pltpu.CompilerParams(
            dimension_semantics=("arbitrary", "arbitrary"), vmem_limit_bytes=VMEM_LIMIT),
        name="gla_bwd" if reverse else "gla_fwd",
    )(*args)


def _lru_kernel(z5_ref, cw_ref, cb_ref, wg_ref, br_ref, bi_ref, lam_ref, h0f_ref, h0b_ref, o_ref,
                zp_ref, af_ref, bf_ref, ab_ref, bb_ref, cf_ref, cr_ref, *, n_rows, rows_per_step):
    w = GRID_W
    seq = n_rows * w
    tb = rows_per_step * w
    n_steps = n_rows // rows_per_step

    zp_ref[0:w, :] = jnp.zeros((w, BLOCK_DIM), F32)
    zp_ref[w + seq:, :] = jnp.zeros((2 * w, BLOCK_DIM), F32)
    tile = z5_ref.shape[1]
    for k in range(z5_ref.shape[0]):
        zp_ref[w + k * tile:w + (k + 1) * tile, :] = z5_ref[k]

    sp = [_softplus(-lam_ref[d:d + 1, :]) for d in (0, 1)]

    def gates(s, c):
        off = pl.multiple_of(s * tb, tb)
        xc = cb_ref[...] + zp_ref[pl.ds(off, tb), :] * cw_ref[0:1, :]
        for kk in range(1, N_CONV):
            xc = xc + zp_ref[pl.ds(off + kk * w, tb), :] * cw_ref[kk:kk + 1, :]
        g = _dot(xc.astype(BF16), wg_ref[...])
        hx = 0.5 * xc
        for d, a_ref, b_ref in ((0, af_ref, bf_ref), (1, ab_ref, bb_ref)):
            c0 = 2 * d * BLOCK_DIM
            a, b = _lru_ab(hx, g[:, c0:c0 + BLOCK_DIM], g[:, c0 + BLOCK_DIM:c0 + 2 * BLOCK_DIM],
                           br_ref[d:d + 1, :], bi_ref[d:d + 1, :], sp[d])
            a_ref[pl.ds(off, tb), :] = a
            b_ref[pl.ds(off, tb), :] = b
        return c

    lax.fori_loop(0, n_steps, gates, 0, unroll=4)

    def scan(n, carry):
        hf, pf, hb, pb = carry
        off_f = pl.multiple_of(n * w, w)
        off_b = pl.multiple_of((n_rows - 1 - n) * w, w)
        a = af_ref[pl.ds(off_f, w), :]
        hf = a * hf + bf_ref[pl.ds(off_f, w), :]
        pf = pf * a
        bf_ref[pl.ds(off_f, w), :] = hf
        af_ref[pl.ds(off_f, w), :] = pf
        a = ab_ref[pl.ds(off_b, w), :]
        hb = a * hb + bb_ref[pl.ds(off_b, w), :]
        pb = pb * a
        bb_ref[pl.ds(off_b, w), :] = hb
        ab_ref[pl.ds(off_b, w), :] = pb
        return hf, pf, hb, pb

    zeros = jnp.zeros((w, BLOCK_DIM), F32)
    ones = jnp.ones((w, BLOCK_DIM), F32)
    lax.fori_loop(0, n_rows, scan, (zeros, ones, zeros, ones), unroll=16)

    last = (n_rows - 1) * w

    def carry(n, c):
        cf, cb = c
        cf_ref[pl.ds(n, 1), :] = cf
        cf = bf_ref[pl.ds(last + n, 1), :] + af_ref[pl.ds(last + n, 1), :] * cf
        col = w - 1 - n
        cr_ref[pl.ds(col, 1), :] = cb
        cb = bb_ref[pl.ds(col, 1), :] + ab_ref[pl.ds(col, 1), :] * cb
        return cf, cb

    lax.fori_loop(0, w, carry, (h0f_ref[...], h0b_ref[...]), unroll=8)

    def fix(s, c):
        off = pl.multiple_of(s * tb, tb)
        cf = jnp.concatenate([cf_ref[...]] * rows_per_step, axis=0)
        cr = jnp.concatenate([cr_ref[...]] * rows_per_step, axis=0)
        h = (bf_ref[pl.ds(off, tb), :] + af_ref[pl.ds(off, tb), :] * cf
             + bb_ref[pl.ds(off, tb), :] + ab_ref[pl.ds(off, tb), :] * cr)
        o_ref[pl.ds(off, tb), :] = h.astype(o_ref.dtype)
        return c

    lax.fori_loop(0, n_steps, fix, 0, unroll=4)


def _lru_call(feat, conv_w, conv_b, wg, br, bi, lam, h0f, h0b, *, batch, seq):
    n_rows = seq // GRID_W
    vec = lambda n: pl.BlockSpec((n, BLOCK_DIM), lambda b, c: (0, c))
    h0 = pl.BlockSpec((None, 1, BLOCK_DIM), lambda b, c: (b, 0, c))
    buf = pltpu.VMEM((seq, BLOCK_DIM), F32)
    n_i, n_col, tm, _ = feat.shape
    feat5 = feat.reshape(batch, n_i // batch, n_col, tm, D_MODEL)
    return pl.pallas_call(
        functools.partial(_lru_kernel, n_rows=n_rows, rows_per_step=LRU_ROWS_PER_STEP),
        grid=(batch, N_BLOCKS),
        in_specs=[pl.BlockSpec((None, n_i // batch, None, tm, BLOCK_DIM), lambda b, c: (b, 0, F_Z5, 0, c)),
                  vec(N_CONV), vec(1),
                  pl.BlockSpec((None, BLOCK_DIM, 4 * BLOCK_DIM), lambda b, c: (c, 0, 0)),
                  vec(2), vec(2), vec(2), h0, h0],
        out_specs=pl.BlockSpec((None, seq, BLOCK_DIM), lambda b, c: (c, b, 0)),
        out_shape=jax.ShapeDtypeStruct((N_BLOCKS, batch * seq, BLOCK_DIM), BF16),
        scratch_shapes=[pltpu.VMEM((seq + 3 * GRID_W, BLOCK_DIM), F32), buf, buf, buf, buf,
                        pltpu.VMEM((GRID_W, BLOCK_DIM), F32), pltpu.VMEM((GRID_W, BLOCK_DIM), F32)],
        compiler_params=pltpu.CompilerParams(
            dimension_semantics=("arbitrary", "arbitrary"), vmem_limit_bytes=VMEM_LIMIT),
        name="lru",
    )(feat5, conv_w, conv_b, wg, br, bi, lam, h0f, h0b)


def _merge_kernel(oa_ref, hx_ref, g4_ref, g6_ref, m7_ref, m8_ref, x_ref, mod_ref, ng_ref,
                  pa_ref, pb_ref, wo_ref, lg_ref, lbias_ref, o_ref):
    f32 = lambda ref: jnp.concatenate([ref[h] for h in range(N_HEADS)], axis=-1).astype(F32)
    o_b = (f32(hx_ref) * f32(g6_ref)).astype(BF16)
    y = f32(m8_ref) * _dot(o_b, pb_ref[...])
    y_a = None
    for pair in range(N_HEADS // 2):
        o_h = []
        for h in (2 * pair, 2 * pair + 1):
            t = oa_ref[h].astype(F32)
            ms = jnp.mean(t * t, axis=-1, keepdims=True)
            o_h.append((t * lax.rsqrt(ms + RMS_EPS) * ng_ref[...] * g4_ref[h].astype(F32)).astype(BF16))
        rows = slice(2 * pair * HEAD_DIM, 2 * (pair + 1) * HEAD_DIM)
        part = _dot(jnp.concatenate(o_h, axis=-1), pa_ref[rows, :])
        y_a = part if y_a is None else y_a + part
    y = y + f32(m7_ref) * y_a
    y = _dot(y.astype(BF16), wo_ref[...])
    t = DEEPNORM_ALPHA * x_ref[...] + mod_ref[2:3, :] * y
    mu = jnp.mean(t, axis=-1, keepdims=True)
    tc = t - mu
    var = jnp.mean(tc * tc, axis=-1, keepdims=True)
    o_ref[...] = tc * lax.rsqrt(var + LN_EPS) * lg_ref[...] + lbias_ref[...]


def _merge_call(oa, hx, feat_h, x2, mod3, ng, pa, pb, wo, lg, lbias, *, tm, tiles_per_batch):
    n_tok = x2.shape[0]
    tok = pl.BlockSpec((tm, D_MODEL), lambda i: (i, 0))
    heads = pl.BlockSpec((N_HEADS, tm, HEAD_DIM), lambda i: (0, i, 0))
    per_tile = feat_h.shape[3] // tm
    col = lambda j: pl.BlockSpec((None, None, N_HEADS, tm, HEAD_DIM),
                                 lambda i, j=j: (i // per_tile, j, 0, i % per_tile, 0))
    full = lambda shape: pl.BlockSpec(shape, lambda i: (0,) * len(shape), pipeline_mode=pl.Buffered(1))
    return pl.pallas_call(
        _merge_kernel,
        grid=(n_tok // tm,),
        in_specs=[heads, heads, col(H_G4), col(H_G6), col(H_M7), col(H_M8), tok,
                  pl.BlockSpec((None, 3, D_MODEL), lambda i: (i // tiles_per_batch, 0, 0)),
                  full((1, HEAD_DIM)), full((D_MODEL, D_MODEL)), full((D_MODEL, D_MODEL)),
                  full((D_MODEL, D_MODEL)), full((1, D_MODEL)), full((1, D_MODEL))],
        out_specs=tok,
        out_shape=jax.ShapeDtypeStruct((n_tok, D_MODEL), F32),
        compiler_params=pltpu.CompilerParams(
            dimension_semantics=("arbitrary",), vmem_limit_bytes=VMEM_LIMIT),
        name="merge",
    )(oa, hx, feat_h, feat_h, feat_h, feat_h, x2, mod3, ng, pa, pb, wo, lg, lbias)


def kernel(x, c, ctx, c_ctx, w_mod, b_mod, w_in, b_in, lb_logits, norm_a_g, conv_w, conv_b,
           w_r, b_r, w_i, b_i, lam, p_a, p_b, w_out, ln_g, ln_b):
    batch, seq, d = x.shape
    n_ctx = ctx.shape[1]
    assert d == D_MODEL and seq % GRID_W == 0 and w_in.shape[0] == 1

    n_blocks = w_in.shape[-1] // D_MODEL
    col_scale = jnp.repeat(jnp.array([0.5 if j in HALVED_BLOCKS else 1.0 for j in range(n_blocks)], F32), D_MODEL)
    w_bf = (w_in[0] * col_scale).astype(BF16)
    b2 = (b_in[0] * col_scale)[None, :]
    wg = (0.5 * jnp.concatenate([w_r[0, 0], w_i[0, 0], w_r[0, 1], w_i[0, 1]], axis=-1)).astype(BF16)
    assert batch < SUBLANES
    mod, lb = _mod_call(c, c_ctx[None, :], w_mod[0], b_mod[0][None, :], lb_logits)
    mod3 = mod.reshape(SUBLANES, 3, D_MODEL)

    x2 = x.reshape(batch * seq, D_MODEL)
    tm = INPROJ_TM
    feat = _inproj_call(x2, mod3, w_bf, b2, LATENT_F32, ("q", "forget0", "forget1", None), lb,
                        out_dtype=F32, tm=tm, tiles_per_mod=seq // tm)
    feat_h = _inproj_call(x2, mod3, w_bf, b2, LATENT_BF16, ("silu", "silu", "sig", "sig", None),
                          out_dtype=BF16, tm=tm, tiles_per_mod=seq // tm, head_major=True)
    feat_c = _inproj_call(ctx.reshape(batch * n_ctx, D_MODEL), mod3[batch:batch + 1], w_bf, b2, CTX_F32,
                          ("forget0", "forget1", None, None), lb,
                          out_dtype=F32, tm=n_ctx, tiles_per_mod=batch)

    cw, cb = conv_w[0], conv_b[0][None, :]
    s0f, s0b, h0f, h0b = _ctx_call(feat_c, cw, cb, wg, b_r[0], b_i[0], lam[0], batch=batch, n_ctx=n_ctx)

    o_b = _gla_call(feat, feat_h, F_F1, s0b, reverse=True, batch=batch, seq=seq, rows=GLA_ROWS)
    oa = _gla_call(feat, feat_h, F_F0, s0f, o_b, reverse=False, batch=batch, seq=seq, rows=GLA_ROWS)
    hx = _lru_call(feat, cw, cb, wg, b_r[0], b_i[0], lam[0], h0f, h0b, batch=batch, seq=seq)

    out = _merge_call(oa, hx, feat_h, x2, mod3, norm_a_g[0][None, :],
                      p_a[0].astype(BF16), p_b[0].astype(BF16), w_out[0].astype(BF16),
                      ln_g[0][None, :], ln_b[0][None, :], tm=MERGE_TM, tiles_per_batch=seq // MERGE_TM)
    return out.reshape(batch, seq, D_MODEL)
```

```python
import functools

import jax
import jax.numpy as jnp
from jax import lax
from jax.experimental import pallas as pl
from jax.experimental.pallas import tpu as pltpu

F32 = jnp.float32
BF16 = jnp.bfloat16

D_MODEL = 1024
GRID_W = 64
HEAD_DIM = 128
N_HEADS = D_MODEL // HEAD_DIM
N_BLOCKS = 8
BLOCK_DIM = D_MODEL // N_BLOCKS
N_CONV = 4
RG_C = 8.0
LN_EPS = 1e-5
RMS_EPS = 1e-6
DEEPNORM_ALPHA = 2.0 ** 0.25
Q_SCALE = HEAD_DIM ** -0.5

V7X_VMEM_BYTES = 64 * 1024 * 1024
VMEM_LIMIT = V7X_VMEM_BYTES - 8 * 1024 * 1024
SUBLANES = 8

INPROJ_TM = 1024
GLA_ROWS = 16
LRU_ROWS_PER_STEP = 16
MERGE_TM = 512

LATENT_F32 = (0, 1, 2, 5)
LATENT_BF16 = (4, 6, 7, 8, 3)
CTX_F32 = (1, 2, 3, 5)
F_Q, F_F0, F_F1, F_Z5 = 0, 1, 2, 3
H_G4, H_G6, H_M7, H_M8, H_V = 0, 1, 2, 3, 4
HALVED_BLOCKS = (0, 1, 2, 4, 6, 7, 8)


def _silu(x):
    return _half_silu(0.5 * x)


def _half_silu(h):
    return h * jnp.tanh(h) + h


def _half_sigmoid(h):
    return 0.5 * jnp.tanh(h) + 0.5


def _half_forget(h, lb):
    return (0.5 + 0.5 * lb) + (0.5 - 0.5 * lb) * jnp.tanh(h)


def _dot(a, b):
    return jnp.dot(a, b, preferred_element_type=F32)


def _dot_tn(a, b):
    return lax.dot_general(a, b, (((0,), (0,)), ((), ())), preferred_element_type=F32)


def _chunk_cumprod(x, reverse):
    n, c = x.shape
    nb = n // 8
    y = x.reshape(nb, 8, c)
    sub = lax.broadcasted_iota(jnp.int32, (nb, 8, c), 1)
    for s in (1, 2, 4):
        if reverse:
            y = y * jnp.where(sub < 8 - s, pltpu.roll(y, 8 - s, axis=1), 1.0)
        else:
            y = y * jnp.where(sub >= s, pltpu.roll(y, s, axis=1), 1.0)
    offs = [None] * nb
    acc = jnp.ones((1, c), F32)
    for b in (reversed(range(nb)) if reverse else range(nb)):
        offs[b] = acc
        acc = acc * (y[b, 0:1, :] if reverse else y[b, 7:8, :])
    return (y * jnp.stack(offs)).reshape(n, c), acc


def _cumsum_rows(mask_bf16, x):
    hi = x.astype(BF16)
    r1 = x - hi.astype(F32)
    mid = r1.astype(BF16)
    lo = (r1 - mid.astype(F32)).astype(BF16)
    return _dot(mask_bf16, hi) + _dot(mask_bf16, mid) + _dot(mask_bf16, lo)


def _mod_kernel(c_ref, cc_ref, w_ref, b_ref, lbl_ref, mod_ref, lb_ref, rows_ref):
    n = c_ref.shape[0]
    rows_ref[...] = jnp.zeros(rows_ref.shape, F32)
    rows_ref[0:n, :] = c_ref[...]
    rows_ref[n:n + 1, :] = cc_ref[...]
    mod_ref[...] = _dot(_silu(rows_ref[...]), w_ref[...]) + b_ref[...]
    l = lbl_ref[...]
    e = jnp.exp(l - jnp.max(l, axis=0, keepdims=True))
    lb_ref[...] = e[0] / jnp.sum(e, axis=0)


def _mod_call(c, c_ctx, w_mod, b_mod, lb_logits):
    return pl.pallas_call(
        _mod_kernel,
        out_shape=(jax.ShapeDtypeStruct((SUBLANES, 3 * D_MODEL), F32),
                   jax.ShapeDtypeStruct((2, D_MODEL), F32)),
        scratch_shapes=[pltpu.VMEM((SUBLANES, D_MODEL), F32)],
        compiler_params=pltpu.CompilerParams(vmem_limit_bytes=VMEM_LIMIT),
        name="mod",
    )(c, c_ctx, w_mod, b_mod, lb_logits)


def _inproj_kernel(x_ref, mod_ref, *refs, head_major, acts):
    n_col = len(acts)
    if any(a and a.startswith("forget") for a in acts):
        lb_ref, refs = refs[0], refs[1:]
    w_refs, b_refs, o_ref = refs[:n_col], refs[n_col:2 * n_col], refs[-1]
    u = (x_ref[...] * (1.0 + mod_ref[1:2, :]) + mod_ref[0:1, :]).astype(BF16)
    for j in range(n_col):
        z = _dot(u, w_refs[j][...]) + b_refs[j][...]
        if acts[j] == "q":
            z = _half_silu(z) * Q_SCALE
        elif acts[j] == "silu":
            z = _half_silu(z)
        elif acts[j] == "sig":
            z = _half_sigmoid(z)
        elif acts[j] is not None:
            d = int(acts[j][-1])
            z = _half_forget(z, lb_ref[d:d + 1, :])
        z = z.astype(o_ref.dtype)
        if head_major:
            for h in range(N_HEADS):
                o_ref[j, h] = z[:, h * HEAD_DIM:(h + 1) * HEAD_DIM]
        else:
            o_ref[j] = z


def _inproj_call(x2, mod3, w, b, cols, acts, lb=None, *, out_dtype, tm, tiles_per_mod, head_major=False):
    n_tok = x2.shape[0]
    n_col = len(cols)
    tile = (N_HEADS, tm, HEAD_DIM) if head_major else (tm, D_MODEL)
    col_block = lambda rows: [pl.BlockSpec((rows, D_MODEL), lambda i, c=c: (0, c), pipeline_mode=pl.Buffered(1))
                              for c in cols]
    return pl.pallas_call(
        functools.partial(_inproj_kernel, head_major=head_major, acts=acts),
        grid=(n_tok // tm,),
        in_specs=[
            pl.BlockSpec((tm, D_MODEL), lambda i: (i, 0)),
            pl.BlockSpec((None, 3, D_MODEL), lambda i: (i // tiles_per_mod, 0, 0)),
            *([] if lb is None else [pl.BlockSpec((2, D_MODEL), lambda i: (0, 0))]),
            *col_block(D_MODEL),
            *col_block(1),
        ],
        out_specs=pl.BlockSpec((None, n_col) + tile, lambda i: (i, 0) + (0,) * len(tile)),
        out_shape=jax.ShapeDtypeStruct((n_tok // tm, n_col) + tile, out_dtype),
        compiler_params=pltpu.CompilerParams(
            dimension_semantics=("arbitrary",), vmem_limit_bytes=VMEM_LIMIT),
        name="inproj",
    )(x2, mod3, *([] if lb is None else [lb]), *([w] * n_col), *([b] * n_col))


LOG2_E = 1.4426950408889634


def _lru_ab(hx, h_r, h_i, br, bi, sp):
    k = (-0.5 * RG_C * LOG2_E) * sp
    a = jnp.exp2(k * jnp.tanh(h_r + 0.5 * br) + k)
    y = 1.0 - a * a
    mult = jnp.where(y > 0.0, y * lax.rsqrt(y), 0.0)
    return a, mult * (hx * jnp.tanh(h_i + 0.5 * bi) + hx)


def _softplus(y):
    return jnp.maximum(y, 0.0) + jnp.log(1.0 + jnp.exp(-jnp.abs(y)))


def _ctx_kernel(ff_ref, fb_ref, v_ref, z5_ref, cw_ref, cb_ref, wg_ref, br_ref, bi_ref, lam_ref,
                sf_ref, sb_ref, hf_ref, hb_ref, zp_ref, a_ref, b_ref):
    n = ff_ref.shape[0]
    ri = lax.broadcasted_iota(jnp.int32, (n, n), 0)
    ci = lax.broadcasted_iota(jnp.int32, (n, n), 1)
    tril = (ci <= ri).astype(F32).astype(BF16)
    v = v_ref[...].astype(BF16)

    f = ff_ref[...]
    g = _cumsum_rows(tril, jnp.log(f))
    ke = ((1.0 - f) * jnp.exp(g[n - 1:n, :] - g)).astype(BF16)
    for h in range(N_HEADS):
        sl = slice(h * HEAD_DIM, (h + 1) * HEAD_DIM)
        sf_ref[h] = _dot_tn(v[:, sl], ke[:, sl])
    f = fb_ref[...]
    lf = jnp.log(f)
    ke = ((1.0 - f) * jnp.exp(_cumsum_rows(tril, lf) - lf)).astype(BF16)
    for h in range(N_HEADS):
        sl = slice(h * HEAD_DIM, (h + 1) * HEAD_DIM)
        sb_ref[h] = _dot_tn(v[:, sl], ke[:, sl])

    pad = SUBLANES
    zp_ref[0:pad, :] = jnp.zeros((pad, D_MODEL), F32)
    zp_ref[pad + n:2 * pad + n, :] = jnp.zeros((pad, D_MODEL), F32)
    zp_ref[pad:pad + n, :] = z5_ref[...]
    first = pad - (N_CONV - 1) // 2
    xc = cb_ref[...] + zp_ref[first:first + n, :] * cw_ref[0:1, :]
    for kk in range(1, N_CONV):
        xc = xc + zp_ref[first + kk:first + kk + n, :] * cw_ref[kk:kk + 1, :]

    for d, h_ref in ((0, hf_ref), (1, hb_ref)):
        sp = _softplus(-lam_ref[d:d + 1, :])
        for blk in range(N_BLOCKS):
            sl = slice(blk * BLOCK_DIM, (blk + 1) * BLOCK_DIM)
            g = _dot(xc[:, sl].astype(BF16), wg_ref[blk, :, 2 * d * BLOCK_DIM:2 * (d + 1) * BLOCK_DIM])
            a, b = _lru_ab(0.5 * xc[:, sl], g[:, :BLOCK_DIM], g[:, BLOCK_DIM:],
                           br_ref[d:d + 1, sl], bi_ref[d:d + 1, sl], sp[:, sl])
            a_ref[:, sl] = a
            b_ref[:, sl] = b

        def step(t, h, d=d):
            tt = (n - 1 - t) if d == 1 else t
            return a_ref[pl.ds(tt, 1), :] * h + b_ref[pl.ds(tt, 1), :]

        h_ref[...] = lax.fori_loop(0, n, step, jnp.zeros((1, D_MODEL), F32), unroll=8)


def _ctx_call(feat_c, conv_w, conv_b, wg, br, bi, lam, *, batch, n_ctx):
    feat = lambda j: pl.BlockSpec((None, None, n_ctx, D_MODEL), lambda b, j=j: (b, j, 0, 0))
    full = lambda shape: pl.BlockSpec(shape, lambda b: (0,) * len(shape))
    state = pl.BlockSpec((None, N_HEADS, HEAD_DIM, HEAD_DIM), lambda b: (b, 0, 0, 0))
    hvec = pl.BlockSpec((None, 1, D_MODEL), lambda b: (b, 0, 0))
    return pl.pallas_call(
        _ctx_kernel,
        grid=(batch,),
        in_specs=[feat(0), feat(1), feat(2), feat(3),
                  full((N_CONV, D_MODEL)), full((1, D_MODEL)),
                  full((N_BLOCKS, BLOCK_DIM, 4 * BLOCK_DIM)),
                  full((2, D_MODEL)), full((2, D_MODEL)), full((2, D_MODEL))],
        out_specs=(state, state, hvec, hvec),
        out_shape=(jax.ShapeDtypeStruct((batch, N_HEADS, HEAD_DIM, HEAD_DIM), F32),
                   jax.ShapeDtypeStruct((batch, N_HEADS, HEAD_DIM, HEAD_DIM), F32),
                   jax.ShapeDtypeStruct((batch, 1, D_MODEL), F32),
                   jax.ShapeDtypeStruct((batch, 1, D_MODEL), F32)),
        scratch_shapes=[pltpu.VMEM((n_ctx + 2 * SUBLANES, D_MODEL), F32),
                        pltpu.VMEM((n_ctx, D_MODEL), F32),
                        pltpu.VMEM((n_ctx, D_MODEL), F32)],
        compiler_params=pltpu.CompilerParams(
            dimension_semantics=("arbitrary",), vmem_limit_bytes=VMEM_LIMIT),
        name="ctx_states",
    )(feat_c, feat_c, feat_c, feat_c, conv_w, conv_b, wg, br, bi, lam)


GLA_ROWS_PER_ITER = 16


def _gla_kernel(*refs, reverse, rows, finish):
    if finish:
        (q_ref, f_ref, v_ref, s0_ref, other_ref, o_ref,
         st_ref, sn_ref, qd_ref, kit_ref, ke_ref, dec_ref) = refs
    else:
        q_ref, f_ref, v_ref, s0_ref, o_ref, st_ref, sn_ref, qd_ref, kit_ref, ke_ref, dec_ref = refs

    @pl.when(pl.program_id(1) == 0)
    def _():
        st_ref[...] = s0_ref[...]
        for h in range(N_HEADS):
            sn_ref[h] = s0_ref[h].astype(BF16).T

    ri = lax.broadcasted_iota(jnp.int32, (GRID_W, GRID_W), 0)
    ci = lax.broadcasted_iota(jnp.int32, (GRID_W, GRID_W), 1)
    allow = (ci >= ri) if reverse else (ci <= ri)

    def row_offset(n):
        n = jnp.minimum(n, rows - 1)
        return pl.multiple_of(((rows - 1 - n) if reverse else n) * GRID_W, GRID_W)

    def prepare(n, slot):
        off = row_offset(n)
        q = q_ref[pl.ds(off, GRID_W), :]
        f = f_ref[pl.ds(off, GRID_W), :]
        dg, dec = _chunk_cumprod(f, reverse)
        ki = (1.0 - f) / dg
        qd = (q * dg).astype(BF16)
        ke = (ki * dec).astype(BF16)
        kit_ref[slot] = ki.astype(BF16).T
        dec_ref[slot] = dec
        for h in range(N_HEADS):
            sl = slice(h * HEAD_DIM, (h + 1) * HEAD_DIM)
            qd_ref[slot, h] = qd[:, sl]
            ke_ref[slot, h] = ke[:, sl]

    def contract(n, slot):
        off = row_offset(n)
        for h in range(N_HEADS):
            sl = slice(h * HEAD_DIM, (h + 1) * HEAD_DIM)
            qd = qd_ref[slot, h]
            v = v_ref[h, pl.ds(off, GRID_W), :]
            p = jnp.where(allow, _dot(qd, kit_ref[slot, sl, :]), 0.0).astype(BF16)
            o = _dot(p, v) + _dot(qd, sn_ref[h])
            st = st_ref[h] * dec_ref[slot, :, sl] + _dot_tn(v, ke_ref[slot, h])
            st_ref[h] = st
            sn_ref[h] = st.astype(BF16).T
            if finish:
                o = o + other_ref[h, pl.ds(off, GRID_W), :].astype(F32)
            o_ref[h, pl.ds(off, GRID_W), :] = o.astype(o_ref.dtype)

    prepare(0, 0)

    def body(m, carry):
        n = GLA_ROWS_PER_ITER * m
        for i in range(GLA_ROWS_PER_ITER):
            contract(n + i, i % 2)
            prepare(n + i + 1, (i + 1) % 2)
        return carry

    lax.fori_loop(0, rows // GLA_ROWS_PER_ITER, body, 0)


def _gla_call(feat, feat_h, f_col, s0, other=None, *, reverse, batch, seq, rows):
    tb = rows * GRID_W
    nrb = seq // tb
    finish = other is not None
    per_tile = feat.shape[2] // tb

    def row_block(b, i):
        return b * nrb + ((nrb - 1 - i) if reverse else i)

    def col(c):
        return pl.BlockSpec((None, None, tb, D_MODEL), lambda b, i, c=c: (
            row_block(b, i) // per_tile, c, row_block(b, i) % per_tile, 0))

    tok = pl.BlockSpec((N_HEADS, tb, HEAD_DIM), lambda b, i: (0, row_block(b, i), 0))
    v_spec = pl.BlockSpec((None, None, N_HEADS, tb, HEAD_DIM), lambda b, i: (
        row_block(b, i) // per_tile, H_V, 0, row_block(b, i) % per_tile, 0))
    in_specs = [col(F_Q), col(f_col), v_spec,
                pl.BlockSpec((None, N_HEADS, HEAD_DIM, HEAD_DIM), lambda b, i: (b, 0, 0, 0))]
    args = [feat, feat, feat_h, s0]
    if finish:
        in_specs += [tok]
        args += [other]
    return pl.pallas_call(
        functools.partial(_gla_kernel, reverse=reverse, rows=rows, finish=finish),
        grid=(batch, nrb),
        in_specs=in_specs,
        out_specs=tok,
        out_shape=jax.ShapeDtypeStruct((N_HEADS, batch * seq, HEAD_DIM), BF16),
        scratch_shapes=[pltpu.VMEM((N_HEADS, HEAD_DIM, HEAD_DIM), F32)]
        + [pltpu.VMEM((N_HEADS, HEAD_DIM, HEAD_DIM), BF16), pltpu.VMEM((2, N_HEADS, GRID_W, HEAD_DIM), BF16),
           pltpu.VMEM((2, D_MODEL, GRID_W), BF16), pltpu.VMEM((2, N_HEADS, GRID_W, HEAD_DIM), BF16),
           pltpu.VMEM((2, 1, D_MODEL), F32)],
        compiler_params=pltpu.CompilerParams(
            dimension_semantics=("arbitrary", "arbitrary"), vmem_limit_bytes=VMEM_LIMIT),
        name="gla_bwd" if reverse else "gla_fwd",
    )(*args)


def _lru_kernel(z5_ref, cw_ref, cb_ref, wg_ref, br_ref, bi_ref, lam_ref, h0f_ref, h0b_ref, o_ref,
                zp_ref, af_ref, bf_ref, ab_ref, bb_ref, cf_ref, cr_ref, *, n_rows, rows_per_step):
    w = GRID_W
    seq = n_rows * w
    tb = rows_per_step * w
    n_steps = n_rows // rows_per_step

    zp_ref[0:w, :] = jnp.zeros((w, BLOCK_DIM), F32)
    zp_ref[w + seq:, :] = jnp.zeros((2 * w, BLOCK_DIM), F32)
    tile = z5_ref.shape[1]
    for k in range(z5_ref.shape[0]):
        zp_ref[w + k * tile:w + (k + 1) * tile, :] = z5_ref[k]

    sp = [_softplus(-lam_ref[d:d + 1, :]) for d in (0, 1)]

    def gates(s, c):
        off = pl.multiple_of(s * tb, tb)
        xc = cb_ref[...] + zp_ref[pl.ds(off, tb), :] * cw_ref[0:1, :]
        for kk in range(1, N_CONV):
            xc = xc + zp_ref[pl.ds(off + kk * w, tb), :] * cw_ref[kk:kk + 1, :]
        g = _dot(xc.astype(BF16), wg_ref[...])
        hx = 0.5 * xc
        for d, a_ref, b_ref in ((0, af_ref, bf_ref), (1, ab_ref, bb_ref)):
            c0 = 2 * d * BLOCK_DIM
            a, b = _lru_ab(hx, g[:, c0:c0 + BLOCK_DIM], g[:, c0 + BLOCK_DIM:c0 + 2 * BLOCK_DIM],
                           br_ref[d:d + 1, :], bi_ref[d:d + 1, :], sp[d])
            a_ref[pl.ds(off, tb), :] = a
            b_ref[pl.ds(off, tb), :] = b
        return c

    lax.fori_loop(0, n_steps, gates, 0, unroll=True)

    def scan(n, carry):
        hf, pf, hb, pb = carry
        off_f = pl.multiple_of(n * w, w)
        off_b = pl.multiple_of((n_rows - 1 - n) * w, w)
        a = af_ref[pl.ds(off_f, w), :]
        hf = a * hf + bf_ref[pl.ds(off_f, w), :]
        pf = pf * a
        bf_ref[pl.ds(off_f, w), :] = hf
        af_ref[pl.ds(off_f, w), :] = pf
        a = ab_ref[pl.ds(off_b, w), :]
        hb = a * hb + bb_ref[pl.ds(off_b, w), :]
        pb = pb * a
        bb_ref[pl.ds(off_b, w), :] = hb
        ab_ref[pl.ds(off_b, w), :] = pb
        return hf, pf, hb, pb

    zeros = jnp.zeros((w, BLOCK_DIM), F32)
    ones = jnp.ones((w, BLOCK_DIM), F32)
    lax.fori_loop(0, n_rows, scan, (zeros, ones, zeros, ones), unroll=16)

    last = (n_rows - 1) * w

    def carry(n, c):
        cf, cb = c
        cf_ref[pl.ds(n, 1), :] = cf
        cf = bf_ref[pl.ds(last + n, 1), :] + af_ref[pl.ds(last + n, 1), :] * cf
        col = w - 1 - n
        cr_ref[pl.ds(col, 1), :] = cb
        cb = bb_ref[pl.ds(col, 1), :] + ab_ref[pl.ds(col, 1), :] * cb
        return cf, cb

    lax.fori_loop(0, w, carry, (h0f_ref[...], h0b_ref[...]), unroll=8)

    def fix(s, c):
        off = pl.multiple_of(s * tb, tb)
        cf = jnp.concatenate([cf_ref[...]] * rows_per_step, axis=0)
        cr = jnp.concatenate([cr_ref[...]] * rows_per_step, axis=0)
        h = (bf_ref[pl.ds(off, tb), :] + af_ref[pl.ds(off, tb), :] * cf
             + bb_ref[pl.ds(off, tb), :] + ab_ref[pl.ds(off, tb), :] * cr)
        o_ref[pl.ds(off, tb), :] = h.astype(o_ref.dtype)
        return c

    lax.fori_loop(0, n_steps, fix, 0, unroll=True)


def _lru_call(feat, conv_w, conv_b, wg, br, bi, lam, h0f, h0b, *, batch, seq):
    n_rows = seq // GRID_W
    vec = lambda n: pl.BlockSpec((n, BLOCK_DIM), lambda b, c: (0, c))
    h0 = pl.BlockSpec((None, 1, BLOCK_DIM), lambda b, c: (b, 0, c))
    buf = pltpu.VMEM((seq, BLOCK_DIM), F32)
    n_i, n_col, tm, _ = feat.shape
    feat5 = feat.reshape(batch, n_i // batch, n_col, tm, D_MODEL)
    return pl.pallas_call(
        functools.partial(_lru_kernel, n_rows=n_rows, rows_per_step=LRU_ROWS_PER_STEP),
        grid=(batch, N_BLOCKS),
        in_specs=[pl.BlockSpec((None, n_i // batch, None, tm, BLOCK_DIM), lambda b, c: (b, 0, F_Z5, 0, c)),
                  vec(N_CONV), vec(1),
                  pl.BlockSpec((None, BLOCK_DIM, 4 * BLOCK_DIM), lambda b, c: (c, 0, 0)),
                  vec(2), vec(2), vec(2), h0, h0],
        out_specs=pl.BlockSpec((None, seq, BLOCK_DIM), lambda b, c: (c, b, 0)),
        out_shape=jax.ShapeDtypeStruct((N_BLOCKS, batch * seq, BLOCK_DIM), BF16),
        scratch_shapes=[pltpu.VMEM((seq + 3 * GRID_W, BLOCK_DIM), F32), buf, buf, buf, buf,
                        pltpu.VMEM((GRID_W, BLOCK_DIM), F32), pltpu.VMEM((GRID_W, BLOCK_DIM), F32)],
        compiler_params=pltpu.CompilerParams(
            dimension_semantics=("arbitrary", "arbitrary"), vmem_limit_bytes=VMEM_LIMIT),
        name="lru",
    )(feat5, conv_w, conv_b, wg, br, bi, lam, h0f, h0b)


def _merge_kernel(oa_ref, hx_ref, g4_ref, g6_ref, m7_ref, m8_ref, x_ref, mod_ref, ng_ref,
                  pa_ref, pb_ref, wo_ref, lg_ref, lbias_ref, o_ref):
    f32 = lambda ref: jnp.concatenate([ref[h] for h in range(N_HEADS)], axis=-1).astype(F32)
    o_b = (f32(hx_ref) * f32(g6_ref)).astype(BF16)
    y = f32(m8_ref) * _dot(o_b, pb_ref[...])
    y_a = None
    for pair in range(N_HEADS // 2):
        o_h = []
        for h in (2 * pair, 2 * pair + 1):
            t = oa_ref[h].astype(F32)
            ms = jnp.mean(t * t, axis=-1, keepdims=True)
            o_h.append((t * lax.rsqrt(ms + RMS_EPS) * ng_ref[...] * g4_ref[h].astype(F32)).astype(BF16))
        rows = slice(2 * pair * HEAD_DIM, 2 * (pair + 1) * HEAD_DIM)
        part = _dot(jnp.concatenate(o_h, axis=-1), pa_ref[rows, :])
        y_a = part if y_a is None else y_a + part
    y = y + f32(m7_ref) * y_a
    y = _dot(y.astype(BF16), wo_ref[...])
    t = DEEPNORM_ALPHA * x_ref[...] + mod_ref[2:3, :] * y
    mu = jnp.mean(t, axis=-1, keepdims=True)
    tc = t - mu
    var = jnp.mean(tc * tc, axis=-1, keepdims=True)
    o_ref[...] = tc * lax.rsqrt(var + LN_EPS) * lg_ref[...] + lbias_ref[...]


def _merge_call(oa, hx, feat_h, x2, mod3, ng, pa, pb, wo, lg, lbias, *, tm, tiles_per_batch):
    n_tok = x2.shape[0]
    tok = pl.BlockSpec((tm, D_MODEL), lambda i: (i, 0))
    heads = pl.BlockSpec((N_HEADS, tm, HEAD_DIM), lambda i: (0, i, 0))
    per_tile = feat_h.shape[3] // tm
    col = lambda j: pl.BlockSpec((None, None, N_HEADS, tm, HEAD_DIM),
                                 lambda i, j=j: (i // per_tile, j, 0, i % per_tile, 0))
    full = lambda shape: pl.BlockSpec(shape, lambda i: (0,) * len(shape), pipeline_mode=pl.Buffered(1))
    return pl.pallas_call(
        _merge_kernel,
        grid=(n_tok // tm,),
        in_specs=[heads, heads, col(H_G4), col(H_G6), col(H_M7), col(H_M8), tok,
                  pl.BlockSpec((None, 3, D_MODEL), lambda i: (i // tiles_per_batch, 0, 0)),
                  full((1, HEAD_DIM)), full((D_MODEL, D_MODEL)), full((D_MODEL, D_MODEL)),
                  full((D_MODEL, D_MODEL)), full((1, D_MODEL)), full((1, D_MODEL))],
        out_specs=tok,
        out_shape=jax.ShapeDtypeStruct((n_tok, D_MODEL), F32),
        compiler_params=pltpu.CompilerParams(
            dimension_semantics=("arbitrary",), vmem_limit_bytes=VMEM_LIMIT),
        name="merge",
    )(oa, hx, feat_h, feat_h, feat_h, feat_h, x2, mod3, ng, pa, pb, wo, lg, lbias)


def kernel(x, c, ctx, c_ctx, w_mod, b_mod, w_in, b_in, lb_logits, norm_a_g, conv_w, conv_b,
           w_r, b_r, w_i, b_i, lam, p_a, p_b, w_out, ln_g, ln_b):
    batch, seq, d = x.shape
    n_ctx = ctx.shape[1]
    assert d == D_MODEL and seq % GRID_W == 0 and w_in.shape[0] == 1

    n_blocks = w_in.shape[-1] // D_MODEL
    col_scale = jnp.repeat(jnp.array([0.5 if j in HALVED_BLOCKS else 1.0 for j in range(n_blocks)], F32), D_MODEL)
    w_bf = (w_in[0] * col_scale).astype(BF16)
    b2 = (b_in[0] * col_scale)[None, :]
    wg = (0.5 * jnp.concatenate([w_r[0, 0], w_i[0, 0], w_r[0, 1], w_i[0, 1]], axis=-1)).astype(BF16)
    assert batch < SUBLANES
    mod, lb = _mod_call(c, c_ctx[None, :], w_mod[0], b_mod[0][None, :], lb_logits)
    mod3 = mod.reshape(SUBLANES, 3, D_MODEL)

    x2 = x.reshape(batch * seq, D_MODEL)
    tm = INPROJ_TM
    feat = _inproj_call(x2, mod3, w_bf, b2, LATENT_F32, ("q", "forget0", "forget1", None), lb,
                        out_dtype=F32, tm=tm, tiles_per_mod=seq // tm)
    feat_h = _inproj_call(x2, mod3, w_bf, b2, LATENT_BF16, ("silu", "silu", "sig", "sig", None),
                          out_dtype=BF16, tm=tm, tiles_per_mod=seq // tm, head_major=True)
    feat_c = _inproj_call(ctx.reshape(batch * n_ctx, D_MODEL), mod3[batch:batch + 1], w_bf, b2, CTX_F32,
                          ("forget0", "forget1", None, None), lb,
                          out_dtype=F32, tm=n_ctx, tiles_per_mod=batch)

    cw, cb = conv_w[0], conv_b[0][None, :]
    s0f, s0b, h0f, h0b = _ctx_call(feat_c, cw, cb, wg, b_r[0], b_i[0], lam[0], batch=batch, n_ctx=n_ctx)

    o_b = _gla_call(feat, feat_h, F_F1, s0b, reverse=True, batch=batch, seq=seq, rows=GLA_ROWS)
    oa = _gla_call(feat, feat_h, F_F0, s0f, o_b, reverse=False, batch=batch, seq=seq, rows=GLA_ROWS)
    hx = _lru_call(feat, cw, cb, wg, b_r[0], b_i[0], lam[0], h0f, h0b, batch=batch, seq=seq)

    out = _merge_call(oa, hx, feat_h, x2, mod3, norm_a_g[0][None, :],
                      p_a[0].astype(BF16), p_b[0].astype(BF16), w_out[0].astype(BF16),
                      ln_g[0][None, :], ln_b[0][None, :], tm=MERGE_TM, tiles_per_batch=seq // MERGE_TM)
    return out.reshape(batch, seq, D_MODEL)
```

```python
import functools

import jax
import jax.numpy as jnp
from jax import lax
from jax.experimental import pallas as pl
from jax.experimental.pallas import tpu as pltpu

F32 = jnp.float32
BF16 = jnp.bfloat16

D_MODEL = 1024
GRID_W = 64
HEAD_DIM = 128
N_HEADS = D_MODEL // HEAD_DIM
N_BLOCKS = 8
BLOCK_DIM = D_MODEL // N_BLOCKS
N_CONV = 4
RG_C = 8.0
LN_EPS = 1e-5
RMS_EPS = 1e-6
DEEPNORM_ALPHA = 2.0 ** 0.25
Q_SCALE = HEAD_DIM ** -0.5

V7X_VMEM_BYTES = 64 * 1024 * 1024
VMEM_LIMIT = V7X_VMEM_BYTES - 8 * 1024 * 1024
SUBLANES = 8

INPROJ_TM = 1024
GLA_ROWS = 16
LRU_ROWS_PER_STEP = 16
MERGE_TM = 512

LATENT_F32 = (0, 1, 2, 5)
LATENT_BF16 = (4, 6, 7, 8, 3)
CTX_F32 = (1, 2, 3, 5)
F_Q, F_F0, F_F1, F_Z5 = 0, 1, 2, 3
H_G4, H_G6, H_M7, H_M8, H_V = 0, 1, 2, 3, 4
HALVED_BLOCKS = (0, 1, 2, 4, 6, 7, 8)


def _silu(x):
    return _half_silu(0.5 * x)


def _half_silu(h):
    return h * jnp.tanh(h) + h


def _half_sigmoid(h):
    return 0.5 * jnp.tanh(h) + 0.5


def _half_forget(h, lb):
    return (0.5 + 0.5 * lb) + (0.5 - 0.5 * lb) * jnp.tanh(h)


def _dot(a, b):
    return jnp.dot(a, b, preferred_element_type=F32)


def _dot_tn(a, b):
    return lax.dot_general(a, b, (((0,), (0,)), ((), ())), preferred_element_type=F32)


def _chunk_cumprod(x, reverse):
    n, c = x.shape
    nb = n // 8
    y = x.reshape(nb, 8, c)
    sub = lax.broadcasted_iota(jnp.int32, (nb, 8, c), 1)
    for s in (1, 2, 4):
        if reverse:
            y = y * jnp.where(sub < 8 - s, pltpu.roll(y, 8 - s, axis=1), 1.0)
        else:
            y = y * jnp.where(sub >= s, pltpu.roll(y, s, axis=1), 1.0)
    offs = [None] * nb
    acc = jnp.ones((1, c), F32)
    for b in (reversed(range(nb)) if reverse else range(nb)):
        offs[b] = acc
        acc = acc * (y[b, 0:1, :] if reverse else y[b, 7:8, :])
    return (y * jnp.stack(offs)).reshape(n, c), acc


def _cumsum_rows(mask_bf16, x):
    hi = x.astype(BF16)
    r1 = x - hi.astype(F32)
    mid = r1.astype(BF16)
    lo = (r1 - mid.astype(F32)).astype(BF16)
    return _dot(mask_bf16, hi) + _dot(mask_bf16, mid) + _dot(mask_bf16, lo)


def _mod_kernel(c_ref, cc_ref, w_ref, b_ref, lbl_ref, mod_ref, lb_ref, rows_ref):
    n = c_ref.shape[0]
    rows_ref[...] = jnp.zeros(rows_ref.shape, F32)
    rows_ref[0:n, :] = c_ref[...]
    rows_ref[n:n + 1, :] = cc_ref[...]
    mod_ref[...] = _dot(_silu(rows_ref[...]), w_ref[...]) + b_ref[...]
    l = lbl_ref[...]
    e = jnp.exp(l - jnp.max(l, axis=0, keepdims=True))
    lb_ref[...] = e[0] / jnp.sum(e, axis=0)


def _mod_call(c, c_ctx, w_mod, b_mod, lb_logits):
    return pl.pallas_call(
        _mod_kernel,
        out_shape=(jax.ShapeDtypeStruct((SUBLANES, 3 * D_MODEL), F32),
                   jax.ShapeDtypeStruct((2, D_MODEL), F32)),
        scratch_shapes=[pltpu.VMEM((SUBLANES, D_MODEL), F32)],
        compiler_params=pltpu.CompilerParams(vmem_limit_bytes=VMEM_LIMIT),
        name="mod",
    )(c, c_ctx, w_mod, b_mod, lb_logits)


def _inproj_kernel(x_ref, mod_ref, *refs, head_major, acts):
    n_col = len(acts)
    if any(a and a.startswith("forget") for a in acts):
        lb_ref, refs = refs[0], refs[1:]
    w_refs, b_refs, o_ref = refs[:n_col], refs[n_col:2 * n_col], refs[-1]
    u = (x_ref[...] * (1.0 + mod_ref[1:2, :]) + mod_ref[0:1, :]).astype(BF16)
    for j in range(n_col):
        z = _dot(u, w_refs[j][...]) + b_refs[j][...]
        if acts[j] == "q":
            z = _half_silu(z) * Q_SCALE
        elif acts[j] == "silu":
            z = _half_silu(z)
        elif acts[j] == "sig":
            z = _half_sigmoid(z)
        elif acts[j] is not None:
            d = int(acts[j][-1])
            z = _half_forget(z, lb_ref[d:d + 1, :])
        z = z.astype(o_ref.dtype)
        if head_major:
            for h in range(N_HEADS):
                o_ref[j, h] = z[:, h * HEAD_DIM:(h + 1) * HEAD_DIM]
        else:
            o_ref[j] = z


def _inproj_call(x2, mod3, w, b, cols, acts, lb=None, *, out_dtype, tm, tiles_per_mod, head_major=False):
    n_tok = x2.shape[0]
    n_col = len(cols)
    tile = (N_HEADS, tm, HEAD_DIM) if head_major else (tm, D_MODEL)
    col_block = lambda rows: [pl.BlockSpec((rows, D_MODEL), lambda i, c=c: (0, c), pipeline_mode=pl.Buffered(1))
                              for c in cols]
    return pl.pallas_call(
        functools.partial(_inproj_kernel, head_major=head_major, acts=acts),
        grid=(n_tok // tm,),
        in_specs=[
            pl.BlockSpec((tm, D_MODEL), lambda i: (i, 0)),
            pl.BlockSpec((None, 3, D_MODEL), lambda i: (i // tiles_per_mod, 0, 0)),
            *([] if lb is None else [pl.BlockSpec((2, D_MODEL), lambda i: (0, 0))]),
            *col_block(D_MODEL),
            *col_block(1),
        ],
        out_specs=pl.BlockSpec((None, n_col) + tile, lambda i: (i, 0) + (0,) * len(tile)),
        out_shape=jax.ShapeDtypeStruct((n_tok // tm, n_col) + tile, out_dtype),
        compiler_params=pltpu.CompilerParams(
            dimension_semantics=("arbitrary",), vmem_limit_bytes=VMEM_LIMIT),
        name="inproj",
    )(x2, mod3, *([] if lb is None else [lb]), *([w] * n_col), *([b] * n_col))


LOG2_E = 1.4426950408889634


def _lru_ab(hx, h_r, h_i, br, bi, sp):
    k = (-0.5 * RG_C * LOG2_E) * sp
    a = jnp.exp2(k * jnp.tanh(h_r + 0.5 * br) + k)
    y = 1.0 - a * a
    mult = jnp.where(y > 0.0, y * lax.rsqrt(y), 0.0)
    return a, mult * (hx * jnp.tanh(h_i + 0.5 * bi) + hx)


def _softplus(y):
    return jnp.maximum(y, 0.0) + jnp.log(1.0 + jnp.exp(-jnp.abs(y)))


def _ctx_kernel(ff_ref, fb_ref, v_ref, z5_ref, cw_ref, cb_ref, wg_ref, br_ref, bi_ref, lam_ref,
                sf_ref, sb_ref, hf_ref, hb_ref, zp_ref, a_ref, b_ref):
    n = ff_ref.shape[0]
    ri = lax.broadcasted_iota(jnp.int32, (n, n), 0)
    ci = lax.broadcasted_iota(jnp.int32, (n, n), 1)
    tril = (ci <= ri).astype(F32).astype(BF16)
    v = v_ref[...].astype(BF16)

    f = ff_ref[...]
    g = _cumsum_rows(tril, jnp.log(f))
    ke = ((1.0 - f) * jnp.exp(g[n - 1:n, :] - g)).astype(BF16)
    for h in range(N_HEADS):
        sl = slice(h * HEAD_DIM, (h + 1) * HEAD_DIM)
        sf_ref[h] = _dot_tn(v[:, sl], ke[:, sl])
    f = fb_ref[...]
    lf = jnp.log(f)
    ke = ((1.0 - f) * jnp.exp(_cumsum_rows(tril, lf) - lf)).astype(BF16)
    for h in range(N_HEADS):
        sl = slice(h * HEAD_DIM, (h + 1) * HEAD_DIM)
        sb_ref[h] = _dot_tn(v[:, sl], ke[:, sl])

    pad = SUBLANES
    zp_ref[0:pad, :] = jnp.zeros((pad, D_MODEL), F32)
    zp_ref[pad + n:2 * pad + n, :] = jnp.zeros((pad, D_MODEL), F32)
    zp_ref[pad:pad + n, :] = z5_ref[...]
    first = pad - (N_CONV - 1) // 2
    xc = cb_ref[...] + zp_ref[first:first + n, :] * cw_ref[0:1, :]
    for kk in range(1, N_CONV):
        xc = xc + zp_ref[first + kk:first + kk + n, :] * cw_ref[kk:kk + 1, :]

    for d, h_ref in ((0, hf_ref), (1, hb_ref)):
        sp = _softplus(-lam_ref[d:d + 1, :])
        for blk in range(N_BLOCKS):
            sl = slice(blk * BLOCK_DIM, (blk + 1) * BLOCK_DIM)
            g = _dot(xc[:, sl].astype(BF16), wg_ref[blk, :, 2 * d * BLOCK_DIM:2 * (d + 1) * BLOCK_DIM])
            a, b = _lru_ab(0.5 * xc[:, sl], g[:, :BLOCK_DIM], g[:, BLOCK_DIM:],
                           br_ref[d:d + 1, sl], bi_ref[d:d + 1, sl], sp[:, sl])
            a_ref[:, sl] = a
            b_ref[:, sl] = b

        def step(t, h, d=d):
            tt = (n - 1 - t) if d == 1 else t
            return a_ref[pl.ds(tt, 1), :] * h + b_ref[pl.ds(tt, 1), :]

        h_ref[...] = lax.fori_loop(0, n, step, jnp.zeros((1, D_MODEL), F32), unroll=32)


def _ctx_call(feat_c, conv_w, conv_b, wg, br, bi, lam, *, batch, n_ctx):
    feat = lambda j: pl.BlockSpec((None, None, n_ctx, D_MODEL), lambda b, j=j: (b, j, 0, 0))
    full = lambda shape: pl.BlockSpec(shape, lambda b: (0,) * len(shape))
    state = pl.BlockSpec((None, N_HEADS, HEAD_DIM, HEAD_DIM), lambda b: (b, 0, 0, 0))
    hvec = pl.BlockSpec((None, 1, D_MODEL), lambda b: (b, 0, 0))
    return pl.pallas_call(
        _ctx_kernel,
        grid=(batch,),
        in_specs=[feat(0), feat(1), feat(2), feat(3),
                  full((N_CONV, D_MODEL)), full((1, D_MODEL)),
                  full((N_BLOCKS, BLOCK_DIM, 4 * BLOCK_DIM)),
                  full((2, D_MODEL)), full((2, D_MODEL)), full((2, D_MODEL))],
        out_specs=(state, state, hvec, hvec),
        out_shape=(jax.ShapeDtypeStruct((batch, N_HEADS, HEAD_DIM, HEAD_DIM), F32),
                   jax.ShapeDtypeStruct((batch, N_HEADS, HEAD_DIM, HEAD_DIM), F32),
                   jax.ShapeDtypeStruct((batch, 1, D_MODEL), F32),
                   jax.ShapeDtypeStruct((batch, 1, D_MODEL), F32)),
        scratch_shapes=[pltpu.VMEM((n_ctx + 2 * SUBLANES, D_MODEL), F32),
                        pltpu.VMEM((n_ctx, D_MODEL), F32),
                        pltpu.VMEM((n_ctx, D_MODEL), F32)],
        compiler_params=pltpu.CompilerParams(
            dimension_semantics=("arbitrary",), vmem_limit_bytes=VMEM_LIMIT),
        name="ctx_states",
    )(feat_c, feat_c, feat_c, feat_c, conv_w, conv_b, wg, br, bi, lam)


GLA_ROWS_PER_ITER = 16


def _gla_kernel(*refs, reverse, rows, finish):
    if finish:
        (q_ref, f_ref, v_ref, s0_ref, other_ref, o_ref,
         st_ref, sn_ref, qd_ref, kit_ref, ke_ref, dec_ref) = refs
    else:
        q_ref, f_ref, v_ref, s0_ref, o_ref, st_ref, sn_ref, qd_ref, kit_ref, ke_ref, dec_ref = refs

    @pl.when(pl.program_id(1) == 0)
    def _():
        st_ref[...] = s0_ref[...]
        for h in range(N_HEADS):
            sn_ref[h] = s0_ref[h].astype(BF16).T

    ri = lax.broadcasted_iota(jnp.int32, (GRID_W, GRID_W), 0)
    ci = lax.broadcasted_iota(jnp.int32, (GRID_W, GRID_W), 1)
    allow = (ci >= ri) if reverse else (ci <= ri)

    def row_offset(n):
        n = jnp.minimum(n, rows - 1)
        return pl.multiple_of(((rows - 1 - n) if reverse else n) * GRID_W, GRID_W)

    def prepare(n, slot):
        off = row_offset(n)
        q = q_ref[pl.ds(off, GRID_W), :]
        f = f_ref[pl.ds(off, GRID_W), :]
        dg, dec = _chunk_cumprod(f, reverse)
        ki = (1.0 - f) / dg
        qd = (q * dg).astype(BF16)
        ke = (ki * dec).astype(BF16)
        kit_ref[slot] = ki.astype(BF16).T
        dec_ref[slot] = dec
        for h in range(N_HEADS):
            sl = slice(h * HEAD_DIM, (h + 1) * HEAD_DIM)
            qd_ref[slot, h] = qd[:, sl]
            ke_ref[slot, h] = ke[:, sl]

    def contract(n, slot):
        off = row_offset(n)
        for h in range(N_HEADS):
            sl = slice(h * HEAD_DIM, (h + 1) * HEAD_DIM)
            qd = qd_ref[slot, h]
            v = v_ref[h, pl.ds(off, GRID_W), :]
            p = jnp.where(allow, _dot(qd, kit_ref[slot, sl, :]), 0.0).astype(BF16)
            o = _dot(p, v) + _dot(qd, sn_ref[h])
            st = st_ref[h] * dec_ref[slot, :, sl] + _dot_tn(v, ke_ref[slot, h])
            st_ref[h] = st
            sn_ref[h] = st.astype(BF16).T
            if finish:
                o = o + other_ref[h, pl.ds(off, GRID_W), :].astype(F32)
            o_ref[h, pl.ds(off, GRID_W), :] = o.astype(o_ref.dtype)

    prepare(0, 0)

    def body(m, carry):
        n = GLA_ROWS_PER_ITER * m
        for i in range(GLA_ROWS_PER_ITER):
            contract(n + i, i % 2)
            prepare(n + i + 1, (i + 1) % 2)
        return carry

    lax.fori_loop(0, rows // GLA_ROWS_PER_ITER, body, 0)


def _gla_call(feat, feat_h, f_col, s0, other=None, *, reverse, batch, seq, rows):
    tb = rows * GRID_W
    nrb = seq // tb
    finish = other is not None
    per_tile = feat.shape[2] // tb

    def row_block(b, i):
        return b * nrb + ((nrb - 1 - i) if reverse else i)

    def col(c):
        return pl.BlockSpec((None, None, tb, D_MODEL), lambda b, i, c=c: (
            row_block(b, i) // per_tile, c, row_block(b, i) % per_tile, 0))

    tok = pl.BlockSpec((N_HEADS, tb, HEAD_DIM), lambda b, i: (0, row_block(b, i), 0))
    v_spec = pl.BlockSpec((None, None, N_HEADS, tb, HEAD_DIM), lambda b, i: (
        row_block(b, i) // per_tile, H_V, 0, row_block(b, i) % per_tile, 0))
    in_specs = [col(F_Q), col(f_col), v_spec,
                pl.BlockSpec((None, N_HEADS, HEAD_DIM, HEAD_DIM), lambda b, i: (b, 0, 0, 0))]
    args = [feat, feat, feat_h, s0]
    if finish:
        in_specs += [tok]
        args += [other]
    return pl.pallas_call(
        functools.partial(_gla_kernel, reverse=reverse, rows=rows, finish=finish),
        grid=(batch, nrb),
        in_specs=in_specs,
        out_specs=tok,
        out_shape=jax.ShapeDtypeStruct((N_HEADS, batch * seq, HEAD_DIM), BF16),
        scratch_shapes=[pltpu.VMEM((N_HEADS, HEAD_DIM, HEAD_DIM), F32)]
        + [pltpu.VMEM((N_HEADS, HEAD_DIM, HEAD_DIM), BF16), pltpu.VMEM((2, N_HEADS, GRID_W, HEAD_DIM), BF16),
           pltpu.VMEM((2, D_MODEL, GRID_W), BF16), pltpu.VMEM((2, N_HEADS, GRID_W, HEAD_DIM), BF16),
           pltpu.VMEM((2, 1, D_MODEL), F32)],
        compiler_params=pltpu.CompilerParams(
            dimension_semantics=("arbitrary", "arbitrary"), vmem_limit_bytes=VMEM_LIMIT),
        name="gla_bwd" if reverse else "gla_fwd",
    )(*args)


def _lru_kernel(z5_ref, cw_ref, cb_ref, wg_ref, br_ref, bi_ref, lam_ref, h0f_ref, h0b_ref, o_ref,
                zp_ref, af_ref, bf_ref, ab_ref, bb_ref, cf_ref, cr_ref, *, n_rows, rows_per_step):
    w = GRID_W
    seq = n_rows * w
    tb = rows_per_step * w
    n_steps = n_rows // rows_per_step

    zp_ref[0:w, :] = jnp.zeros((w, BLOCK_DIM), F32)
    zp_ref[w + seq:, :] = jnp.zeros((2 * w, BLOCK_DIM), F32)
    tile = z5_ref.shape[1]
    for k in range(z5_ref.shape[0]):
        zp_ref[w + k * tile:w + (k + 1) * tile, :] = z5_ref[k]

    sp = [_softplus(-lam_ref[d:d + 1, :]) for d in (0, 1)]

    def gates(s, c):
        off = pl.multiple_of(s * tb, tb)
        xc = cb_ref[...] + zp_ref[pl.ds(off, tb), :] * cw_ref[0:1, :]
        for kk in range(1, N_CONV):
            xc = xc + zp_ref[pl.ds(off + kk * w, tb), :] * cw_ref[kk:kk + 1, :]
        g = _dot(xc.astype(BF16), wg_ref[...])
        hx = 0.5 * xc
        for d, a_ref, b_ref in ((0, af_ref, bf_ref), (1, ab_ref, bb_ref)):
            c0 = 2 * d * BLOCK_DIM
            a, b = _lru_ab(hx, g[:, c0:c0 + BLOCK_DIM], g[:, c0 + BLOCK_DIM:c0 + 2 * BLOCK_DIM],
                           br_ref[d:d + 1, :], bi_ref[d:d + 1, :], sp[d])
            a_ref[pl.ds(off, tb), :] = a
            b_ref[pl.ds(off, tb), :] = b
        return c

    lax.fori_loop(0, n_steps, gates, 0, unroll=True)

    def scan(n, carry):
        hf, pf, hb, pb = carry
        off_f = pl.multiple_of(n * w, w)
        off_b = pl.multiple_of((n_rows - 1 - n) * w, w)
        a = af_ref[pl.ds(off_f, w), :]
        hf = a * hf + bf_ref[pl.ds(off_f, w), :]
        pf = pf * a
        bf_ref[pl.ds(off_f, w), :] = hf
        af_ref[pl.ds(off_f, w), :] = pf
        a = ab_ref[pl.ds(off_b, w), :]
        hb = a * hb + bb_ref[pl.ds(off_b, w), :]
        pb = pb * a
        bb_ref[pl.ds(off_b, w), :] = hb
        ab_ref[pl.ds(off_b, w), :] = pb
        return hf, pf, hb, pb

    zeros = jnp.zeros((w, BLOCK_DIM), F32)
    ones = jnp.ones((w, BLOCK_DIM), F32)
    lax.fori_loop(0, n_rows, scan, (zeros, ones, zeros, ones), unroll=16)

    last = (n_rows - 1) * w

    def carry(n, c):
        cf, cb = c
        cf_ref[pl.ds(n, 1), :] = cf
        cf = bf_ref[pl.ds(last + n, 1), :] + af_ref[pl.ds(last + n, 1), :] * cf
        col = w - 1 - n
        cr_ref[pl.ds(col, 1), :] = cb
        cb = bb_ref[pl.ds(col, 1), :] + ab_ref[pl.ds(col, 1), :] * cb
        return cf, cb

    lax.fori_loop(0, w, carry, (h0f_ref[...], h0b_ref[...]), unroll=True)

    def fix(s, c):
        off = pl.multiple_of(s * tb, tb)
        cf = jnp.concatenate([cf_ref[...]] * rows_per_step, axis=0)
        cr = jnp.concatenate([cr_ref[...]] * rows_per_step, axis=0)
        h = (bf_ref[pl.ds(off, tb), :] + af_ref[pl.ds(off, tb), :] * cf
             + bb_ref[pl.ds(off, tb), :] + ab_ref[pl.ds(off, tb), :] * cr)
        o_ref[pl.ds(off, tb), :] = h.astype(o_ref.dtype)
        return c

    lax.fori_loop(0, n_steps, fix, 0, unroll=True)


def _lru_call(feat, conv_w, conv_b, wg, br, bi, lam, h0f, h0b, *, batch, seq):
    n_rows = seq // GRID_W
    vec = lambda n: pl.BlockSpec((n, BLOCK_DIM), lambda b, c: (0, c))
    h0 = pl.BlockSpec((None, 1, BLOCK_DIM), lambda b, c: (b, 0, c))
    buf = pltpu.VMEM((seq, BLOCK_DIM), F32)
    n_i, n_col, tm, _ = feat.shape
    feat5 = feat.reshape(batch, n_i // batch, n_col, tm, D_MODEL)
    return pl.pallas_call(
        functools.partial(_lru_kernel, n_rows=n_rows, rows_per_step=LRU_ROWS_PER_STEP),
        grid=(batch, N_BLOCKS),
        in_specs=[pl.BlockSpec((None, n_i // batch, None, tm, BLOCK_DIM), lambda b, c: (b, 0, F_Z5, 0, c)),
                  vec(N_CONV), vec(1),
                  pl.BlockSpec((None, BLOCK_DIM, 4 * BLOCK_DIM), lambda b, c: (c, 0, 0)),
                  vec(2), vec(2), vec(2), h0, h0],
        out_specs=pl.BlockSpec((None, seq, BLOCK_DIM), lambda b, c: (c, b, 0)),
        out_shape=jax.ShapeDtypeStruct((N_BLOCKS, batch * seq, BLOCK_DIM), BF16),
        scratch_shapes=[pltpu.VMEM((seq + 3 * GRID_W, BLOCK_DIM), F32), buf, buf, buf, buf,
                        pltpu.VMEM((GRID_W, BLOCK_DIM), F32), pltpu.VMEM((GRID_W, BLOCK_DIM), F32)],
        compiler_params=pltpu.CompilerParams(
            dimension_semantics=("arbitrary", "arbitrary"), vmem_limit_bytes=VMEM_LIMIT),
        name="lru",
    )(feat5, conv_w, conv_b, wg, br, bi, lam, h0f, h0b)


def _merge_kernel(oa_ref, hx_ref, g4_ref, g6_ref, m7_ref, m8_ref, x_ref, mod_ref, ng_ref,
                  pa_ref, pb_ref, wo_ref, lg_ref, lbias_ref, o_ref):
    f32 = lambda ref: jnp.concatenate([ref[h] for h in range(N_HEADS)], axis=-1).astype(F32)
    o_b = (f32(hx_ref) * f32(g6_ref)).astype(BF16)
    y = f32(m8_ref) * _dot(o_b, pb_ref[...])
    y_a = None
    for pair in range(N_HEADS // 2):
        o_h = []
        for h in (2 * pair, 2 * pair + 1):
            t = oa_ref[h].astype(F32)
            ms = jnp.mean(t * t, axis=-1, keepdims=True)
            o_h.append((t * lax.rsqrt(ms + RMS_EPS) * ng_ref[...] * g4_ref[h].astype(F32)).astype(BF16))
        rows = slice(2 * pair * HEAD_DIM, 2 * (pair + 1) * HEAD_DIM)
        part = _dot(jnp.concatenate(o_h, axis=-1), pa_ref[rows, :])
        y_a = part if y_a is None else y_a + part
    y = y + f32(m7_ref) * y_a
    y = _dot(y.astype(BF16), wo_ref[...])
    t = DEEPNORM_ALPHA * x_ref[...] + mod_ref[2:3, :] * y
    mu = jnp.mean(t, axis=-1, keepdims=True)
    tc = t - mu
    var = jnp.mean(tc * tc, axis=-1, keepdims=True)
    o_ref[...] = tc * lax.rsqrt(var + LN_EPS) * lg_ref[...] + lbias_ref[...]


def _merge_call(oa, hx, feat_h, x2, mod3, ng, pa, pb, wo, lg, lbias, *, tm, tiles_per_batch):
    n_tok = x2.shape[0]
    tok = pl.BlockSpec((tm, D_MODEL), lambda i: (i, 0))
    heads = pl.BlockSpec((N_HEADS, tm, HEAD_DIM), lambda i: (0, i, 0))
    per_tile = feat_h.shape[3] // tm
    col = lambda j: pl.BlockSpec((None, None, N_HEADS, tm, HEAD_DIM),
                                 lambda i, j=j: (i // per_tile, j, 0, i % per_tile, 0))
    full = lambda shape: pl.BlockSpec(shape, lambda i: (0,) * len(shape), pipeline_mode=pl.Buffered(1))
    return pl.pallas_call(
        _merge_kernel,
        grid=(n_tok // tm,),
        in_specs=[heads, heads, col(H_G4), col(H_G6), col(H_M7), col(H_M8), tok,
                  pl.BlockSpec((None, 3, D_MODEL), lambda i: (i // tiles_per_batch, 0, 0)),
                  full((1, HEAD_DIM)), full((D_MODEL, D_MODEL)), full((D_MODEL, D_MODEL)),
                  full((D_MODEL, D_MODEL)), full((1, D_MODEL)), full((1, D_MODEL))],
        out_specs=tok,
        out_shape=jax.ShapeDtypeStruct((n_tok, D_MODEL), F32),
        compiler_params=pltpu.CompilerParams(
            dimension_semantics=("arbitrary",), vmem_limit_bytes=VMEM_LIMIT),
        name="merge",
    )(oa, hx, feat_h, feat_h, feat_h, feat_h, x2, mod3, ng, pa, pb, wo, lg, lbias)


def kernel(x, c, ctx, c_ctx, w_mod, b_mod, w_in, b_in, lb_logits, norm_a_g, conv_w, conv_b,
           w_r, b_r, w_i, b_i, lam, p_a, p_b, w_out, ln_g, ln_b):
    batch, seq, d = x.shape
    n_ctx = ctx.shape[1]
    assert d == D_MODEL and seq % GRID_W == 0 and w_in.shape[0] == 1

    n_blocks = w_in.shape[-1] // D_MODEL
    col_scale = jnp.repeat(jnp.array([0.5 if j in HALVED_BLOCKS else 1.0 for j in range(n_blocks)], F32), D_MODEL)
    w_bf = (w_in[0] * col_scale).astype(BF16)
    b2 = (b_in[0] * col_scale)[None, :]
    wg = (0.5 * jnp.concatenate([w_r[0, 0], w_i[0, 0], w_r[0, 1], w_i[0, 1]], axis=-1)).astype(BF16)
    assert batch < SUBLANES
    mod, lb = _mod_call(c, c_ctx[None, :], w_mod[0], b_mod[0][None, :], lb_logits)
    mod3 = mod.reshape(SUBLANES, 3, D_MODEL)

    x2 = x.reshape(batch * seq, D_MODEL)
    tm = INPROJ_TM
    feat = _inproj_call(x2, mod3, w_bf, b2, LATENT_F32, ("q", "forget0", "forget1", None), lb,
                        out_dtype=F32, tm=tm, tiles_per_mod=seq // tm)
    feat_h = _inproj_call(x2, mod3, w_bf, b2, LATENT_BF16, ("silu", "silu", "sig", "sig", None),
                          out_dtype=BF16, tm=tm, tiles_per_mod=seq // tm, head_major=True)
    feat_c = _inproj_call(ctx.reshape(batch * n_ctx, D_MODEL), mod3[batch:batch + 1], w_bf, b2, CTX_F32,
                          ("forget0", "forget1", None, None), lb,
                          out_dtype=F32, tm=n_ctx, tiles_per_mod=batch)

    cw, cb = conv_w[0], conv_b[0][None, :]
    s0f, s0b, h0f, h0b = _ctx_call(feat_c, cw, cb, wg, b_r[0], b_i[0], lam[0], batch=batch, n_ctx=n_ctx)

    o_b = _gla_call(feat, feat_h, F_F1, s0b, reverse=True, batch=batch, seq=seq, rows=GLA_ROWS)
    oa = _gla_call(feat, feat_h, F_F0, s0f, o_b, reverse=False, batch=batch, seq=seq, rows=GLA_ROWS)
    hx = _lru_call(feat, cw, cb, wg, b_r[0], b_i[0], lam[0], h0f, h0b, batch=batch, seq=seq)

    out = _merge_call(oa, hx, feat_h, x2, mod3, norm_a_g[0][None, :],
                      p_a[0].astype(BF16), p_b[0].astype(BF16), w_out[0].astype(BF16),
                      ln_g[0][None, :], ln_b[0][None, :], tm=MERGE_TM, tiles_per_batch=seq // MERGE_TM)
    return out.reshape(batch, seq, D_MODEL)
```

```python
import functools

import jax
import jax.numpy as jnp
from jax import lax
from jax.experimental import pallas as pl
from jax.experimental.pallas import tpu as pltpu

F32 = jnp.float32
BF16 = jnp.bfloat16

D_MODEL = 1024
GRID_W = 64
HEAD_DIM = 128
N_HEADS = D_MODEL // HEAD_DIM
N_BLOCKS = 8
BLOCK_DIM = D_MODEL // N_BLOCKS
N_CONV = 4
RG_C = 8.0
LN_EPS = 1e-5
RMS_EPS = 1e-6
DEEPNORM_ALPHA = 2.0 ** 0.25
Q_SCALE = HEAD_DIM ** -0.5

V7X_VMEM_BYTES = 64 * 1024 * 1024
VMEM_LIMIT = V7X_VMEM_BYTES - 8 * 1024 * 1024
SUBLANES = 8

INPROJ_TM = 1024
GLA_ROWS = 16
LRU_ROWS_PER_STEP = 16
MERGE_TM = 1024

LATENT_F32 = (0, 1, 2, 5)
LATENT_BF16 = (4, 6, 7, 8, 3)
CTX_F32 = (1, 2, 3, 5)
F_Q, F_F0, F_F1, F_Z5 = 0, 1, 2, 3
H_G4, H_G6, H_M7, H_M8, H_V = 0, 1, 2, 3, 4
HALVED_BLOCKS = (0, 1, 2, 4, 6, 7, 8)


def _silu(x):
    return _half_silu(0.5 * x)


def _half_silu(h):
    return h * jnp.tanh(h) + h


def _half_sigmoid(h):
    return 0.5 * jnp.tanh(h) + 0.5


def _half_forget(h, lb):
    return (0.5 + 0.5 * lb) + (0.5 - 0.5 * lb) * jnp.tanh(h)


def _dot(a, b):
    return jnp.dot(a, b, preferred_element_type=F32)


def _dot_tn(a, b):
    return lax.dot_general(a, b, (((0,), (0,)), ((), ())), preferred_element_type=F32)


def _chunk_cumprod(x, reverse):
    n, c = x.shape
    nb = n // 8
    y = x.reshape(nb, 8, c)
    sub = lax.broadcasted_iota(jnp.int32, (nb, 8, c), 1)
    for s in (1, 2, 4):
        if reverse:
            y = y * jnp.where(sub < 8 - s, pltpu.roll(y, 8 - s, axis=1), 1.0)
        else:
            y = y * jnp.where(sub >= s, pltpu.roll(y, s, axis=1), 1.0)
    offs = [None] * nb
    acc = jnp.ones((1, c), F32)
    for b in (reversed(range(nb)) if reverse else range(nb)):
        offs[b] = acc
        acc = acc * (y[b, 0:1, :] if reverse else y[b, 7:8, :])
    return (y * jnp.stack(offs)).reshape(n, c), acc


def _cumsum_rows(mask_bf16, x):
    hi = x.astype(BF16)
    r1 = x - hi.astype(F32)
    mid = r1.astype(BF16)
    lo = (r1 - mid.astype(F32)).astype(BF16)
    return _dot(mask_bf16, hi) + _dot(mask_bf16, mid) + _dot(mask_bf16, lo)


def _mod_kernel(c_ref, cc_ref, w_ref, b_ref, lbl_ref, mod_ref, lb_ref, rows_ref):
    n = c_ref.shape[0]
    rows_ref[...] = jnp.zeros(rows_ref.shape, F32)
    rows_ref[0:n, :] = c_ref[...]
    rows_ref[n:n + 1, :] = cc_ref[...]
    mod_ref[...] = _dot(_silu(rows_ref[...]), w_ref[...]) + b_ref[...]
    l = lbl_ref[...]
    e = jnp.exp(l - jnp.max(l, axis=0, keepdims=True))
    lb_ref[...] = e[0] / jnp.sum(e, axis=0)


def _mod_call(c, c_ctx, w_mod, b_mod, lb_logits):
    return pl.pallas_call(
        _mod_kernel,
        out_shape=(jax.ShapeDtypeStruct((SUBLANES, 3 * D_MODEL), F32),
                   jax.ShapeDtypeStruct((2, D_MODEL), F32)),
        scratch_shapes=[pltpu.VMEM((SUBLANES, D_MODEL), F32)],
        compiler_params=pltpu.CompilerParams(vmem_limit_bytes=VMEM_LIMIT),
        name="mod",
    )(c, c_ctx, w_mod, b_mod, lb_logits)


def _inproj_kernel(x_ref, mod_ref, *refs, head_major, acts):
    n_col = len(acts)
    if any(a and a.startswith("forget") for a in acts):
        lb_ref, refs = refs[0], refs[1:]
    w_refs, b_refs, o_ref = refs[:n_col], refs[n_col:2 * n_col], refs[-1]
    u = (x_ref[...] * (1.0 + mod_ref[1:2, :]) + mod_ref[0:1, :]).astype(BF16)
    for j in range(n_col):
        z = _dot(u, w_refs[j][...]) + b_refs[j][...]
        if acts[j] == "q":
            z = _half_silu(z) * Q_SCALE
        elif acts[j] == "silu":
            z = _half_silu(z)
        elif acts[j] == "sig":
            z = _half_sigmoid(z)
        elif acts[j] is not None:
            d = int(acts[j][-1])
            z = _half_forget(z, lb_ref[d:d + 1, :])
        z = z.astype(o_ref.dtype)
        if head_major:
            for h in range(N_HEADS):
                o_ref[j, h] = z[:, h * HEAD_DIM:(h + 1) * HEAD_DIM]
        else:
            o_ref[j] = z


def _inproj_call(x2, mod3, w, b, cols, acts, lb=None, *, out_dtype, tm, tiles_per_mod, head_major=False):
    n_tok = x2.shape[0]
    n_col = len(cols)
    tile = (N_HEADS, tm, HEAD_DIM) if head_major else (tm, D_MODEL)
    col_block = lambda rows: [pl.BlockSpec((rows, D_MODEL), lambda i, c=c: (0, c), pipeline_mode=pl.Buffered(1))
                              for c in cols]
    return pl.pallas_call(
        functools.partial(_inproj_kernel, head_major=head_major, acts=acts),
        grid=(n_tok // tm,),
        in_specs=[
            pl.BlockSpec((tm, D_MODEL), lambda i: (i, 0)),
            pl.BlockSpec((None, 3, D_MODEL), lambda i: (i // tiles_per_mod, 0, 0)),
            *([] if lb is None else [pl.BlockSpec((2, D_MODEL), lambda i: (0, 0))]),
            *col_block(D_MODEL),
            *col_block(1),
        ],
        out_specs=pl.BlockSpec((None, n_col) + tile, lambda i: (i, 0) + (0,) * len(tile)),
        out_shape=jax.ShapeDtypeStruct((n_tok // tm, n_col) + tile, out_dtype),
        compiler_params=pltpu.CompilerParams(
            dimension_semantics=("arbitrary",), vmem_limit_bytes=VMEM_LIMIT),
        name="inproj",
    )(x2, mod3, *([] if lb is None else [lb]), *([w] * n_col), *([b] * n_col))


LOG2_E = 1.4426950408889634


def _lru_ab(hx, h_r, h_i, br, bi, sp):
    k = (-0.5 * RG_C * LOG2_E) * sp
    a = jnp.exp2(k * jnp.tanh(h_r + 0.5 * br) + k)
    y = 1.0 - a * a
    mult = jnp.where(y > 0.0, y * lax.rsqrt(y), 0.0)
    return a, mult * (hx * jnp.tanh(h_i + 0.5 * bi) + hx)


def _softplus(y):
    return jnp.maximum(y, 0.0) + jnp.log(1.0 + jnp.exp(-jnp.abs(y)))


def _ctx_kernel(ff_ref, fb_ref, v_ref, z5_ref, cw_ref, cb_ref, wg_ref, br_ref, bi_ref, lam_ref,
                sf_ref, sb_ref, hf_ref, hb_ref, zp_ref, a_ref, b_ref):
    n = ff_ref.shape[0]
    ri = lax.broadcasted_iota(jnp.int32, (n, n), 0)
    ci = lax.broadcasted_iota(jnp.int32, (n, n), 1)
    tril = (ci <= ri).astype(F32).astype(BF16)
    v = v_ref[...].astype(BF16)

    f = ff_ref[...]
    g = _cumsum_rows(tril, jnp.log(f))
    ke = ((1.0 - f) * jnp.exp(g[n - 1:n, :] - g)).astype(BF16)
    for h in range(N_HEADS):
        sl = slice(h * HEAD_DIM, (h + 1) * HEAD_DIM)
        sf_ref[h] = _dot_tn(v[:, sl], ke[:, sl])
    f = fb_ref[...]
    lf = jnp.log(f)
    ke = ((1.0 - f) * jnp.exp(_cumsum_rows(tril, lf) - lf)).astype(BF16)
    for h in range(N_HEADS):
        sl = slice(h * HEAD_DIM, (h + 1) * HEAD_DIM)
        sb_ref[h] = _dot_tn(v[:, sl], ke[:, sl])

    pad = SUBLANES
    zp_ref[0:pad, :] = jnp.zeros((pad, D_MODEL), F32)
    zp_ref[pad + n:2 * pad + n, :] = jnp.zeros((pad, D_MODEL), F32)
    zp_ref[pad:pad + n, :] = z5_ref[...]
    first = pad - (N_CONV - 1) // 2
    xc = cb_ref[...] + zp_ref[first:first + n, :] * cw_ref[0:1, :]
    for kk in range(1, N_CONV):
        xc = xc + zp_ref[first + kk:first + kk + n, :] * cw_ref[kk:kk + 1, :]

    for d, h_ref in ((0, hf_ref), (1, hb_ref)):
        sp = _softplus(-lam_ref[d:d + 1, :])
        for blk in range(N_BLOCKS):
            sl = slice(blk * BLOCK_DIM, (blk + 1) * BLOCK_DIM)
            g = _dot(xc[:, sl].astype(BF16), wg_ref[blk, :, 2 * d * BLOCK_DIM:2 * (d + 1) * BLOCK_DIM])
            a, b = _lru_ab(0.5 * xc[:, sl], g[:, :BLOCK_DIM], g[:, BLOCK_DIM:],
                           br_ref[d:d + 1, sl], bi_ref[d:d + 1, sl], sp[:, sl])
            a_ref[:, sl] = a
            b_ref[:, sl] = b

        def step(t, h, d=d):
            tt = (n - 1 - t) if d == 1 else t
            return a_ref[pl.ds(tt, 1), :] * h + b_ref[pl.ds(tt, 1), :]

        h_ref[...] = lax.fori_loop(0, n, step, jnp.zeros((1, D_MODEL), F32), unroll=32)


def _ctx_call(feat_c, conv_w, conv_b, wg, br, bi, lam, *, batch, n_ctx):
    feat = lambda j: pl.BlockSpec((None, None, n_ctx, D_MODEL), lambda b, j=j: (b, j, 0, 0))
    full = lambda shape: pl.BlockSpec(shape, lambda b: (0,) * len(shape))
    state = pl.BlockSpec((None, N_HEADS, HEAD_DIM, HEAD_DIM), lambda b: (b, 0, 0, 0))
    hvec = pl.BlockSpec((None, 1, D_MODEL), lambda b: (b, 0, 0))
    return pl.pallas_call(
        _ctx_kernel,
        grid=(batch,),
        in_specs=[feat(0), feat(1), feat(2), feat(3),
                  full((N_CONV, D_MODEL)), full((1, D_MODEL)),
                  full((N_BLOCKS, BLOCK_DIM, 4 * BLOCK_DIM)),
                  full((2, D_MODEL)), full((2, D_MODEL)), full((2, D_MODEL))],
        out_specs=(state, state, hvec, hvec),
        out_shape=(jax.ShapeDtypeStruct((batch, N_HEADS, HEAD_DIM, HEAD_DIM), F32),
                   jax.ShapeDtypeStruct((batch, N_HEADS, HEAD_DIM, HEAD_DIM), F32),
                   jax.ShapeDtypeStruct((batch, 1, D_MODEL), F32),
                   jax.ShapeDtypeStruct((batch, 1, D_MODEL), F32)),
        scratch_shapes=[pltpu.VMEM((n_ctx + 2 * SUBLANES, D_MODEL), F32),
                        pltpu.VMEM((n_ctx, D_MODEL), F32),
                        pltpu.VMEM((n_ctx, D_MODEL), F32)],
        compiler_params=pltpu.CompilerParams(
            dimension_semantics=("arbitrary",), vmem_limit_bytes=VMEM_LIMIT),
        name="ctx_states",
    )(feat_c, feat_c, feat_c, feat_c, conv_w, conv_b, wg, br, bi, lam)


GLA_ROWS_PER_ITER = 16


def _gla_kernel(*refs, reverse, rows, finish):
    if finish:
        (q_ref, f_ref, v_ref, s0_ref, other_ref, o_ref,
         st_ref, sn_ref, qd_ref, kit_ref, ke_ref, dec_ref) = refs
    else:
        q_ref, f_ref, v_ref, s0_ref, o_ref, st_ref, sn_ref, qd_ref, kit_ref, ke_ref, dec_ref = refs

    @pl.when(pl.program_id(1) == 0)
    def _():
        st_ref[...] = s0_ref[...]
        for h in range(N_HEADS):
            sn_ref[h] = s0_ref[h].astype(BF16).T

    ri = lax.broadcasted_iota(jnp.int32, (GRID_W, GRID_W), 0)
    ci = lax.broadcasted_iota(jnp.int32, (GRID_W, GRID_W), 1)
    allow = (ci >= ri) if reverse else (ci <= ri)

    def row_offset(n):
        n = jnp.minimum(n, rows - 1)
        return pl.multiple_of(((rows - 1 - n) if reverse else n) * GRID_W, GRID_W)

    def prepare(n, slot):
        off = row_offset(n)
        q = q_ref[pl.ds(off, GRID_W), :]
        f = f_ref[pl.ds(off, GRID_W), :]
        dg, dec = _chunk_cumprod(f, reverse)
        ki = (1.0 - f) / dg
        qd = (q * dg).astype(BF16)
        ke = (ki * dec).astype(BF16)
        kit_ref[slot] = ki.astype(BF16).T
        dec_ref[slot] = dec
        for h in range(N_HEADS):
            sl = slice(h * HEAD_DIM, (h + 1) * HEAD_DIM)
            qd_ref[slot, h] = qd[:, sl]
            ke_ref[slot, h] = ke[:, sl]

    def contract(n, slot):
        off = row_offset(n)
        for h in range(N_HEADS):
            sl = slice(h * HEAD_DIM, (h + 1) * HEAD_DIM)
            qd = qd_ref[slot, h]
            v = v_ref[h, pl.ds(off, GRID_W), :]
            p = jnp.where(allow, _dot(qd, kit_ref[slot, sl, :]), 0.0).astype(BF16)
            o = _dot(p, v) + _dot(qd, sn_ref[h])
            st = st_ref[h] * dec_ref[slot, :, sl] + _dot_tn(v, ke_ref[slot, h])
            st_ref[h] = st
            sn_ref[h] = st.astype(BF16).T
            if finish:
                o = o + other_ref[h, pl.ds(off, GRID_W), :].astype(F32)
            o_ref[h, pl.ds(off, GRID_W), :] = o.astype(o_ref.dtype)

    prepare(0, 0)

    def body(m, carry):
        n = GLA_ROWS_PER_ITER * m
        for i in range(GLA_ROWS_PER_ITER):
            contract(n + i, i % 2)
            prepare(n + i + 1, (i + 1) % 2)
        return carry

    lax.fori_loop(0, rows // GLA_ROWS_PER_ITER, body, 0)


def _gla_call(feat, feat_h, f_col, s0, other=None, *, reverse, batch, seq, rows):
    tb = rows * GRID_W
    nrb = seq // tb
    finish = other is not None
    per_tile = feat.shape[2] // tb

    def row_block(b, i):
        return b * nrb + ((nrb - 1 - i) if reverse else i)

    def col(c):
        return pl.BlockSpec((None, None, tb, D_MODEL), lambda b, i, c=c: (
            row_block(b, i) // per_tile, c, row_block(b, i) % per_tile, 0))

    tok = pl.BlockSpec((N_HEADS, tb, HEAD_DIM), lambda b, i: (0, row_block(b, i), 0))
    v_spec = pl.BlockSpec((None, None, N_HEADS, tb, HEAD_DIM), lambda b, i: (
        row_block(b, i) // per_tile, H_V, 0, row_block(b, i) % per_tile, 0))
    in_specs = [col(F_Q), col(f_col), v_spec,
                pl.BlockSpec((None, N_HEADS, HEAD_DIM, HEAD_DIM), lambda b, i: (b, 0, 0, 0))]
    args = [feat, feat, feat_h, s0]
    if finish:
        in_specs += [tok]
        args += [other]
    return pl.pallas_call(
        functools.partial(_gla_kernel, reverse=reverse, rows=rows, finish=finish),
        grid=(batch, nrb),
        in_specs=in_specs,
        out_specs=tok,
        out_shape=jax.ShapeDtypeStruct((N_HEADS, batch * seq, HEAD_DIM), BF16),
        scratch_shapes=[pltpu.VMEM((N_HEADS, HEAD_DIM, HEAD_DIM), F32)]
        + [pltpu.VMEM((N_HEADS, HEAD_DIM, HEAD_DIM), BF16), pltpu.VMEM((2, N_HEADS, GRID_W, HEAD_DIM), BF16),
           pltpu.VMEM((2, D_MODEL, GRID_W), BF16), pltpu.VMEM((2, N_HEADS, GRID_W, HEAD_DIM), BF16),
           pltpu.VMEM((2, 1, D_MODEL), F32)],
        compiler_params=pltpu.CompilerParams(
            dimension_semantics=("arbitrary", "arbitrary"), vmem_limit_bytes=VMEM_LIMIT),
        name="gla_bwd" if reverse else "gla_fwd",
    )(*args)


def _lru_kernel(z5_ref, cw_ref, cb_ref, wg_ref, br_ref, bi_ref, lam_ref, h0f_ref, h0b_ref, o_ref,
                zp_ref, af_ref, bf_ref, ab_ref, bb_ref, cf_ref, cr_ref, *, n_rows, rows_per_step):
    w = GRID_W
    seq = n_rows * w
    tb = rows_per_step * w
    n_steps = n_rows // rows_per_step

    zp_ref[0:w, :] = jnp.zeros((w, BLOCK_DIM), F32)
    zp_ref[w + seq:, :] = jnp.zeros((2 * w, BLOCK_DIM), F32)
    tile = z5_ref.shape[1]
    for k in range(z5_ref.shape[0]):
        zp_ref[w + k * tile:w + (k + 1) * tile, :] = z5_ref[k]

    sp = [_softplus(-lam_ref[d:d + 1, :]) for d in (0, 1)]

    def gates(s, c):
        off = pl.multiple_of(s * tb, tb)
        xc = cb_ref[...] + zp_ref[pl.ds(off, tb), :] * cw_ref[0:1, :]
        for kk in range(1, N_CONV):
            xc = xc + zp_ref[pl.ds(off + kk * w, tb), :] * cw_ref[kk:kk + 1, :]
        g = _dot(xc.astype(BF16), wg_ref[...])
        hx = 0.5 * xc
        for d, a_ref, b_ref in ((0, af_ref, bf_ref), (1, ab_ref, bb_ref)):
            c0 = 2 * d * BLOCK_DIM
            a, b = _lru_ab(hx, g[:, c0:c0 + BLOCK_DIM], g[:, c0 + BLOCK_DIM:c0 + 2 * BLOCK_DIM],
                           br_ref[d:d + 1, :], bi_ref[d:d + 1, :], sp[d])
            a_ref[pl.ds(off, tb), :] = a
            b_ref[pl.ds(off, tb), :] = b
        return c

    lax.fori_loop(0, n_steps, gates, 0, unroll=True)

    def scan(n, carry):
        hf, pf, hb, pb = carry
        off_f = pl.multiple_of(n * w, w)
        off_b = pl.multiple_of((n_rows - 1 - n) * w, w)
        a = af_ref[pl.ds(off_f, w), :]
        hf = a * hf + bf_ref[pl.ds(off_f, w), :]
        pf = pf * a
        bf_ref[pl.ds(off_f, w), :] = hf
        af_ref[pl.ds(off_f, w), :] = pf
        a = ab_ref[pl.ds(off_b, w), :]
        hb = a * hb + bb_ref[pl.ds(off_b, w), :]
        pb = pb * a
        bb_ref[pl.ds(off_b, w), :] = hb
        ab_ref[pl.ds(off_b, w), :] = pb
        return hf, pf, hb, pb

    zeros = jnp.zeros((w, BLOCK_DIM), F32)
    ones = jnp.ones((w, BLOCK_DIM), F32)
    lax.fori_loop(0, n_rows, scan, (zeros, ones, zeros, ones), unroll=16)

    last = (n_rows - 1) * w

    def carry(n, c):
        cf, cb = c
        cf_ref[pl.ds(n, 1), :] = cf
        cf = bf_ref[pl.ds(last + n, 1), :] + af_ref[pl.ds(last + n, 1), :] * cf
        col = w - 1 - n
        cr_ref[pl.ds(col, 1), :] = cb
        cb = bb_ref[pl.ds(col, 1), :] + ab_ref[pl.ds(col, 1), :] * cb
        return cf, cb

    lax.fori_loop(0, w, carry, (h0f_ref[...], h0b_ref[...]), unroll=True)

    def fix(s, c):
        off = pl.multiple_of(s * tb, tb)
        cf = jnp.concatenate([cf_ref[...]] * rows_per_step, axis=0)
        cr = jnp.concatenate([cr_ref[...]] * rows_per_step, axis=0)
        h = (bf_ref[pl.ds(off, tb), :] + af_ref[pl.ds(off, tb), :] * cf
             + bb_ref[pl.ds(off, tb), :] + ab_ref[pl.ds(off, tb), :] * cr)
        o_ref[pl.ds(off, tb), :] = h.astype(o_ref.dtype)
        return c

    lax.fori_loop(0, n_steps, fix, 0, unroll=True)


def _lru_call(feat, conv_w, conv_b, wg, br, bi, lam, h0f, h0b, *, batch, seq):
    n_rows = seq // GRID_W
    vec = lambda n: pl.BlockSpec((n, BLOCK_DIM), lambda b, c: (0, c))
    h0 = pl.BlockSpec((None, 1, BLOCK_DIM), lambda b, c: (b, 0, c))
    buf = pltpu.VMEM((seq, BLOCK_DIM), F32)
    n_i, n_col, tm, _ = feat.shape
    feat5 = feat.reshape(batch, n_i // batch, n_col, tm, D_MODEL)
    return pl.pallas_call(
        functools.partial(_lru_kernel, n_rows=n_rows, rows_per_step=LRU_ROWS_PER_STEP),
        grid=(batch, N_BLOCKS),
        in_specs=[pl.BlockSpec((None, n_i // batch, None, tm, BLOCK_DIM), lambda b, c: (b, 0, F_Z5, 0, c)),
                  vec(N_CONV), vec(1),
                  pl.BlockSpec((None, BLOCK_DIM, 4 * BLOCK_DIM), lambda b, c: (c, 0, 0)),
                  vec(2), vec(2), vec(2), h0, h0],
        out_specs=pl.BlockSpec((None, seq, BLOCK_DIM), lambda b, c: (c, b, 0)),
        out_shape=jax.ShapeDtypeStruct((N_BLOCKS, batch * seq, BLOCK_DIM), BF16),
        scratch_shapes=[pltpu.VMEM((seq + 3 * GRID_W, BLOCK_DIM), F32), buf, buf, buf, buf,
                        pltpu.VMEM((GRID_W, BLOCK_DIM), F32), pltpu.VMEM((GRID_W, BLOCK_DIM), F32)],
        compiler_params=pltpu.CompilerParams(
            dimension_semantics=("arbitrary", "arbitrary"), vmem_limit_bytes=VMEM_LIMIT),
        name="lru",
    )(feat5, conv_w, conv_b, wg, br, bi, lam, h0f, h0b)


def _merge_kernel(oa_ref, hx_ref, g4_ref, g6_ref, m7_ref, m8_ref, x_ref, mod_ref, ng_ref,
                  pa_ref, pb_ref, wo_ref, lg_ref, lbias_ref, o_ref):
    f32 = lambda ref: jnp.concatenate([ref[h] for h in range(N_HEADS)], axis=-1).astype(F32)
    o_b = (f32(hx_ref) * f32(g6_ref)).astype(BF16)
    y = f32(m8_ref) * _dot(o_b, pb_ref[...])
    y_a = None
    for pair in range(N_HEADS // 2):
        o_h = []
        for h in (2 * pair, 2 * pair + 1):
            t = oa_ref[h].astype(F32)
            ms = jnp.mean(t * t, axis=-1, keepdims=True)
            o_h.append((t * lax.rsqrt(ms + RMS_EPS) * ng_ref[...] * g4_ref[h].astype(F32)).astype(BF16))
        rows = slice(2 * pair * HEAD_DIM, 2 * (pair + 1) * HEAD_DIM)
        part = _dot(jnp.concatenate(o_h, axis=-1), pa_ref[rows, :])
        y_a = part if y_a is None else y_a + part
    y = y + f32(m7_ref) * y_a
    y = _dot(y.astype(BF16), wo_ref[...])
    t = DEEPNORM_ALPHA * x_ref[...] + mod_ref[2:3, :] * y
    mu = jnp.mean(t, axis=-1, keepdims=True)
    tc = t - mu
    var = jnp.mean(tc * tc, axis=-1, keepdims=True)
    o_ref[...] = tc * lax.rsqrt(var + LN_EPS) * lg_ref[...] + lbias_ref[...]


def _merge_call(oa, hx, feat_h, x2, mod3, ng, pa, pb, wo, lg, lbias, *, tm, tiles_per_batch):
    n_tok = x2.shape[0]
    tok = pl.BlockSpec((tm, D_MODEL), lambda i: (i, 0))
    heads = pl.BlockSpec((N_HEADS, tm, HEAD_DIM), lambda i: (0, i, 0))
    per_tile = feat_h.shape[3] // tm
    col = lambda j: pl.BlockSpec((None, None, N_HEADS, tm, HEAD_DIM),
                                 lambda i, j=j: (i // per_tile, j, 0, i % per_tile, 0))
    full = lambda shape: pl.BlockSpec(shape, lambda i: (0,) * len(shape), pipeline_mode=pl.Buffered(1))
    return pl.pallas_call(
        _merge_kernel,
        grid=(n_tok // tm,),
        in_specs=[heads, heads, col(H_G4), col(H_G6), col(H_M7), col(H_M8), tok,
                  pl.BlockSpec((None, 3, D_MODEL), lambda i: (i // tiles_per_batch, 0, 0)),
                  full((1, HEAD_DIM)), full((D_MODEL, D_MODEL)), full((D_MODEL, D_MODEL)),
                  full((D_MODEL, D_MODEL)), full((1, D_MODEL)), full((1, D_MODEL))],
        out_specs=tok,
        out_shape=jax.ShapeDtypeStruct((n_tok, D_MODEL), F32),
        compiler_params=pltpu.CompilerParams(
            dimension_semantics=("arbitrary",), vmem_limit_bytes=VMEM_LIMIT),
        name="merge",
    )(oa, hx, feat_h, feat_h, feat_h, feat_h, x2, mod3, ng, pa, pb, wo, lg, lbias)


def kernel(x, c, ctx, c_ctx, w_mod, b_mod, w_in, b_in, lb_logits, norm_a_g, conv_w, conv_b,
           w_r, b_r, w_i, b_i, lam, p_a, p_b, w_out, ln_g, ln_b):
    batch, seq, d = x.shape
    n_ctx = ctx.shape[1]
    assert d == D_MODEL and seq % GRID_W == 0 and w_in.shape[0] == 1

    n_blocks = w_in.shape[-1] // D_MODEL
    col_scale = jnp.repeat(jnp.array([0.5 if j in HALVED_BLOCKS else 1.0 for j in range(n_blocks)], F32), D_MODEL)
    w_bf = (w_in[0] * col_scale).astype(BF16)
    b2 = (b_in[0] * col_scale)[None, :]
    wg = (0.5 * jnp.concatenate([w_r[0, 0], w_i[0, 0], w_r[0, 1], w_i[0, 1]], axis=-1)).astype(BF16)
    assert batch < SUBLANES
    mod, lb = _mod_call(c, c_ctx[None, :], w_mod[0], b_mod[0][None, :], lb_logits)
    mod3 = mod.reshape(SUBLANES, 3, D_MODEL)

    x2 = x.reshape(batch * seq, D_MODEL)
    tm = INPROJ_TM
    feat = _inproj_call(x2, mod3, w_bf, b2, LATENT_F32, ("q", "forget0", "forget1", None), lb,
                        out_dtype=F32, tm=tm, tiles_per_mod=seq // tm)
    feat_h = _inproj_call(x2, mod3, w_bf, b2, LATENT_BF16, ("silu", "silu", "sig", "sig", None),
                          out_dtype=BF16, tm=tm, tiles_per_mod=seq // tm, head_major=True)
    feat_c = _inproj_call(ctx.reshape(batch * n_ctx, D_MODEL), mod3[batch:batch + 1], w_bf, b2, CTX_F32,
                          ("forget0", "forget1", None, None), lb,
                          out_dtype=F32, tm=n_ctx, tiles_per_mod=batch)

    cw, cb = conv_w[0], conv_b[0][None, :]
    s0f, s0b, h0f, h0b = _ctx_call(feat_c, cw, cb, wg, b_r[0], b_i[0], lam[0], batch=batch, n_ctx=n_ctx)

    o_b = _gla_call(feat, feat_h, F_F1, s0b, reverse=True, batch=batch, seq=seq, rows=GLA_ROWS)
    oa = _gla_call(feat, feat_h, F_F0, s0f, o_b, reverse=False, batch=batch, seq=seq, rows=GLA_ROWS)
    hx = _lru_call(feat, cw, cb, wg, b_r[0], b_i[0], lam[0], h0f, h0b, batch=batch, seq=seq)

    out = _merge_call(oa, hx, feat_h, x2, mod3, norm_a_g[0][None, :],
                      p_a[0].astype(BF16), p_b[0].astype(BF16), w_out[0].astype(BF16),
                      ln_g[0][None, :], ln_b[0][None, :], tm=MERGE_TM, tiles_per_batch=seq // MERGE_TM)
    return out.reshape(batch, seq, D_MODEL)
```
